```python
import jax, jax.numpy as jnp
from jax import lax
import numpy as np

D_MODEL = 1024
BATCH = 2
SEQ = 8192
DEPTH = 1
DEC_BATCH = 32
DEC_SEQ = 1
PAST_LEN = 16384
PAGE_SIZE = 128

N_HEADS = 8
HEAD_DIM = 64
ATTN_W = N_HEADS * HEAD_DIM
CONV_CH = 512
CONV_K = 31
PLE_DIM = 256
N_GROUPS = 4
EXP_PER_GROUP = 8
N_EXPERTS = N_GROUPS * EXP_PER_GROUP
TOP_K = 2
EXPERT_FF = 256
Q_BLOCK = 128
EPS = 1e-6
FORGET_BIAS_LO = 2.0
FORGET_BIAS_HI = 10.0

O_Q = 0
O_K = O_Q + ATTN_W
O_V = O_K + ATTN_W
O_F = O_V + ATTN_W
O_GLU = O_F + N_HEADS
O_GATE = O_GLU + 2 * CONV_CH
N_IN = O_GATE + 2 * D_MODEL

kernel_name = "fox_conformer_hmoe_decoder_step"


def rmsnorm(x, g):
    xf = x.astype(jnp.float32)
    y = xf * lax.rsqrt(jnp.mean(xf * xf, axis=-1, keepdims=True) + EPS) * g.astype(jnp.float32)
    return y.astype(x.dtype)


def layernorm(x, g, b):
    xf = x.astype(jnp.float32)
    mu = jnp.mean(xf, axis=-1, keepdims=True)
    var = jnp.mean(jnp.square(xf - mu), axis=-1, keepdims=True)
    y = (xf - mu) * lax.rsqrt(var + EPS) * g.astype(jnp.float32) + b.astype(jnp.float32)
    return y.astype(x.dtype)


def mixer_inputs(h, w_in, b_forget, q_g, k_g):
    z = h @ w_in
    B, T, _ = z.shape
    q = rmsnorm(z[..., O_Q:O_K].reshape(B, T, N_HEADS, HEAD_DIM), q_g)
    k = rmsnorm(z[..., O_K:O_V].reshape(B, T, N_HEADS, HEAD_DIM), k_g)
    v = z[..., O_V:O_F].reshape(B, T, N_HEADS, HEAD_DIM)
    logf = jax.nn.log_sigmoid(z[..., O_F:O_GLU].astype(jnp.float32) + b_forget)
    glu_a, glu_b = jnp.split(z[..., O_GLU:O_GATE], 2, axis=-1)
    u = glu_a * jax.nn.sigmoid(glu_b)
    gates = z[..., O_GATE:]
    return q, k, v, logf, u, gates


def fox_attention_prompt(q, k, v, logf):
    B, T, H, Dh = q.shape
    scale = Dh ** -0.5
    c = jnp.cumsum(logf, axis=1).transpose(0, 2, 1)
    key_pos = jnp.arange(T)

    def block(i):
        start = i * Q_BLOCK
        qb = lax.dynamic_slice_in_dim(q, start, Q_BLOCK, axis=1)
        cb = lax.dynamic_slice_in_dim(c, start, Q_BLOCK, axis=2)
        s = jnp.einsum('bqhd,bkhd->bhqk', qb, k).astype(jnp.float32) * scale
        bias = cb[..., :, None] - c[..., None, :]
        qpos = start + jnp.arange(Q_BLOCK)
        mask = key_pos[None, :] <= qpos[:, None]
        p = jax.nn.softmax(jnp.where(mask, s + bias, -jnp.inf), axis=-1).astype(v.dtype)
        return jnp.einsum('bhqk,bkhd->bqhd', p, v)

    out = lax.map(block, jnp.arange(T // Q_BLOCK))
    return out.transpose(1, 0, 2, 3, 4).reshape(B, T, H * Dh)


def fox_attention_sample(q, k_new, v_new, logf_new, k_past, v_past, logf_past):
    Bd, S, H, Dh = q.shape
    P = k_past.shape[1]
    scale = Dh ** -0.5
    lf_past = logf_past.astype(jnp.float32)
    c_past = lf_past - lax.cumsum(lf_past, axis=1, reverse=True)
    c_new = jnp.cumsum(logf_new, axis=1)
    c_all = jnp.concatenate([c_past, c_new], axis=1).transpose(0, 2, 1)
    c_q = c_new.transpose(0, 2, 1)
    k_all = jnp.concatenate([k_past, k_new], axis=1)
    v_all = jnp.concatenate([v_past, v_new], axis=1)
    s = jnp.einsum('bqhd,bkhd->bhqk', q, k_all).astype(jnp.float32) * scale
    bias = c_q[..., :, None] - c_all[..., None, :]
    mask = jnp.arange(P + S)[None, :] <= (P + jnp.arange(S))[:, None]
    p = jax.nn.softmax(jnp.where(mask, s + bias, -jnp.inf), axis=-1).astype(v_all.dtype)
    return jnp.einsum('bhqk,bkhd->bqhd', p, v_all).reshape(Bd, S, H * Dh)


def conformer_conv(u, conv_state, w_dw, b_dw, ln_g, ln_b, w_o):
    full = jnp.concatenate([conv_state.astype(u.dtype), u], axis=1)
    y = lax.conv_general_dilated(full, w_dw[:, None, :].astype(u.dtype), window_strides=(1,),
                                 padding='VALID', dimension_numbers=('NWC', 'WIO', 'NWC'),
                                 feature_group_count=CONV_CH) + b_dw
    y = jax.nn.silu(layernorm(y, ln_g, ln_b))
    return y @ w_o, full[:, -(CONV_K - 1):, :]


def merge_branches(attn, conv_out, gates, w_attn_o, w_out):
    g_attn, g_conv = jnp.split(gates, 2, axis=-1)
    merged = jax.nn.sigmoid(g_attn) * (attn @ w_attn_o) + jax.nn.sigmoid(g_conv) * conv_out
    return merged @ w_out


def hier_moe(h, w_rg, b_rg, w_re, b_re, w_gate, w_up, w_down):
    N = h.shape[0]
    gprob = jax.nn.softmax((h @ w_rg).astype(jnp.float32) + b_rg, axis=-1)
    gval, gidx = lax.top_k(gprob, 1)
    elog = ((h @ w_re).astype(jnp.float32) + b_re).reshape(N, N_GROUPS, EXP_PER_GROUP)
    elog_sel = jnp.take_along_axis(elog, gidx[:, :, None], axis=1)[:, 0]
    ev, ei = lax.top_k(elog_sel, TOP_K)
    ew = jax.nn.softmax(ev, axis=-1) * gval
    expert_id = gidx * EXP_PER_GROUP + ei
    gates = jnp.sum(jax.nn.one_hot(expert_id, N_EXPERTS, dtype=jnp.float32) * ew[..., None], axis=1)
    gates = gates.astype(h.dtype)
    y = jnp.zeros_like(h)
    for e in range(N_EXPERTS):
        he = jax.nn.silu(h @ w_gate[e]) * (h @ w_up[e])
        y = y + gates[:, e:e + 1] * (he @ w_down[e])
    return y


def setup_inputs(seed: int = 0) -> dict:
    key = jax.random.key(seed)
    ks = jax.random.split(key, 40)
    f32 = jnp.float32
    n_pages = PAST_LEN // PAGE_SIZE
    n_used = DEC_BATCH * n_pages
    n_pool = n_used + max(1, n_used // 4)

    def nrm(k, shape, scale=1.0):
        return jax.random.normal(k, shape, f32) * scale

    def gain(k, shape):
        return 1.0 + 0.05 * jax.random.normal(k, shape, f32)

    page_table = jax.random.permutation(ks[0], n_pool)[:n_used].reshape(DEC_BATCH, n_pages).astype(jnp.int32)
    b_forget = jax.random.uniform(ks[11], (DEPTH, N_HEADS), f32, FORGET_BIAS_LO, FORGET_BIAS_HI)
    cache_logf = jax.nn.log_sigmoid(b_forget[:, None, None, :]
                                    + nrm(ks[5], (DEPTH, n_pool, PAGE_SIZE, N_HEADS), 0.5))
    return {
        "x_prompt": nrm(ks[1], (BATCH, SEQ, D_MODEL)),
        "x_sample": nrm(ks[2], (DEC_BATCH, DEC_SEQ, D_MODEL)),
        "cache_k": nrm(ks[3], (DEPTH, n_pool, PAGE_SIZE, N_HEADS, HEAD_DIM)),
        "cache_v": nrm(ks[4], (DEPTH, n_pool, PAGE_SIZE, N_HEADS, HEAD_DIM)),
        "cache_logf": cache_logf,
        "state_conv": nrm(ks[6], (DEPTH, DEC_BATCH, CONV_K - 1, CONV_CH), 0.5),
        "page_table": page_table,
        "p_prompt": nrm(ks[7], (DEPTH, BATCH, SEQ, PLE_DIM)),
        "p_sample": nrm(ks[8], (DEPTH, DEC_BATCH, DEC_SEQ, PLE_DIM)),
        "norm_mix_g": gain(ks[9], (DEPTH, D_MODEL)),
        "w_in": nrm(ks[10], (DEPTH, D_MODEL, N_IN), D_MODEL ** -0.5),
        "b_forget": b_forget,
        "q_norm_g": gain(ks[12], (DEPTH, HEAD_DIM)),
        "k_norm_g": gain(ks[13], (DEPTH, HEAD_DIM)),
        "w_attn_o": nrm(ks[14], (DEPTH, ATTN_W, D_MODEL), ATTN_W ** -0.5),
        "conv_dw_w": nrm(ks[15], (DEPTH, CONV_K, CONV_CH), CONV_K ** -0.5),
        "conv_dw_b": nrm(ks[16], (DEPTH, CONV_CH), 0.02),
        "conv_ln_g": gain(ks[17], (DEPTH, CONV_CH)),
        "conv_ln_b": nrm(ks[18], (DEPTH, CONV_CH), 0.02),
        "w_conv_o": nrm(ks[19], (DEPTH, CONV_CH, D_MODEL), CONV_CH ** -0.5),
        "w_out": nrm(ks[20], (DEPTH, D_MODEL, D_MODEL), D_MODEL ** -0.5),
        "norm_ffn_g": gain(ks[21], (DEPTH, D_MODEL)),
        "w_router_group": nrm(ks[22], (DEPTH, D_MODEL, N_GROUPS), D_MODEL ** -0.5),
        "b_router_group": nrm(ks[23], (DEPTH, N_GROUPS), 0.01),
        "w_router_expert": nrm(ks[24], (DEPTH, D_MODEL, N_EXPERTS), D_MODEL ** -0.5),
        "b_router_expert": nrm(ks[25], (DEPTH, N_EXPERTS), 0.01),
        "w_exp_gate": nrm(ks[26], (DEPTH, N_EXPERTS, D_MODEL, EXPERT_FF), D_MODEL ** -0.5),
        "w_exp_up": nrm(ks[27], (DEPTH, N_EXPERTS, D_MODEL, EXPERT_FF), D_MODEL ** -0.5),
        "w_exp_down": nrm(ks[28], (DEPTH, N_EXPERTS, EXPERT_FF, D_MODEL), EXPERT_FF ** -0.5),
        "norm_ple_g": gain(ks[29], (DEPTH, D_MODEL)),
        "w_ple_gate": nrm(ks[30], (DEPTH, D_MODEL, D_MODEL), D_MODEL ** -0.5),
        "w_ple_proj": nrm(ks[31], (DEPTH, PLE_DIM, D_MODEL), PLE_DIM ** -0.5),
    }


def reference(x_prompt, x_sample, cache_k, cache_v, cache_logf, state_conv, page_table,
              p_prompt, p_sample, norm_mix_g, w_in, b_forget, q_norm_g, k_norm_g, w_attn_o,
              conv_dw_w, conv_dw_b, conv_ln_g, conv_ln_b, w_conv_o, w_out, norm_ffn_g,
              w_router_group, b_router_group, w_router_expert, b_router_expert,
              w_exp_gate, w_exp_up, w_exp_down, norm_ple_g, w_ple_gate, w_ple_proj):
    Bd, S, _ = x_sample.shape
    n_pages = page_table.shape[1]
    past = n_pages * PAGE_SIZE
    xp, xs = x_prompt, x_sample
    nk_p, nv_p, nf_p, nc_p, nk_s, nv_s, nf_s, nc_s = [], [], [], [], [], [], [], []

    def channel_and_ple(x, p, i):
        h = rmsnorm(x, norm_ffn_g[i])
        B, T, _ = h.shape
        x = x + hier_moe(h.reshape(B * T, D_MODEL), w_router_group[i], b_router_group[i],
                         w_router_expert[i], b_router_expert[i], w_exp_gate[i], w_exp_up[i],
                         w_exp_down[i]).reshape(B, T, D_MODEL)
        gate = jax.nn.sigmoid(rmsnorm(x, norm_ple_g[i]) @ w_ple_gate[i])
        return x + gate * (p @ w_ple_proj[i])

    for i in range(DEPTH):
        hp = rmsnorm(xp, norm_mix_g[i])
        q, k, v, lf, u, g = mixer_inputs(hp, w_in[i], b_forget[i], q_norm_g[i], k_norm_g[i])
        attn = fox_attention_prompt(q, k, v, lf)
        zero_state = jnp.zeros((xp.shape[0], CONV_K - 1, CONV_CH), u.dtype)
        conv_out, cst = conformer_conv(u, zero_state, conv_dw_w[i], conv_dw_b[i], conv_ln_g[i],
                                       conv_ln_b[i], w_conv_o[i])
        xp = xp + merge_branches(attn, conv_out, g, w_attn_o[i], w_out[i])
        xp = channel_and_ple(xp, p_prompt[i], i)
        nk_p.append(k); nv_p.append(v); nf_p.append(lf); nc_p.append(cst)

        hs = rmsnorm(xs, norm_mix_g[i])
        q, k, v, lf, u, g = mixer_inputs(hs, w_in[i], b_forget[i], q_norm_g[i], k_norm_g[i])
        k_past = cache_k[i][page_table].reshape(Bd, past, N_HEADS, HEAD_DIM)
        v_past = cache_v[i][page_table].reshape(Bd, past, N_HEADS, HEAD_DIM)
        f_past = cache_logf[i][page_table].reshape(Bd, past, N_HEADS)
        attn = fox_attention_sample(q, k, v, lf, k_past, v_past, f_past)
        conv_out, cst = conformer_conv(u, state_conv[i], conv_dw_w[i], conv_dw_b[i], conv_ln_g[i],
                                       conv_ln_b[i], w_conv_o[i])
        xs = xs + merge_branches(attn, conv_out, g, w_attn_o[i], w_out[i])
        xs = channel_and_ple(xs, p_sample[i], i)
        nk_s.append(k); nv_s.append(v); nf_s.append(lf); nc_s.append(cst)

    new_k_prompt = jnp.stack(nk_p, axis=0)
    new_v_prompt = jnp.stack(nv_p, axis=0)
    new_logf_prompt = jnp.stack(nf_p, axis=0)
    new_conv_prompt = jnp.stack(nc_p, axis=0)
    new_k_sample = jnp.stack(nk_s, axis=0)
    new_v_sample = jnp.stack(nv_s, axis=0)
    new_logf_sample = jnp.stack(nf_s, axis=0)
    new_conv_sample = jnp.stack(nc_s, axis=0)
    return (xp, xs, new_k_prompt, new_v_prompt, new_logf_prompt, new_conv_prompt,
            new_k_sample, new_v_sample, new_logf_sample, new_conv_sample)
```

```python
import functools

import jax
import jax.numpy as jnp
from jax import lax
from jax.experimental import pallas as pl
from jax.experimental.pallas import tpu as pltpu

F32 = jnp.float32
BF16 = jnp.bfloat16

N_HEADS = 8
HEAD_DIM = 64
ATTN_W = N_HEADS * HEAD_DIM
CONV_CH = 512
CONV_K = 31
N_GROUPS = 4
EXP_PER_GROUP = 8
N_EXPERTS = N_GROUPS * EXP_PER_GROUP
EXPERT_FF = 256
PAGE_SIZE = 128
EPS = 1e-6

LANES = 128
CONV_HALO = 32
VMEM_LIMIT = 56 * 1024 * 1024
EXPERT_LANE0 = 32


def _const_spec(shape):
    nd = len(shape)
    return pl.BlockSpec(shape, lambda *_: (0,) * nd, pipeline_mode=pl.Buffered(1))


def _sigmoid(x):
    return 1.0 / (1.0 + jnp.exp(-x))


def _log_sigmoid(x):
    return -(jnp.maximum(-x, 0.0) + jnp.log1p(jnp.exp(-jnp.abs(x))))


def _rms(x, g):
    return x * lax.rsqrt(jnp.mean(x * x, axis=-1, keepdims=True) + EPS) * g


def _dot(a, b):
    return jnp.dot(a, b, preferred_element_type=F32)


def _project(x, gmix, wqkv, wf, wglu, wgate, bf, qg, kg, gsum):
    hb = _rms(x, gmix).astype(BF16)
    zqkv = _dot(hb, wqkv)
    zq = zqkv[:, :ATTN_W]
    zk = zqkv[:, ATTN_W:2 * ATTN_W]
    zv = zqkv[:, 2 * ATTN_W:]

    def head_norm(z, g):
        ss = _dot((z * z).astype(BF16), gsum)
        return z * lax.rsqrt(ss * (1.0 / HEAD_DIM) + EPS) * g

    q = head_norm(zq, qg) * (HEAD_DIM ** -0.5)
    k = head_norm(zk, kg)
    zf = _dot(hb, wf) + bf
    lane = lax.broadcasted_iota(jnp.int32, zf.shape, 1)
    logf = jnp.where(lane < N_HEADS, _log_sigmoid(zf), 0.0)
    zglu = _dot(hb, wglu)
    u = zglu[:, :CONV_CH] * _sigmoid(zglu[:, CONV_CH:])
    sg = _sigmoid(_dot(hb, wgate))
    return q, k, zv, logf, u, sg


def _ln_silu(y, g, b):
    mu = jnp.mean(y, axis=-1, keepdims=True)
    d = y - mu
    var = jnp.mean(d * d, axis=-1, keepdims=True)
    z = d * lax.rsqrt(var + EPS) * g + b
    return z * _sigmoid(z)


def _split3(x):
    a = x.astype(BF16)
    r = x - a.astype(F32)
    b = r.astype(BF16)
    c = (r - b.astype(F32)).astype(BF16)
    return a, b, c


def _proj_prompt_kernel(x_ref, gmix_ref, wqkv_ref, wf_ref, wglu_ref, wgate_ref, bf_ref, qg_ref, kg_ref,
                        gsum_ref, tri_ref, wdw_ref, bdw_ref, lng_ref, lnb_ref,
                        qb_ref, kb_ref, vb_ref, k32_ref, v32_ref, lf_ref, c_ref, y_ref, sg_ref, utail_ref,
                        ext_ref, carry_ref, *, tiles_per_seq, tm):
    i = pl.program_id(0)

    @pl.when(i % tiles_per_seq == 0)
    def _():
        ext_ref[0:CONV_HALO, :] = jnp.zeros((CONV_HALO, CONV_CH), F32)
        carry_ref[...] = jnp.zeros_like(carry_ref)

    q, k, v, logf, u, sg = _project(x_ref[...], gmix_ref[...], wqkv_ref[...], wf_ref[...], wglu_ref[...],
                                    wgate_ref[...], bf_ref[...], qg_ref[...], kg_ref[...], gsum_ref[...])
    qb_ref[...] = q.astype(BF16)
    kb_ref[...] = k.astype(BF16)
    vb_ref[...] = v.astype(BF16)
    k32_ref[...] = k
    v32_ref[...] = v
    sg_ref[...] = sg
    lf_ref[...] = logf[:, :N_HEADS]

    tri = tri_ref[...]
    a, b, c3 = _split3(logf)
    c = _dot(tri, a) + _dot(tri, b) + _dot(tri, c3) + carry_ref[0:1, :]
    c_ref[...] = c[:, :N_HEADS]
    carry_ref[0:1, :] = c[tm - 1:tm, :]

    ext_ref[CONV_HALO:CONV_HALO + tm, :] = u
    off = CONV_HALO - (CONV_K - 1)
    rc = min(tm, 128)
    for r0 in range(0, tm, rc):
        for l0 in range(0, CONV_CH, LANES):
            acc = jnp.zeros((rc, LANES), F32)
            for j in range(CONV_K):
                acc = acc + ext_ref[off + j + r0:off + j + r0 + rc, l0:l0 + LANES] * wdw_ref[j:j + 1, l0:l0 + LANES]
            ext_ref[CONV_HALO + tm + r0:CONV_HALO + tm + r0 + rc, l0:l0 + LANES] = acc
    yc = ext_ref[CONV_HALO + tm:CONV_HALO + 2 * tm, :] + bdw_ref[...]
    y_ref[...] = _ln_silu(yc, lng_ref[...], lnb_ref[...]).astype(BF16)
    tail = u[tm - CONV_HALO:, :]
    ext_ref[0:CONV_HALO, :] = tail
    utail_ref[0] = tail


def _proj_sample_kernel(x_ref, gmix_ref, wqkv_ref, wf_ref, wglu_ref, wgate_ref, bf_ref, qg_ref, kg_ref,
                        gsum_ref, st_ref, wdw_ref, bdw_ref, lng_ref, lnb_ref,
                        qb_ref, k32_ref, v32_ref, lf_ref, u_ref, y_ref, sg_ref):
    q, k, v, logf, u, sg = _project(x_ref[...], gmix_ref[...], wqkv_ref[...], wf_ref[...], wglu_ref[...],
                                    wgate_ref[...], bf_ref[...], qg_ref[...], kg_ref[...], gsum_ref[...])
    qb_ref[...] = q.astype(BF16)
    k32_ref[...] = k
    v32_ref[...] = v
    sg_ref[...] = sg
    lf_ref[...] = logf[:, :N_HEADS]
    u_ref[...] = u
    acc = u * wdw_ref[CONV_K - 1:CONV_K, :]
    for j in range(CONV_K - 1):
        acc = acc + st_ref[j] * wdw_ref[j:j + 1, :]
    y_ref[...] = _ln_silu(acc + bdw_ref[...], lng_ref[...], lnb_ref[...]).astype(BF16)


def _prep_proj_weights(norm_mix_g, w_in, b_forget, q_norm_g, k_norm_g):
    d = w_in.shape[0]
    o_f = 3 * ATTN_W
    o_glu = o_f + N_HEADS
    o_gate = o_glu + 2 * CONV_CH
    wqkv = w_in[:, :o_f].astype(BF16)
    wf = jnp.pad(w_in[:, o_f:o_glu], ((0, 0), (0, LANES - N_HEADS))).astype(BF16)
    wglu = w_in[:, o_glu:o_gate].astype(BF16)
    wgate = w_in[:, o_gate:].astype(BF16)
    bf = jnp.pad(b_forget, (0, LANES - N_HEADS)).reshape(1, LANES)
    qg = jnp.tile(q_norm_g, N_HEADS).reshape(1, ATTN_W)
    kg = jnp.tile(k_norm_g, N_HEADS).reshape(1, ATTN_W)
    hid = jnp.arange(ATTN_W) // HEAD_DIM
    gsum = (hid[:, None] == hid[None, :]).astype(BF16)
    return (norm_mix_g.reshape(1, d), wqkv, wf, wglu, wgate, bf, qg, kg, gsum)


def _proj_prompt(x2d, pw, conv_w, batch, seq, tm):
    n, d = x2d.shape
    tps = seq // tm
    tri = (jnp.arange(tm)[:, None] >= jnp.arange(tm)[None, :]).astype(BF16)
    wdw, bdw, lng, lnb = conv_w
    row = lambda w: pl.BlockSpec((tm, w), lambda i: (i, 0))
    in_specs = ([row(d)] + [_const_spec(a.shape) for a in pw] + [_const_spec(tri.shape)]
                + [_const_spec(a.shape) for a in conv_w])
    out_shape = (
        jax.ShapeDtypeStruct((n, ATTN_W), BF16), jax.ShapeDtypeStruct((n, ATTN_W), BF16),
        jax.ShapeDtypeStruct((n, ATTN_W), BF16), jax.ShapeDtypeStruct((n, ATTN_W), F32),
        jax.ShapeDtypeStruct((n, ATTN_W), F32), jax.ShapeDtypeStruct((n, N_HEADS), F32),
        jax.ShapeDtypeStruct((n, N_HEADS), F32), jax.ShapeDtypeStruct((n, CONV_CH), BF16),
        jax.ShapeDtypeStruct((n, 2 * d), F32), jax.ShapeDtypeStruct((batch, CONV_HALO, CONV_CH), F32),
    )
    out_specs = (row(ATTN_W), row(ATTN_W), row(ATTN_W), row(ATTN_W), row(ATTN_W), row(N_HEADS), row(N_HEADS),
                 row(CONV_CH), row(2 * d), pl.BlockSpec((1, CONV_HALO, CONV_CH), lambda i: (i // tps, 0, 0)))
    return pl.pallas_call(
        functools.partial(_proj_prompt_kernel, tiles_per_seq=tps, tm=tm),
        grid=(n // tm,), in_specs=in_specs, out_specs=out_specs, out_shape=out_shape,
        scratch_shapes=[pltpu.VMEM((CONV_HALO + 2 * tm, CONV_CH), F32), pltpu.VMEM((8, LANES), F32)],
        compiler_params=pltpu.CompilerParams(dimension_semantics=("arbitrary",), vmem_limit_bytes=VMEM_LIMIT),
        name="proj_prompt",
    )(x2d, *pw, tri, wdw, bdw, lng, lnb)


def _proj_sample(x2d, pw, state_t, conv_w):
    n, d = x2d.shape
    args = (x2d,) + tuple(pw) + (state_t,) + tuple(conv_w)
    full = lambda a: pl.BlockSpec(a.shape, lambda i, nd=a.ndim: (0,) * nd)
    out_shape = (
        jax.ShapeDtypeStruct((n, ATTN_W), BF16), jax.ShapeDtypeStruct((n, ATTN_W), F32),
        jax.ShapeDtypeStruct((n, ATTN_W), F32), jax.ShapeDtypeStruct((n, N_HEADS), F32),
        jax.ShapeDtypeStruct((n, CONV_CH), F32), jax.ShapeDtypeStruct((n, CONV_CH), BF16),
        jax.ShapeDtypeStruct((n, 2 * d), F32),
    )
    return pl.pallas_call(
        _proj_sample_kernel, grid=(1,), in_specs=[full(a) for a in args],
        out_specs=tuple(full(s) for s in out_shape), out_shape=out_shape,
        compiler_params=pltpu.CompilerParams(dimension_semantics=("arbitrary",), vmem_limit_bytes=VMEM_LIMIT),
        name="proj_sample",
    )(*args)


def _attn_prompt_kernel(q_ref, k_ref, v_ref, cq_ref, ck_ref, o_ref, *, tq):
    i = pl.program_id(2)
    lane = lax.broadcasted_iota(jnp.int32, (1, LANES), 1)
    row = lax.broadcasted_iota(jnp.int32, (tq, tq), 0)
    col = lax.broadcasted_iota(jnp.int32, (tq, tq), 1)
    q2 = q_ref[0]
    out = jnp.zeros((tq, LANES), F32)
    for h in range(2):
        in_head = (lane // HEAD_DIM) == h
        q = jnp.where(in_head, q2, jnp.zeros_like(q2))
        cq = cq_ref[0, 0][:, h:h + 1]

        def scores(j):
            start = pl.multiple_of(j * tq, tq)
            ks = k_ref[0, pl.ds(start, tq), :]
            s = lax.dot_general(q, ks, (((1,), (1,)), ((), ())), preferred_element_type=F32)
            ck = ck_ref[0, 0, h:h + 1, pl.ds(start, tq)]
            return s + (cq - ck), start

        def update(carry, s, start):
            m, l, acc = carry
            m_new = jnp.maximum(m, jnp.max(s, axis=-1, keepdims=True))
            alpha = jnp.exp(m - m_new)
            p = jnp.exp(s - m_new)
            l = alpha * l + jnp.sum(p, axis=-1, keepdims=True)
            vs = v_ref[0, pl.ds(start, tq), :]
            acc = alpha * acc + _dot(p.astype(BF16), vs)
            return m_new, l, acc

        def body(j, carry):
            s, start = scores(j)
            return update(carry, s, start)

        init = (jnp.full((tq, 1), -jnp.inf, F32), jnp.zeros((tq, 1), F32), jnp.zeros((tq, LANES), F32))
        carry = lax.fori_loop(0, i, body, init)
        s, start = scores(i)
        s = jnp.where(col <= row, s, -jnp.inf)
        m, l, acc = update(carry, s, start)
        out = jnp.where(in_head, acc / l, out)
    o_ref[0] = out.astype(BF16)


def _attn_prompt(qb, kb, vb, c, batch, seq, tq):
    hp = N_HEADS // 2
    q3 = qb.reshape(batch, seq, ATTN_W)
    k3 = kb.reshape(batch, seq, ATTN_W)
    v3 = vb.reshape(batch, seq, ATTN_W)
    c4 = c.reshape(batch, seq, hp, 2)
    cq = c4.transpose(0, 2, 1, 3)
    ck = c4.transpose(0, 2, 3, 1)
    return pl.pallas_call(
        functools.partial(_attn_prompt_kernel, tq=tq),
        grid=(batch, hp, seq // tq),
        in_specs=[
            pl.BlockSpec((1, tq, LANES), lambda b, h, i: (b, i, h)),
            pl.BlockSpec((1, seq, LANES), lambda b, h, i: (b, 0, h)),
            pl.BlockSpec((1, seq, LANES), lambda b, h, i: (b, 0, h)),
            pl.BlockSpec((1, 1, tq, 2), lambda b, h, i: (b, h, i, 0)),
            pl.BlockSpec((1, 1, 2, seq), lambda b, h, i: (b, h, 0, 0)),
        ],
        out_specs=pl.BlockSpec((1, tq, LANES), lambda b, h, i: (b, i, h)),
        out_shape=jax.ShapeDtypeStruct((batch, seq, ATTN_W), BF16),
        compiler_params=pltpu.CompilerParams(dimension_semantics=("arbitrary", "arbitrary", "arbitrary"),
                                             vmem_limit_bytes=VMEM_LIMIT),
        name="attn_prompt",
    )(q3, k3, v3, cq, ck)


def _attn_sample_kernel(pt_ref, q_ref, kn_ref, vn_ref, lfn_ref, *refs, pp):
    k_refs = refs[:pp]
    v_refs = refs[pp:2 * pp]
    f_refs = refs[2 * pp:3 * pp]
    o_ref = refs[3 * pp]
    m_ref, l_ref, r_ref, acc_ref = refs[3 * pp + 1:]
    j = pl.program_id(1)
    hrow = lax.broadcasted_iota(jnp.int32, (N_HEADS, ATTN_W), 0)
    hlane = lax.broadcasted_iota(jnp.int32, (N_HEADS, ATTN_W), 1) // HEAD_DIM
    diag = hrow == hlane
    qf = jnp.where(diag, q_ref[0].astype(F32), 0.0)
    qm = qf.astype(BF16)

    @pl.when(j == 0)
    def _():
        m_ref[...] = jnp.sum(qf * kn_ref[0], axis=-1, keepdims=True)
        l_ref[...] = jnp.ones_like(l_ref)
        r_ref[...] = lfn_ref[0]
        acc_ref[...] = jnp.broadcast_to(vn_ref[0], (N_HEADS, ATTN_W))

    srow = lax.broadcasted_iota(jnp.int32, (PAGE_SIZE, 2 * PAGE_SIZE), 0)
    scol = lax.broadcasted_iota(jnp.int32, (PAGE_SIZE, 2 * PAGE_SIZE), 1)
    suffix = jnp.where((srow > scol) | (scol >= PAGE_SIZE), 1.0, 0.0).astype(F32)

    for t in range(pp):
        lf = f_refs[t][0]
        sx = lax.dot_general(lf, suffix, (((0,), (0,)), ((), ())), precision=lax.Precision.HIGHEST,
                             preferred_element_type=F32)
        r = r_ref[...]
        bias = r + sx[:, :PAGE_SIZE]
        r_ref[...] = r + sx[:, PAGE_SIZE:PAGE_SIZE + 1]
        kb = k_refs[t][0].astype(BF16)
        s = lax.dot_general(qm, kb, (((1,), (1,)), ((), ())), preferred_element_type=F32) + bias
        m = m_ref[...]
        m_new = jnp.maximum(m, jnp.max(s, axis=-1, keepdims=True))
        alpha = jnp.exp(m - m_new)
        p = jnp.exp(s - m_new)
        l_ref[...] = alpha * l_ref[...] + jnp.sum(p, axis=-1, keepdims=True)
        m_ref[...] = m_new
        acc_ref[...] = alpha * acc_ref[...] + _dot(p.astype(BF16), v_refs[t][0].astype(BF16))

    @pl.when(j == pl.num_programs(1) - 1)
    def _():
        o = jnp.where(diag, acc_ref[...] / l_ref[...], 0.0)
        o_ref[0] = jnp.sum(o, axis=0, keepdims=True).astype(BF16)


def _attn_sample(page_table, qb, k_new, v_new, lf_new, cache_k, cache_v, cache_logf, pp):
    bd, n_pages = page_table.shape
    n_pool = cache_k.shape[0]
    kc = cache_k.reshape(n_pool, PAGE_SIZE, ATTN_W)
    vc = cache_v.reshape(n_pool, PAGE_SIZE, ATTN_W)
    steps = n_pages // pp

    def page_spec(t, width):
        return pl.BlockSpec((1, PAGE_SIZE, width),
                            lambda b, j, pt: (pt[b, n_pages - 1 - (j * pp + t)], 0, 0))

    per_seq = lambda shape: pl.BlockSpec((1,) + shape, lambda b, j, pt: (b, 0, 0))
    in_specs = ([per_seq((1, ATTN_W)), per_seq((1, ATTN_W)), per_seq((1, ATTN_W)), per_seq((N_HEADS, 1))]
                + [page_spec(t, ATTN_W) for t in range(pp)] * 2
                + [page_spec(t, N_HEADS) for t in range(pp)])
    grid_spec = pltpu.PrefetchScalarGridSpec(
        num_scalar_prefetch=1, grid=(bd, steps), in_specs=in_specs,
        out_specs=per_seq((1, ATTN_W)),
        scratch_shapes=[pltpu.VMEM((N_HEADS, 1), F32), pltpu.VMEM((N_HEADS, 1), F32),
                        pltpu.VMEM((N_HEADS, 1), F32), pltpu.VMEM((N_HEADS, ATTN_W), F32)],
    )
    out = pl.pallas_call(
        functools.partial(_attn_sample_kernel, pp=pp), grid_spec=grid_spec,
        out_shape=jax.ShapeDtypeStruct((bd, 1, ATTN_W), BF16),
        compiler_params=pltpu.CompilerParams(dimension_semantics=("arbitrary", "arbitrary"),
                                             vmem_limit_bytes=VMEM_LIMIT),
        name="attn_sample",
    )(page_table, qb.reshape(bd, 1, ATTN_W), k_new.reshape(bd, 1, ATTN_W), v_new.reshape(bd, 1, ATTN_W),
      lf_new.reshape(bd, N_HEADS, 1), *([kc] * pp), *([vc] * pp), *([cache_logf] * pp))
    return out.reshape(bd, ATTN_W)


def _mix_kernel(x_ref, a_ref, y_ref, sg_ref, wao_ref, wco_ref, wout_ref, gffn_ref, wr_ref, br_ref,
                x1_ref, h2_ref, gates_ref):
    d = x_ref.shape[-1]
    ao = _dot(a_ref[...], wao_ref[...])
    co = _dot(y_ref[...], wco_ref[...])
    merged = sg_ref[:, :d] * ao + sg_ref[:, d:] * co
    x1 = x_ref[...] + _dot(merged.astype(BF16), wout_ref[...])
    x1_ref[...] = x1
    h2 = _rms(x1, gffn_ref[...]).astype(BF16)
    h2_ref[...] = h2

    logits = _dot(h2, wr_ref[...]) + br_ref[...]
    lane = lax.broadcasted_iota(jnp.int32, logits.shape, 1)
    neg = -jnp.inf
    big = jnp.int32(4 * LANES)
    gl = jnp.where(lane < N_GROUPS, logits, neg)
    gmax = jnp.max(gl, axis=-1, keepdims=True)
    gval = 1.0 / jnp.sum(jnp.exp(gl - gmax), axis=-1, keepdims=True)
    gidx = jnp.min(jnp.where(gl == gmax, lane, big), axis=-1, keepdims=True)
    ex = lane - EXPERT_LANE0
    in_group = (ex >= gidx * EXP_PER_GROUP) & (ex < (gidx + 1) * EXP_PER_GROUP)
    el = jnp.where(in_group, logits, neg)
    v1 = jnp.max(el, axis=-1, keepdims=True)
    i1 = jnp.min(jnp.where(el == v1, lane, big), axis=-1, keepdims=True)
    el2 = jnp.where(lane == i1, neg, el)
    v2 = jnp.max(el2, axis=-1, keepdims=True)
    i2 = jnp.min(jnp.where(el2 == v2, lane, big), axis=-1, keepdims=True)
    e2 = jnp.exp(v2 - v1)
    w1 = gval / (1.0 + e2)
    w2 = gval * e2 / (1.0 + e2)
    gates_ref[...] = jnp.where(lane == i1, w1, 0.0) + jnp.where(lane == i2, w2, 0.0)


def _mix(x2d, attn, y, sg, mw, tm):
    n, d = x2d.shape
    row = lambda w: pl.BlockSpec((tm, w), lambda i: (i, 0))
    return pl.pallas_call(
        _mix_kernel, grid=(n // tm,),
        in_specs=[row(d), row(ATTN_W), row(CONV_CH), row(2 * d)] + [_const_spec(a.shape) for a in mw],
        out_specs=(row(d), row(d), row(LANES)),
        out_shape=(jax.ShapeDtypeStruct((n, d), F32), jax.ShapeDtypeStruct((n, d), BF16),
                   jax.ShapeDtypeStruct((n, LANES), F32)),
        compiler_params=pltpu.CompilerParams(dimension_semantics=("arbitrary",), vmem_limit_bytes=VMEM_LIMIT),
        name="mix",
    )(x2d, attn, y, sg, *mw)


def _moe_kernel(x1_ref, h2_ref, gates_ref, p_ref, wg_ref, wu_ref, wd_ref, gple_ref, wpg_ref, wpp_ref,
                o_ref, acc_ref, *, eb):
    jb = pl.program_id(1)

    @pl.when(jb == 0)
    def _():
        acc_ref[...] = jnp.zeros_like(acc_ref)

    h2 = h2_ref[...]
    gates = gates_ref[...]
    lane = lax.broadcasted_iota(jnp.int32, gates.shape, 1)
    for e in range(eb):
        gcol = jnp.sum(jnp.where(lane == EXPERT_LANE0 + jb * eb + e, gates, 0.0), axis=-1, keepdims=True)
        hg = _dot(h2, wg_ref[e])
        hu = _dot(h2, wu_ref[e])
        he = (hg * _sigmoid(hg)) * hu * gcol
        acc_ref[...] += _dot(he.astype(BF16), wd_ref[e])

    @pl.when(jb == pl.num_programs(1) - 1)
    def _():
        x2 = x1_ref[...] + acc_ref[...]
        g = _sigmoid(_dot(_rms(x2, gple_ref[...]).astype(BF16), wpg_ref[...]))
        o_ref[...] = x2 + g * _dot(p_ref[...].astype(BF16), wpp_ref[...])


def _moe_ple(x1, h2, gates, p2d, ew, pw, tm, eb):
    n, d = x1.shape
    wg, wu, wd = ew
    row = lambda w: pl.BlockSpec((tm, w), lambda i, j: (i, 0))
    return pl.pallas_call(
        functools.partial(_moe_kernel, eb=eb), grid=(n // tm, N_EXPERTS // eb),
        in_specs=[row(d), row(d), row(LANES), row(p2d.shape[1]),
                  pl.BlockSpec((eb, d, EXPERT_FF), lambda i, j: (j, 0, 0)),
                  pl.BlockSpec((eb, d, EXPERT_FF), lambda i, j: (j, 0, 0)),
                  pl.BlockSpec((eb, EXPERT_FF, d), lambda i, j: (j, 0, 0))]
                 + [_const_spec(a.shape) for a in pw],
        out_specs=row(d), out_shape=jax.ShapeDtypeStruct((n, d), F32),
        scratch_shapes=[pltpu.VMEM((tm, d), F32)],
        compiler_params=pltpu.CompilerParams(dimension_semantics=("arbitrary", "arbitrary"),
                                             vmem_limit_bytes=VMEM_LIMIT),
        name="moe_ple",
    )(x1, h2, gates, p2d, wg, wu, wd, *pw)


def _tile(n, pref):
    return pref if n % pref == 0 else n


def kernel(x_prompt, x_sample, cache_k, cache_v, cache_logf, state_conv, page_table, p_prompt, p_sample, norm_mix_g, w_in, b_forget, q_norm_g, k_norm_g, w_attn_o, conv_dw_w, conv_dw_b, conv_ln_g, conv_ln_b, w_conv_o, w_out, norm_ffn_g, w_router_group, b_router_group, w_router_expert, b_router_expert, w_exp_gate, w_exp_up, w_exp_down, norm_ple_g, w_ple_gate, w_ple_proj):
    depth = w_in.shape[0]
    assert depth == 1
    li = 0
    batch, seq, d = x_prompt.shape
    bd, dec_seq, _ = x_sample.shape
    assert dec_seq == 1
    n_pages = page_table.shape[1]

    pw = _prep_proj_weights(norm_mix_g[li], w_in[li], b_forget[li], q_norm_g[li], k_norm_g[li])
    conv_w = (conv_dw_w[li], conv_dw_b[li].reshape(1, CONV_CH), conv_ln_g[li].reshape(1, CONV_CH),
              conv_ln_b[li].reshape(1, CONV_CH))
    wr = jnp.zeros((d, LANES), F32)
    wr = wr.at[:, :N_GROUPS].set(w_router_group[li]).at[:, EXPERT_LANE0:EXPERT_LANE0 + N_EXPERTS].set(
        w_router_expert[li]).astype(BF16)
    br = jnp.zeros((1, LANES), F32)
    br = br.at[0, :N_GROUPS].set(b_router_group[li]).at[0, EXPERT_LANE0:EXPERT_LANE0 + N_EXPERTS].set(
        b_router_expert[li])
    mw = (w_attn_o[li].astype(BF16), w_conv_o[li].astype(BF16), w_out[li].astype(BF16),
          norm_ffn_g[li].reshape(1, d), wr, br)
    ew = (w_exp_gate[li].astype(BF16), w_exp_up[li].astype(BF16), w_exp_down[li].astype(BF16))
    plew = (norm_ple_g[li].reshape(1, d), w_ple_gate[li].astype(BF16), w_ple_proj[li].astype(BF16))

    xp = x_prompt.reshape(batch * seq, d)
    tm = _tile(seq, 512)
    qb, kb, vb, k32, v32, lf, c, y, sg, utail = _proj_prompt(xp, pw, conv_w, batch, seq, tm)
    attn = _attn_prompt(qb, kb, vb, c, batch, seq, _tile(seq, 512)).reshape(batch * seq, ATTN_W)
    x1, h2, gates = _mix(xp, attn, y, sg, mw, tm)
    yp = _moe_ple(x1, h2, gates, p_prompt[li].reshape(batch * seq, -1), ew, plew, _tile(batch * seq, 1024), 4)
    y_prompt = yp.reshape(batch, seq, d)
    new_k_prompt = k32.reshape(1, batch, seq, N_HEADS, HEAD_DIM)
    new_v_prompt = v32.reshape(1, batch, seq, N_HEADS, HEAD_DIM)
    new_logf_prompt = lf.reshape(1, batch, seq, N_HEADS)
    new_conv_prompt = utail[:, CONV_HALO - (CONV_K - 1):, :].reshape(1, batch, CONV_K - 1, CONV_CH)

    xs = x_sample.reshape(bd, d)
    state_t = state_conv[li].transpose(1, 0, 2)
    qs, ks, vs, lfs, us, ys, sgs = _proj_sample(xs, pw, state_t, conv_w)
    attn_s = _attn_sample(page_table, qs, ks, vs, lfs, cache_k[li], cache_v[li], cache_logf[li],
                          8 if n_pages % 8 == 0 else 1)
    x1s, h2s, gates_s = _mix(xs, attn_s, ys, sgs, mw, bd)
    ysm = _moe_ple(x1s, h2s, gates_s, p_sample[li].reshape(bd, -1), ew, plew, bd, 4)
    y_sample = ysm.reshape(bd, 1, d)
    new_k_sample = ks.reshape(1, bd, 1, N_HEADS, HEAD_DIM)
    new_v_sample = vs.reshape(1, bd, 1, N_HEADS, HEAD_DIM)
    new_logf_sample = lfs.reshape(1, bd, 1, N_HEADS)
    new_conv_sample = jnp.concatenate([state_conv[li][:, 1:, :], us[:, None, :]], axis=1)[None]

    return (y_prompt, y_sample, new_k_prompt, new_v_prompt, new_logf_prompt, new_conv_prompt,
            new_k_sample, new_v_sample, new_logf_sample, new_conv_sample)
```

```python
import functools

import jax
import jax.numpy as jnp
from jax import lax
from jax.experimental import pallas as pl
from jax.experimental.pallas import tpu as pltpu

F32 = jnp.float32
BF16 = jnp.bfloat16

N_HEADS = 8
HEAD_DIM = 64
ATTN_W = N_HEADS * HEAD_DIM
CONV_CH = 512
CONV_K = 31
N_GROUPS = 4
EXP_PER_GROUP = 8
N_EXPERTS = N_GROUPS * EXP_PER_GROUP
EXPERT_FF = 256
PAGE_SIZE = 128
EPS = 1e-6

LANES = 128
CONV_HALO = 32
VMEM_LIMIT = 56 * 1024 * 1024
EXPERT_LANE0 = 32
LOG2E = 1.4426950408889634


def _const_spec(shape):
    nd = len(shape)
    return pl.BlockSpec(shape, lambda *_: (0,) * nd, pipeline_mode=pl.Buffered(1))


def _sigmoid(x):
    return 1.0 / (1.0 + jnp.exp(-x))


def _log_sigmoid(x):
    return -(jnp.maximum(-x, 0.0) + jnp.log1p(jnp.exp(-jnp.abs(x))))


def _rms(x, g):
    return x * lax.rsqrt(jnp.mean(x * x, axis=-1, keepdims=True) + EPS) * g


def _dot(a, b):
    return jnp.dot(a, b, preferred_element_type=F32)


def _project(x, gmix, wqkv, wf, wglu, wgate, bf, qg, kg, gsum):
    hb = _rms(x, gmix).astype(BF16)
    zqkv = _dot(hb, wqkv)
    zq = zqkv[:, :ATTN_W]
    zk = zqkv[:, ATTN_W:2 * ATTN_W]
    zv = zqkv[:, 2 * ATTN_W:]

    def head_norm(z, g):
        ss = _dot((z * z).astype(BF16), gsum)
        return z * lax.rsqrt(ss * (1.0 / HEAD_DIM) + EPS) * g

    q = head_norm(zq, qg) * (HEAD_DIM ** -0.5 * LOG2E)
    k = head_norm(zk, kg)
    zf = _dot(hb, wf) + bf
    lane = lax.broadcasted_iota(jnp.int32, zf.shape, 1)
    logf = jnp.where(lane < N_HEADS, _log_sigmoid(zf), 0.0)
    zglu = _dot(hb, wglu)
    u = zglu[:, :CONV_CH] * _sigmoid(zglu[:, CONV_CH:])
    sg = _sigmoid(_dot(hb, wgate))
    return q, k, zv, logf, u, sg


def _ln_silu(y, g, b):
    mu = jnp.mean(y, axis=-1, keepdims=True)
    d = y - mu
    var = jnp.mean(d * d, axis=-1, keepdims=True)
    z = d * lax.rsqrt(var + EPS) * g + b
    return z * _sigmoid(z)


def _split3(x):
    a = x.astype(BF16)
    r = x - a.astype(F32)
    b = r.astype(BF16)
    c = (r - b.astype(F32)).astype(BF16)
    return a, b, c


def _proj_prompt_kernel(x_ref, gmix_ref, wqkv_ref, wf_ref, wglu_ref, wgate_ref, bf_ref, qg_ref, kg_ref,
                        gsum_ref, tri_ref, wdw_ref, bdw_ref, lng_ref, lnb_ref,
                        qb_ref, kb_ref, vb_ref, k32_ref, v32_ref, lf_ref, c_ref, y_ref, sg_ref, utail_ref,
                        ext_ref, carry_ref, *, tiles_per_seq, tm):
    i = pl.program_id(0)

    @pl.when(i % tiles_per_seq == 0)
    def _():
        ext_ref[0:CONV_HALO, :] = jnp.zeros((CONV_HALO, CONV_CH), F32)
        carry_ref[...] = jnp.zeros_like(carry_ref)

    q, k, v, logf, u, sg = _project(x_ref[...], gmix_ref[...], wqkv_ref[...], wf_ref[...], wglu_ref[...],
                                    wgate_ref[...], bf_ref[...], qg_ref[...], kg_ref[...], gsum_ref[...])
    qb_ref[...] = q.astype(BF16)
    kb_ref[...] = k.astype(BF16)
    vb_ref[...] = v.astype(BF16)
    k32_ref[...] = k
    v32_ref[...] = v
    sg_ref[...] = sg
    lf_ref[...] = logf[:, :N_HEADS]

    tri = tri_ref[...]
    a, b, c3 = _split3(logf)
    c = _dot(tri, a) + _dot(tri, b) + _dot(tri, c3) + carry_ref[0:1, :]
    c_ref[...] = c[:, :N_HEADS]
    carry_ref[0:1, :] = c[tm - 1:tm, :]

    ext_ref[CONV_HALO:CONV_HALO + tm, :] = u
    off = CONV_HALO - (CONV_K - 1)
    rc = min(tm, 128)
    for r0 in range(0, tm, rc):
        for l0 in range(0, CONV_CH, LANES):
            acc = jnp.zeros((rc, LANES), F32)
            for j in range(CONV_K):
                acc = acc + ext_ref[off + j + r0:off + j + r0 + rc, l0:l0 + LANES] * wdw_ref[j:j + 1, l0:l0 + LANES]
            ext_ref[CONV_HALO + tm + r0:CONV_HALO + tm + r0 + rc, l0:l0 + LANES] = acc
    yc = ext_ref[CONV_HALO + tm:CONV_HALO + 2 * tm, :] + bdw_ref[...]
    y_ref[...] = _ln_silu(yc, lng_ref[...], lnb_ref[...]).astype(BF16)
    tail = u[tm - CONV_HALO:, :]
    ext_ref[0:CONV_HALO, :] = tail
    utail_ref[0] = tail


def _proj_sample_kernel(x_ref, gmix_ref, wqkv_ref, wf_ref, wglu_ref, wgate_ref, bf_ref, qg_ref, kg_ref,
                        gsum_ref, st_ref, wdw_ref, bdw_ref, lng_ref, lnb_ref,
                        qb_ref, k32_ref, v32_ref, lf_ref, u_ref, y_ref, sg_ref):
    q, k, v, logf, u, sg = _project(x_ref[...], gmix_ref[...], wqkv_ref[...], wf_ref[...], wglu_ref[...],
                                    wgate_ref[...], bf_ref[...], qg_ref[...], kg_ref[...], gsum_ref[...])
    qb_ref[...] = q.astype(BF16)
    k32_ref[...] = k
    v32_ref[...] = v
    sg_ref[...] = sg
    lf_ref[...] = logf[:, :N_HEADS]
    u_ref[...] = u
    acc = u * wdw_ref[CONV_K - 1:CONV_K, :]
    for j in range(CONV_K - 1):
        acc = acc + st_ref[j] * wdw_ref[j:j + 1, :]
    y_ref[...] = _ln_silu(acc + bdw_ref[...], lng_ref[...], lnb_ref[...]).astype(BF16)


def _prep_proj_weights(norm_mix_g, w_in, b_forget, q_norm_g, k_norm_g):
    d = w_in.shape[0]
    o_f = 3 * ATTN_W
    o_glu = o_f + N_HEADS
    o_gate = o_glu + 2 * CONV_CH
    wqkv = w_in[:, :o_f].astype(BF16)
    wf = jnp.pad(w_in[:, o_f:o_glu], ((0, 0), (0, LANES - N_HEADS))).astype(BF16)
    wglu = w_in[:, o_glu:o_gate].astype(BF16)
    wgate = w_in[:, o_gate:].astype(BF16)
    bf = jnp.pad(b_forget, (0, LANES - N_HEADS)).reshape(1, LANES)
    qg = jnp.tile(q_norm_g, N_HEADS).reshape(1, ATTN_W)
    kg = jnp.tile(k_norm_g, N_HEADS).reshape(1, ATTN_W)
    hid = jnp.arange(ATTN_W) // HEAD_DIM
    gsum = (hid[:, None] == hid[None, :]).astype(BF16)
    return (norm_mix_g.reshape(1, d), wqkv, wf, wglu, wgate, bf, qg, kg, gsum)


def _proj_prompt(x2d, pw, conv_w, batch, seq, tm):
    n, d = x2d.shape
    tps = seq // tm
    tri = (jnp.arange(tm)[:, None] >= jnp.arange(tm)[None, :]).astype(BF16)
    wdw, bdw, lng, lnb = conv_w
    row = lambda w: pl.BlockSpec((tm, w), lambda i: (i, 0))
    in_specs = ([row(d)] + [_const_spec(a.shape) for a in pw] + [_const_spec(tri.shape)]
                + [_const_spec(a.shape) for a in conv_w])
    out_shape = (
        jax.ShapeDtypeStruct((n, ATTN_W), BF16), jax.ShapeDtypeStruct((n, ATTN_W), BF16),
        jax.ShapeDtypeStruct((n, ATTN_W), BF16), jax.ShapeDtypeStruct((n, ATTN_W), F32),
        jax.ShapeDtypeStruct((n, ATTN_W), F32), jax.ShapeDtypeStruct((n, N_HEADS), F32),
        jax.ShapeDtypeStruct((n, N_HEADS), F32), jax.ShapeDtypeStruct((n, CONV_CH), BF16),
        jax.ShapeDtypeStruct((n, 2 * d), F32), jax.ShapeDtypeStruct((batch, CONV_HALO, CONV_CH), F32),
    )
    out_specs = (row(ATTN_W), row(ATTN_W), row(ATTN_W), row(ATTN_W), row(ATTN_W), row(N_HEADS), row(N_HEADS),
                 row(CONV_CH), row(2 * d), pl.BlockSpec((1, CONV_HALO, CONV_CH), lambda i: (i // tps, 0, 0)))
    return pl.pallas_call(
        functools.partial(_proj_prompt_kernel, tiles_per_seq=tps, tm=tm),
        grid=(n // tm,), in_specs=in_specs, out_specs=out_specs, out_shape=out_shape,
        scratch_shapes=[pltpu.VMEM((CONV_HALO + 2 * tm, CONV_CH), F32), pltpu.VMEM((8, LANES), F32)],
        compiler_params=pltpu.CompilerParams(dimension_semantics=("arbitrary",), vmem_limit_bytes=VMEM_LIMIT),
        name="proj_prompt",
    )(x2d, *pw, tri, wdw, bdw, lng, lnb)


def _proj_sample(x2d, pw, state_t, conv_w):
    n, d = x2d.shape
    args = (x2d,) + tuple(pw) + (state_t,) + tuple(conv_w)
    full = lambda a: pl.BlockSpec(a.shape, lambda i, nd=a.ndim: (0,) * nd)
    out_shape = (
        jax.ShapeDtypeStruct((n, ATTN_W), BF16), jax.ShapeDtypeStruct((n, ATTN_W), F32),
        jax.ShapeDtypeStruct((n, ATTN_W), F32), jax.ShapeDtypeStruct((n, N_HEADS), F32),
        jax.ShapeDtypeStruct((n, CONV_CH), F32), jax.ShapeDtypeStruct((n, CONV_CH), BF16),
        jax.ShapeDtypeStruct((n, 2 * d), F32),
    )
    return pl.pallas_call(
        _proj_sample_kernel, grid=(1,), in_specs=[full(a) for a in args],
        out_specs=tuple(full(s) for s in out_shape), out_shape=out_shape,
        compiler_params=pltpu.CompilerParams(dimension_semantics=("arbitrary",), vmem_limit_bytes=VMEM_LIMIT),
        name="proj_sample",
    )(*args)


def _attn_prompt_kernel(q_ref, k_ref, v_ref, cq_ref, ck_ref, o_ref, *, tq):
    i = pl.program_id(2)
    lane = lax.broadcasted_iota(jnp.int32, (1, LANES), 1)
    row = lax.broadcasted_iota(jnp.int32, (tq, tq), 0)
    col = lax.broadcasted_iota(jnp.int32, (tq, tq), 1)
    q2 = q_ref[0]
    out = jnp.zeros((tq, LANES), F32)
    for h in range(2):
        in_head = (lane // HEAD_DIM) == h
        q = jnp.where(in_head, q2, jnp.zeros_like(q2))
        cq = cq_ref[0, 0][:, h:h + 1] * LOG2E

        def scores(j):
            start = pl.multiple_of(j * tq, tq)
            ks = k_ref[0, pl.ds(start, tq), :]
            s = lax.dot_general(q, ks, (((1,), (1,)), ((), ())), preferred_element_type=F32)
            ck = ck_ref[0, 0, h:h + 1, pl.ds(start, tq)] * LOG2E
            return s + (cq - ck), start

        def update(carry, s, start):
            m, l, acc = carry
            m_new = jnp.maximum(m, jnp.max(s, axis=-1, keepdims=True))
            alpha = jnp.exp2(m - m_new)
            p = jnp.exp2(s - m_new)
            l = alpha * l + jnp.sum(p, axis=-1, keepdims=True)
            vs = v_ref[0, pl.ds(start, tq), :]
            acc = alpha * acc + _dot(p.astype(BF16), vs)
            return m_new, l, acc

        def body(j, carry):
            s, start = scores(j)
            return update(carry, s, start)

        init = (jnp.full((tq, 1), -jnp.inf, F32), jnp.zeros((tq, 1), F32), jnp.zeros((tq, LANES), F32))
        carry = lax.fori_loop(0, i, body, init)
        s, start = scores(i)
        s = jnp.where(col <= row, s, -jnp.inf)
        m, l, acc = update(carry, s, start)
        out = jnp.where(in_head, acc / l, out)
    o_ref[0] = out.astype(BF16)


def _attn_prompt(qb, kb, vb, c, batch, seq, tq):
    hp = N_HEADS // 2
    q3 = qb.reshape(batch, seq, ATTN_W)
    k3 = kb.reshape(batch, seq, ATTN_W)
    v3 = vb.reshape(batch, seq, ATTN_W)
    c4 = c.reshape(batch, seq, hp, 2)
    cq = c4.transpose(0, 2, 1, 3)
    ck = c4.transpose(0, 2, 3, 1)
    return pl.pallas_call(
        functools.partial(_attn_prompt_kernel, tq=tq),
        grid=(batch, hp, seq // tq),
        in_specs=[
            pl.BlockSpec((1, tq, LANES), lambda b, h, i: (b, i, h)),
            pl.BlockSpec((1, seq, LANES), lambda b, h, i: (b, 0, h)),
            pl.BlockSpec((1, seq, LANES), lambda b, h, i: (b, 0, h)),
            pl.BlockSpec((1, 1, tq, 2), lambda b, h, i: (b, h, i, 0)),
            pl.BlockSpec((1, 1, 2, seq), lambda b, h, i: (b, h, 0, 0)),
        ],
        out_specs=pl.BlockSpec((1, tq, LANES), lambda b, h, i: (b, i, h)),
        out_shape=jax.ShapeDtypeStruct((batch, seq, ATTN_W), BF16),
        compiler_params=pltpu.CompilerParams(dimension_semantics=("arbitrary", "arbitrary", "arbitrary"),
                                             vmem_limit_bytes=VMEM_LIMIT),
        name="attn_prompt",
    )(q3, k3, v3, cq, ck)


def _attn_sample_kernel(pt_ref, q_ref, kn_ref, vn_ref, lfn_ref, *refs, pp):
    k_refs = refs[:pp]
    v_refs = refs[pp:2 * pp]
    f_refs = refs[2 * pp:3 * pp]
    o_ref = refs[3 * pp]
    m_ref, l_ref, r_ref, acc_ref = refs[3 * pp + 1:]
    j = pl.program_id(1)
    hrow = lax.broadcasted_iota(jnp.int32, (N_HEADS, ATTN_W), 0)
    hlane = lax.broadcasted_iota(jnp.int32, (N_HEADS, ATTN_W), 1) // HEAD_DIM
    diag = hrow == hlane
    qf = jnp.where(diag, q_ref[0].astype(F32), 0.0)
    qm = qf.astype(BF16)

    @pl.when(j == 0)
    def _():
        m_ref[...] = jnp.sum(qf * kn_ref[0], axis=-1, keepdims=True)
        l_ref[...] = jnp.ones_like(l_ref)
        r_ref[...] = lfn_ref[0]
        acc_ref[...] = jnp.broadcast_to(vn_ref[0], (N_HEADS, ATTN_W))

    srow = lax.broadcasted_iota(jnp.int32, (PAGE_SIZE, 2 * PAGE_SIZE), 0)
    scol = lax.broadcasted_iota(jnp.int32, (PAGE_SIZE, 2 * PAGE_SIZE), 1)
    suffix = jnp.where((srow > scol) | (scol >= PAGE_SIZE), 1.0, 0.0).astype(F32)

    r = r_ref[...]
    parts = []
    for t in range(pp):
        sx = jnp.dot(f_refs[t][0], suffix, precision=lax.Precision.HIGHEST, preferred_element_type=F32)
        bias = r + sx[:, :PAGE_SIZE]
        r = r + sx[:, PAGE_SIZE:PAGE_SIZE + 1]
        parts.append(_dot(qm, k_refs[t][0].astype(BF16)) + bias * LOG2E)
    r_ref[...] = r
    s = jnp.concatenate(parts, axis=1)
    m = m_ref[...]
    m_new = jnp.maximum(m, jnp.max(s, axis=-1, keepdims=True))
    alpha = jnp.exp2(m - m_new)
    p = jnp.exp2(s - m_new)
    l_ref[...] = alpha * l_ref[...] + jnp.sum(p, axis=-1, keepdims=True)
    m_ref[...] = m_new
    pb = p.astype(BF16)
    pv = jnp.zeros((N_HEADS, ATTN_W), F32)
    for t in range(pp):
        pv = pv + lax.dot_general(pb[:, t * PAGE_SIZE:(t + 1) * PAGE_SIZE], v_refs[t][0].astype(BF16),
                                  (((1,), (1,)), ((), ())), preferred_element_type=F32)
    acc_ref[...] = alpha * acc_ref[...] + pv

    @pl.when(j == pl.num_programs(1) - 1)
    def _():
        o = jnp.where(diag, acc_ref[...] / l_ref[...], 0.0)
        o_ref[0] = jnp.sum(o, axis=0, keepdims=True).astype(BF16)


def _attn_sample(page_table, qb, k_new, v_new, lf_new, cache_k, cache_v, cache_logf, pp):
    bd, n_pages = page_table.shape
    n_pool = cache_k.shape[0]
    kc = cache_k.transpose(0, 2, 3, 1).reshape(n_pool, ATTN_W, PAGE_SIZE)
    vc = cache_v.transpose(0, 2, 3, 1).reshape(n_pool, ATTN_W, PAGE_SIZE)
    fc = cache_logf.transpose(0, 2, 1)
    steps = n_pages // pp

    def page_spec(t, rows):
        return pl.BlockSpec((1, rows, PAGE_SIZE),
                            lambda b, j, pt: (pt[b, n_pages - 1 - (j * pp + t)], 0, 0))

    per_seq = lambda shape: pl.BlockSpec((1,) + shape, lambda b, j, pt: (b, 0, 0))
    in_specs = ([per_seq((1, ATTN_W)), per_seq((1, ATTN_W)), per_seq((1, ATTN_W)), per_seq((N_HEADS, 1))]
                + [page_spec(t, ATTN_W) for t in range(pp)] * 2
                + [page_spec(t, N_HEADS) for t in range(pp)])
    grid_spec = pltpu.PrefetchScalarGridSpec(
        num_scalar_prefetch=1, grid=(bd, steps), in_specs=in_specs,
        out_specs=per_seq((1, ATTN_W)),
        scratch_shapes=[pltpu.VMEM((N_HEADS, 1), F32), pltpu.VMEM((N_HEADS, 1), F32),
                        pltpu.VMEM((N_HEADS, 1), F32), pltpu.VMEM((N_HEADS, ATTN_W), F32)],
    )
    out = pl.pallas_call(
        functools.partial(_attn_sample_kernel, pp=pp), grid_spec=grid_spec,
        out_shape=jax.ShapeDtypeStruct((bd, 1, ATTN_W), BF16),
        compiler_params=pltpu.CompilerParams(dimension_semantics=("arbitrary", "arbitrary"),
                                             vmem_limit_bytes=VMEM_LIMIT),
        name="attn_sample",
    )(page_table, qb.reshape(bd, 1, ATTN_W), k_new.reshape(bd, 1, ATTN_W), v_new.reshape(bd, 1, ATTN_W),
      lf_new.reshape(bd, N_HEADS, 1), *([kc] * pp), *([vc] * pp), *([fc] * pp))
    return out.reshape(bd, ATTN_W)


def _mix_kernel(x_ref, a_ref, y_ref, sg_ref, wao_ref, wco_ref, wout_ref, gffn_ref, wr_ref, br_ref,
                x1_ref, h2_ref, gates_ref):
    d = x_ref.shape[-1]
    ao = _dot(a_ref[...], wao_ref[...])
    co = _dot(y_ref[...], wco_ref[...])
    merged = sg_ref[:, :d] * ao + sg_ref[:, d:] * co
    x1 = x_ref[...] + _dot(merged.astype(BF16), wout_ref[...])
    x1_ref[...] = x1
    h2 = _rms(x1, gffn_ref[...]).astype(BF16)
    h2_ref[...] = h2

    logits = _dot(h2, wr_ref[...]) + br_ref[...]
    lane = lax.broadcasted_iota(jnp.int32, logits.shape, 1)
    neg = -jnp.inf
    big = jnp.int32(4 * LANES)
    gl = jnp.where(lane < N_GROUPS, logits, neg)
    gmax = jnp.max(gl, axis=-1, keepdims=True)
    gval = 1.0 / jnp.sum(jnp.exp(gl - gmax), axis=-1, keepdims=True)
    gidx = jnp.min(jnp.where(gl == gmax, lane, big), axis=-1, keepdims=True)
    ex = lane - EXPERT_LANE0
    in_group = (ex >= gidx * EXP_PER_GROUP) & (ex < (gidx + 1) * EXP_PER_GROUP)
    el = jnp.where(in_group, logits, neg)
    v1 = jnp.max(el, axis=-1, keepdims=True)
    i1 = jnp.min(jnp.where(el == v1, lane, big), axis=-1, keepdims=True)
    el2 = jnp.where(lane == i1, neg, el)
    v2 = jnp.max(el2, axis=-1, keepdims=True)
    i2 = jnp.min(jnp.where(el2 == v2, lane, big), axis=-1, keepdims=True)
    e2 = jnp.exp(v2 - v1)
    w1 = gval / (1.0 + e2)
    w2 = gval * e2 / (1.0 + e2)
    gates_ref[...] = jnp.where(lane == i1, w1, 0.0) + jnp.where(lane == i2, w2, 0.0)


def _mix(x2d, attn, y, sg, mw, tm):
    n, d = x2d.shape
    row = lambda w: pl.BlockSpec((tm, w), lambda i: (i, 0))
    return pl.pallas_call(
        _mix_kernel, grid=(n // tm,),
        in_specs=[row(d), row(ATTN_W), row(CONV_CH), row(2 * d)] + [_const_spec(a.shape) for a in mw],
        out_specs=(row(d), row(d), row(LANES)),
        out_shape=(jax.ShapeDtypeStruct((n, d), F32), jax.ShapeDtypeStruct((n, d), BF16),
                   jax.ShapeDtypeStruct((n, LANES), F32)),
        compiler_params=pltpu.CompilerParams(dimension_semantics=("arbitrary",), vmem_limit_bytes=VMEM_LIMIT),
        name="mix",
    )(x2d, attn, y, sg, *mw)


def _moe_kernel(x1_ref, h2_ref, gates_ref, p_ref, wg_ref, wu_ref, wd_ref, gple_ref, wpg_ref, wpp_ref,
                o_ref, acc_ref, *, eb):
    jb = pl.program_id(1)

    @pl.when(jb == 0)
    def _():
        acc_ref[...] = jnp.zeros_like(acc_ref)

    h2 = h2_ref[...]
    gates = gates_ref[...]
    lane = lax.broadcasted_iota(jnp.int32, gates.shape, 1)
    for e in range(eb):
        gcol = jnp.sum(jnp.where(lane == EXPERT_LANE0 + jb * eb + e, gates, 0.0), axis=-1, keepdims=True)
        hg = _dot(h2, wg_ref[e])
        hu = _dot(h2, wu_ref[e])
        he = (hg * _sigmoid(hg)) * hu * gcol
        acc_ref[...] += _dot(he.astype(BF16), wd_ref[e])

    @pl.when(jb == pl.num_programs(1) - 1)
    def _():
        x2 = x1_ref[...] + acc_ref[...]
        g = _sigmoid(_dot(_rms(x2, gple_ref[...]).astype(BF16), wpg_ref[...]))
        o_ref[...] = x2 + g * _dot(p_ref[...].astype(BF16), wpp_ref[...])


def _moe_ple(x1, h2, gates, p2d, ew, pw, tm, eb):
    n, d = x1.shape
    wg, wu, wd = ew
    row = lambda w: pl.BlockSpec((tm, w), lambda i, j: (i, 0))
    return pl.pallas_call(
        functools.partial(_moe_kernel, eb=eb), grid=(n // tm, N_EXPERTS // eb),
        in_specs=[row(d), row(d), row(LANES), row(p2d.shape[1]),
                  pl.BlockSpec((eb, d, EXPERT_FF), lambda i, j: (j, 0, 0)),
                  pl.BlockSpec((eb, d, EXPERT_FF), lambda i, j: (j, 0, 0)),
                  pl.BlockSpec((eb, EXPERT_FF, d), lambda i, j: (j, 0, 0))]
                 + [_const_spec(a.shape) for a in pw],
        out_specs=row(d), out_shape=jax.ShapeDtypeStruct((n, d), F32),
        scratch_shapes=[pltpu.VMEM((tm, d), F32)],
        compiler_params=pltpu.CompilerParams(dimension_semantics=("arbitrary", "arbitrary"),
                                             vmem_limit_bytes=VMEM_LIMIT),
        name="moe_ple",
    )(x1, h2, gates, p2d, wg, wu, wd, *pw)


def _tile(n, pref):
    return pref if n % pref == 0 else n


def kernel(x_prompt, x_sample, cache_k, cache_v, cache_logf, state_conv, page_table, p_prompt, p_sample, norm_mix_g, w_in, b_forget, q_norm_g, k_norm_g, w_attn_o, conv_dw_w, conv_dw_b, conv_ln_g, conv_ln_b, w_conv_o, w_out, norm_ffn_g, w_router_group, b_router_group, w_router_expert, b_router_expert, w_exp_gate, w_exp_up, w_exp_down, norm_ple_g, w_ple_gate, w_ple_proj):
    depth = w_in.shape[0]
    assert depth == 1
    li = 0
    batch, seq, d = x_prompt.shape
    bd, dec_seq, _ = x_sample.shape
    assert dec_seq == 1
    n_pages = page_table.shape[1]

    pw = _prep_proj_weights(norm_mix_g[li], w_in[li], b_forget[li], q_norm_g[li], k_norm_g[li])
    conv_w = (conv_dw_w[li], conv_dw_b[li].reshape(1, CONV_CH), conv_ln_g[li].reshape(1, CONV_CH),
              conv_ln_b[li].reshape(1, CONV_CH))
    wr = jnp.zeros((d, LANES), F32)
    wr = wr.at[:, :N_GROUPS].set(w_router_group[li]).at[:, EXPERT_LANE0:EXPERT_LANE0 + N_EXPERTS].set(
        w_router_expert[li]).astype(BF16)
    br = jnp.zeros((1, LANES), F32)
    br = br.at[0, :N_GROUPS].set(b_router_group[li]).at[0, EXPERT_LANE0:EXPERT_LANE0 + N_EXPERTS].set(
        b_router_expert[li])
    mw = (w_attn_o[li].astype(BF16), w_conv_o[li].astype(BF16), w_out[li].astype(BF16),
          norm_ffn_g[li].reshape(1, d), wr, br)
    ew = (w_exp_gate[li].astype(BF16), w_exp_up[li].astype(BF16), w_exp_down[li].astype(BF16))
    plew = (norm_ple_g[li].reshape(1, d), w_ple_gate[li].astype(BF16), w_ple_proj[li].astype(BF16))

    xp = x_prompt.reshape(batch * seq, d)
    tm = _tile(seq, 512)
    qb, kb, vb, k32, v32, lf, c, y, sg, utail = _proj_prompt(xp, pw, conv_w, batch, seq, tm)
    attn = _attn_prompt(qb, kb, vb, c, batch, seq, _tile(seq, 512)).reshape(batch * seq, ATTN_W)
    x1, h2, gates = _mix(xp, attn, y, sg, mw, tm)
    yp = _moe_ple(x1, h2, gates, p_prompt[li].reshape(batch * seq, -1), ew, plew, _tile(batch * seq, 1024), 4)
    y_prompt = yp.reshape(batch, seq, d)
    new_k_prompt = k32.reshape(1, batch, seq, N_HEADS, HEAD_DIM)
    new_v_prompt = v32.reshape(1, batch, seq, N_HEADS, HEAD_DIM)
    new_logf_prompt = lf.reshape(1, batch, seq, N_HEADS)
    new_conv_prompt = utail[:, CONV_HALO - (CONV_K - 1):, :].reshape(1, batch, CONV_K - 1, CONV_CH)

    xs = x_sample.reshape(bd, d)
    state_t = state_conv[li].transpose(1, 0, 2)
    qs, ks, vs, lfs, us, ys, sgs = _proj_sample(xs, pw, state_t, conv_w)
    attn_s = _attn_sample(page_table, qs, ks, vs, lfs, cache_k[li], cache_v[li], cache_logf[li],
                          16 if n_pages % 16 == 0 else 8)
    x1s, h2s, gates_s = _mix(xs, attn_s, ys, sgs, mw, bd)
    ysm = _moe_ple(x1s, h2s, gates_s, p_sample[li].reshape(bd, -1), ew, plew, bd, 4)
    y_sample = ysm.reshape(bd, 1, d)
    new_k_sample = ks.reshape(1, bd, 1, N_HEADS, HEAD_DIM)
    new_v_sample = vs.reshape(1, bd, 1, N_HEADS, HEAD_DIM)
    new_logf_sample = lfs.reshape(1, bd, 1, N_HEADS)
    new_conv_sample = jnp.concatenate([state_conv[li][:, 1:, :], us[:, None, :]], axis=1)[None]

    return (y_prompt, y_sample, new_k_prompt, new_v_prompt, new_logf_prompt, new_conv_prompt,
            new_k_sample, new_v_sample, new_logf_sample, new_conv_sample)
```

```python
import functools

import jax
import jax.numpy as jnp
from jax import lax
from jax.experimental import pallas as pl
from jax.experimental.pallas import tpu as pltpu

F32 = jnp.float32
BF16 = jnp.bfloat16

N_HEADS = 8
HEAD_DIM = 64
ATTN_W = N_HEADS * HEAD_DIM
CONV_CH = 512
CONV_K = 31
N_GROUPS = 4
EXP_PER_GROUP = 8
N_EXPERTS = N_GROUPS * EXP_PER_GROUP
EXPERT_FF = 256
PAGE_SIZE = 128
EPS = 1e-6

LANES = 128
CONV_HALO = 32
VMEM_LIMIT = 56 * 1024 * 1024
EXPERT_LANE0 = 32
LOG2E = 1.4426950408889634
AUG_K0 = 64
AUG_Q0 = 88


def _const_spec(shape):
    nd = len(shape)
    return pl.BlockSpec(shape, lambda *_: (0,) * nd, pipeline_mode=pl.Buffered(1))


def _sigmoid(x):
    return 1.0 / (1.0 + jnp.exp(-x))


def _log_sigmoid(x):
    return -(jnp.maximum(-x, 0.0) + jnp.log1p(jnp.exp(-jnp.abs(x))))


def _rms(x, g):
    return x * lax.rsqrt(jnp.mean(x * x, axis=-1, keepdims=True) + EPS) * g


def _dot(a, b):
    return jnp.dot(a, b, preferred_element_type=F32)


def _project(x, gmix, wqkv, wf, wglu, wgate, bf, qg, kg, gsum):
    hb = _rms(x, gmix).astype(BF16)
    zqkv = _dot(hb, wqkv)
    zq = zqkv[:, :ATTN_W]
    zk = zqkv[:, ATTN_W:2 * ATTN_W]
    zv = zqkv[:, 2 * ATTN_W:]

    def head_norm(z, g):
        ss = _dot((z * z).astype(BF16), gsum)
        return z * lax.rsqrt(ss * (1.0 / HEAD_DIM) + EPS) * g

    q = head_norm(zq, qg) * (HEAD_DIM ** -0.5 * LOG2E)
    k = head_norm(zk, kg)
    zf = _dot(hb, wf) + bf
    lane = lax.broadcasted_iota(jnp.int32, zf.shape, 1)
    logf = jnp.where(lane < N_HEADS, _log_sigmoid(zf), 0.0)
    zglu = _dot(hb, wglu)
    u = zglu[:, :CONV_CH] * _sigmoid(zglu[:, CONV_CH:])
    sg = _sigmoid(_dot(hb, wgate))
    return q, k, zv, logf, u, sg


def _ln_silu(y, g, b):
    mu = jnp.mean(y, axis=-1, keepdims=True)
    d = y - mu
    var = jnp.mean(d * d, axis=-1, keepdims=True)
    z = d * lax.rsqrt(var + EPS) * g + b
    return z * _sigmoid(z)


def _split3(x):
    a = x.astype(BF16)
    r = x - a.astype(F32)
    b = r.astype(BF16)
    c = (r - b.astype(F32)).astype(BF16)
    return a, b, c


def _proj_prompt_kernel(x_ref, gmix_ref, wqkv_ref, wf_ref, wglu_ref, wgate_ref, bf_ref, qg_ref, kg_ref,
                        gsum_ref, tri_ref, wdw_ref, bdw_ref, lng_ref, lnb_ref,
                        qa_ref, ka_ref, va_ref, kt_ref, vt_ref, lf_ref, y_ref, sg_ref, utail_ref,
                        ext_ref, carry_ref, *, tiles_per_seq, tm):
    i = pl.program_id(0)

    @pl.when(i % tiles_per_seq == 0)
    def _():
        ext_ref[0:CONV_HALO, :] = jnp.zeros((CONV_HALO, CONV_CH), F32)
        carry_ref[...] = jnp.zeros_like(carry_ref)

    q, k, v, logf, u, sg = _project(x_ref[...], gmix_ref[...], wqkv_ref[...], wf_ref[...], wglu_ref[...],
                                    wgate_ref[...], bf_ref[...], qg_ref[...], kg_ref[...], gsum_ref[...])
    sg_ref[...] = sg
    lf_ref[...] = logf[:, :N_HEADS]

    tri = tri_ref[...]
    a, b, c3 = _split3(logf)
    c = _dot(tri, a) + _dot(tri, b) + _dot(tri, c3) + carry_ref[0:1, :]
    carry_ref[0:1, :] = c[tm - 1:tm, :]

    ca, cb, cc = (t.astype(F32) for t in _split3(c * LOG2E))
    tail_k = -(pltpu.roll(ca, AUG_K0, 1) + pltpu.roll(cb, AUG_K0 + 8, 1) + pltpu.roll(cc, AUG_K0 + 16, 1))
    tail_q = pltpu.roll(ca, AUG_Q0, 1) + pltpu.roll(cb, AUG_Q0 + 8, 1) + pltpu.roll(cc, AUG_Q0 + 16, 1)
    lane = lax.broadcasted_iota(jnp.int32, (1, LANES), 1)
    low = lane < HEAD_DIM
    for h in range(N_HEADS):
        blk = slice((h // 2) * LANES, (h // 2 + 1) * LANES)
        qh, kh = q[:, blk], k[:, blk]
        if h % 2:
            qh, kh = pltpu.roll(qh, HEAD_DIM, 1), pltpu.roll(kh, HEAD_DIM, 1)
        pick_k = ((lane == AUG_K0 + h) | (lane == AUG_K0 + 8 + h) | (lane == AUG_K0 + 16 + h)).astype(F32)
        pick_q = ((lane == AUG_Q0 + h) | (lane == AUG_Q0 + 8 + h) | (lane == AUG_Q0 + 16 + h)).astype(F32)
        qa_ref[0, h] = jnp.where(low, qh, tail_q + pick_k).astype(BF16)
        ka_ref[0, h] = jnp.where(low, kh, tail_k + pick_q).astype(BF16)

    kt_ref[0] = k.T
    vt = v.T
    vt_ref[0] = vt
    row = lax.broadcasted_iota(jnp.int32, (LANES - HEAD_DIM, tm), 0)
    ones_row = jnp.where(row == 0, 1.0, 0.0)
    for h in range(N_HEADS):
        va_ref[0, h] = jnp.concatenate([vt[h * HEAD_DIM:(h + 1) * HEAD_DIM], ones_row], axis=0).astype(BF16)

    ext_ref[CONV_HALO:CONV_HALO + tm, :] = u
    off = CONV_HALO - (CONV_K - 1)
    rc = min(tm, 128)
    for r0 in range(0, tm, rc):
        for l0 in range(0, CONV_CH, LANES):
            acc = jnp.zeros((rc, LANES), F32)
            for j in range(CONV_K):
                acc = acc + ext_ref[off + j + r0:off + j + r0 + rc, l0:l0 + LANES] * wdw_ref[j:j + 1, l0:l0 + LANES]
            ext_ref[CONV_HALO + tm + r0:CONV_HALO + tm + r0 + rc, l0:l0 + LANES] = acc
    yc = ext_ref[CONV_HALO + tm:CONV_HALO + 2 * tm, :] + bdw_ref[...]
    y_ref[...] = _ln_silu(yc, lng_ref[...], lnb_ref[...]).astype(BF16)
    tail = u[tm - CONV_HALO:, :]
    ext_ref[0:CONV_HALO, :] = tail
    utail_ref[0] = tail


def _proj_sample_kernel(x_ref, gmix_ref, wqkv_ref, wf_ref, wglu_ref, wgate_ref, bf_ref, qg_ref, kg_ref,
                        gsum_ref, st_ref, wdw_ref, bdw_ref, lng_ref, lnb_ref,
                        qb_ref, k32_ref, v32_ref, lf_ref, u_ref, y_ref, sg_ref):
    q, k, v, logf, u, sg = _project(x_ref[...], gmix_ref[...], wqkv_ref[...], wf_ref[...], wglu_ref[...],
                                    wgate_ref[...], bf_ref[...], qg_ref[...], kg_ref[...], gsum_ref[...])
    qb_ref[...] = q.astype(BF16)
    k32_ref[...] = k
    v32_ref[...] = v
    sg_ref[...] = sg
    lf_ref[...] = logf[:, :N_HEADS]
    u_ref[...] = u
    acc = u * wdw_ref[CONV_K - 1:CONV_K, :]
    for j in range(CONV_K - 1):
        acc = acc + st_ref[j] * wdw_ref[j:j + 1, :]
    y_ref[...] = _ln_silu(acc + bdw_ref[...], lng_ref[...], lnb_ref[...]).astype(BF16)


def _prep_proj_weights(norm_mix_g, w_in, b_forget, q_norm_g, k_norm_g):
    d = w_in.shape[0]
    o_f = 3 * ATTN_W
    o_glu = o_f + N_HEADS
    o_gate = o_glu + 2 * CONV_CH
    wqkv = w_in[:, :o_f].astype(BF16)
    wf = jnp.pad(w_in[:, o_f:o_glu], ((0, 0), (0, LANES - N_HEADS))).astype(BF16)
    wglu = w_in[:, o_glu:o_gate].astype(BF16)
    wgate = w_in[:, o_gate:].astype(BF16)
    bf = jnp.pad(b_forget, (0, LANES - N_HEADS)).reshape(1, LANES)
    qg = jnp.tile(q_norm_g, N_HEADS).reshape(1, ATTN_W)
    kg = jnp.tile(k_norm_g, N_HEADS).reshape(1, ATTN_W)
    hid = jnp.arange(ATTN_W) // HEAD_DIM
    gsum = (hid[:, None] == hid[None, :]).astype(BF16)
    return (norm_mix_g.reshape(1, d), wqkv, wf, wglu, wgate, bf, qg, kg, gsum)


def _proj_prompt(x2d, pw, conv_w, batch, seq, tm):
    n, d = x2d.shape
    tps = seq // tm
    tri = (jnp.arange(tm)[:, None] >= jnp.arange(tm)[None, :]).astype(BF16)
    wdw, bdw, lng, lnb = conv_w
    row = lambda w: pl.BlockSpec((tm, w), lambda i: (i, 0))
    in_specs = ([row(d)] + [_const_spec(a.shape) for a in pw] + [_const_spec(tri.shape)]
                + [_const_spec(a.shape) for a in conv_w])
    head_rows = jax.ShapeDtypeStruct((batch, N_HEADS, seq, LANES), BF16)
    head_rows_spec = pl.BlockSpec((1, N_HEADS, tm, LANES), lambda i: (i // tps, 0, i % tps, 0))
    seq_minor = jax.ShapeDtypeStruct((batch, ATTN_W, seq), F32)
    seq_minor_spec = pl.BlockSpec((1, ATTN_W, tm), lambda i: (i // tps, 0, i % tps))
    out_shape = (
        head_rows, head_rows, jax.ShapeDtypeStruct((batch, N_HEADS, LANES, seq), BF16), seq_minor, seq_minor,
        jax.ShapeDtypeStruct((n, N_HEADS), F32), jax.ShapeDtypeStruct((n, CONV_CH), BF16),
        jax.ShapeDtypeStruct((n, 2 * d), F32), jax.ShapeDtypeStruct((batch, CONV_HALO, CONV_CH), F32),
    )
    out_specs = (head_rows_spec, head_rows_spec,
                 pl.BlockSpec((1, N_HEADS, LANES, tm), lambda i: (i // tps, 0, 0, i % tps)),
                 seq_minor_spec, seq_minor_spec, row(N_HEADS),
                 row(CONV_CH), row(2 * d), pl.BlockSpec((1, CONV_HALO, CONV_CH), lambda i: (i // tps, 0, 0)))
    return pl.pallas_call(
        functools.partial(_proj_prompt_kernel, tiles_per_seq=tps, tm=tm),
        grid=(n // tm,), in_specs=in_specs, out_specs=out_specs, out_shape=out_shape,
        scratch_shapes=[pltpu.VMEM((CONV_HALO + 2 * tm, CONV_CH), F32), pltpu.VMEM((8, LANES), F32)],
        compiler_params=pltpu.CompilerParams(dimension_semantics=("arbitrary",), vmem_limit_bytes=VMEM_LIMIT),
        name="proj_prompt",
    )(x2d, *pw, tri, wdw, bdw, lng, lnb)


def _proj_sample(x2d, pw, state_t, conv_w):
    n, d = x2d.shape
    args = (x2d,) + tuple(pw) + (state_t,) + tuple(conv_w)
    full = lambda a: pl.BlockSpec(a.shape, lambda i, nd=a.ndim: (0,) * nd)
    out_shape = (
        jax.ShapeDtypeStruct((n, ATTN_W), BF16), jax.ShapeDtypeStruct((n, ATTN_W), F32),
        jax.ShapeDtypeStruct((n, ATTN_W), F32), jax.ShapeDtypeStruct((n, N_HEADS), F32),
        jax.ShapeDtypeStruct((n, CONV_CH), F32), jax.ShapeDtypeStruct((n, CONV_CH), BF16),
        jax.ShapeDtypeStruct((n, 2 * d), F32),
    )
    return pl.pallas_call(
        _proj_sample_kernel, grid=(1,), in_specs=[full(a) for a in args],
        out_specs=tuple(full(s) for s in out_shape), out_shape=out_shape,
        compiler_params=pltpu.CompilerParams(dimension_semantics=("arbitrary",), vmem_limit_bytes=VMEM_LIMIT),
        name="proj_sample",
    )(*args)


def _attn_prompt_kernel(q_ref, k_ref, v_ref, o_ref, *, tq, tk):
    i = pl.program_id(2)
    heads = range(2)
    qs = [q_ref[0, h] for h in heads]
    per_q = tq // tk

    def block(j, carry, diag_chunk=None):
        start = pl.multiple_of(j * tk, tk)
        new = []
        for h in heads:
            m, acc = carry[h]
            ks = k_ref[0, h, pl.ds(start, tk), :]
            st = lax.dot_general(ks, qs[h], (((1,), (1,)), ((), ())), preferred_element_type=F32)
            if diag_chunk is not None:
                key = lax.broadcasted_iota(jnp.int32, (tk, tq), 0) + diag_chunk * tk
                qry = lax.broadcasted_iota(jnp.int32, (tk, tq), 1)
                st = jnp.where(key <= qry, st, -jnp.inf)
            m_new = jnp.maximum(m, jnp.max(st, axis=0, keepdims=True))
            alpha = jnp.exp2(m - m_new)
            pt = jnp.exp2(st - m_new).astype(BF16)
            vs = v_ref[0, h, :, pl.ds(start, tk)]
            new.append((m_new, alpha * acc + _dot(vs, pt)))
        return tuple(new)

    init = tuple((jnp.full((1, tq), -jnp.inf, F32), jnp.zeros((LANES, tq), F32)) for _ in heads)
    carry = lax.fori_loop(0, i * per_q, block, init)
    for c in range(per_q):
        carry = block(i * per_q + c, carry, diag_chunk=c)
    outs = []
    for h in heads:
        acc = carry[h][1]
        outs.append(acc[:HEAD_DIM] / acc[HEAD_DIM:HEAD_DIM + 1])
    o_ref[0] = jnp.concatenate(outs, axis=0).T.astype(BF16)


def _attn_prompt(qa, ka, va, batch, seq, tq, tk):
    hp = N_HEADS // 2
    return pl.pallas_call(
        functools.partial(_attn_prompt_kernel, tq=tq, tk=tk),
        grid=(batch, hp, seq // tq),
        in_specs=[
            pl.BlockSpec((1, 2, tq, LANES), lambda b, h, i: (b, h, i, 0)),
            pl.BlockSpec((1, 2, seq, LANES), lambda b, h, i: (b, h, 0, 0)),
            pl.BlockSpec((1, 2, LANES, seq), lambda b, h, i: (b, h, 0, 0)),
        ],
        out_specs=pl.BlockSpec((1, tq, LANES), lambda b, h, i: (b, i, h)),
        out_shape=jax.ShapeDtypeStruct((batch, seq, ATTN_W), BF16),
        compiler_params=pltpu.CompilerParams(dimension_semantics=("arbitrary", "arbitrary", "arbitrary"),
                                             vmem_limit_bytes=VMEM_LIMIT),
        name="attn_prompt",
    )(qa, ka, va)


def _attn_sample_kernel(pt_ref, q_ref, kn_ref, vn_ref, lfn_ref, *refs, pp):
    k_refs = refs[:pp]
    v_refs = refs[pp:2 * pp]
    f_refs = refs[2 * pp:3 * pp]
    o_ref = refs[3 * pp]
    m_ref, l_ref, r_ref, acc_ref = refs[3 * pp + 1:]
    j = pl.program_id(1)
    hrow = lax.broadcasted_iota(jnp.int32, (N_HEADS, ATTN_W), 0)
    hlane = lax.broadcasted_iota(jnp.int32, (N_HEADS, ATTN_W), 1) // HEAD_DIM
    diag = hrow == hlane
    qf = jnp.where(diag, q_ref[0].astype(F32), 0.0)
    qm = qf.astype(BF16)

    @pl.when(j == 0)
    def _():
        m_ref[...] = jnp.sum(qf * kn_ref[0], axis=-1, keepdims=True)
        l_ref[...] = jnp.ones_like(l_ref)
        r_ref[...] = lfn_ref[0]
        acc_ref[...] = jnp.broadcast_to(vn_ref[0], (N_HEADS, ATTN_W))

    srow = lax.broadcasted_iota(jnp.int32, (PAGE_SIZE, 2 * PAGE_SIZE), 0)
    scol = lax.broadcasted_iota(jnp.int32, (PAGE_SIZE, 2 * PAGE_SIZE), 1)
    suffix = jnp.where((srow > scol) | (scol >= PAGE_SIZE), 1.0, 0.0).astype(F32)

    r = r_ref[...]
    parts = []
    for t in range(pp):
        sx = jnp.dot(f_refs[t][0], suffix, precision=lax.Precision.HIGHEST, preferred_element_type=F32)
        bias = r + sx[:, :PAGE_SIZE]
        r = r + sx[:, PAGE_SIZE:PAGE_SIZE + 1]
        parts.append(_dot(qm, k_refs[t][0].astype(BF16)) + bias * LOG2E)
    r_ref[...] = r
    s = jnp.concatenate(parts, axis=1)
    m = m_ref[...]
    m_new = jnp.maximum(m, jnp.max(s, axis=-1, keepdims=True))
    alpha = jnp.exp2(m - m_new)
    p = jnp.exp2(s - m_new)
    l_ref[...] = alpha * l_ref[...] + jnp.sum(p, axis=-1, keepdims=True)
    m_ref[...] = m_new
    pb = p.astype(BF16)
    pv = jnp.zeros((N_HEADS, ATTN_W), F32)
    for t in range(pp):
        pv = pv + lax.dot_general(pb[:, t * PAGE_SIZE:(t + 1) * PAGE_SIZE], v_refs[t][0].astype(BF16),
                                  (((1,), (1,)), ((), ())), preferred_element_type=F32)
    acc_ref[...] = alpha * acc_ref[...] + pv

    @pl.when(j == pl.num_programs(1) - 1)
    def _():
        o = jnp.where(diag, acc_ref[...] / l_ref[...], 0.0)
        o_ref[0] = jnp.sum(o, axis=0, keepdims=True).astype(BF16)


def _attn_sample(page_table, qb, k_new, v_new, lf_new, cache_k, cache_v, cache_logf, pp):
    bd, n_pages = page_table.shape
    n_pool = cache_k.shape[0]
    kc = cache_k.transpose(0, 2, 3, 1).reshape(n_pool, ATTN_W, PAGE_SIZE)
    vc = cache_v.transpose(0, 2, 3, 1).reshape(n_pool, ATTN_W, PAGE_SIZE)
    fc = cache_logf.transpose(0, 2, 1)
    steps = n_pages // pp

    def page_spec(t, rows):
        return pl.BlockSpec((1, rows, PAGE_SIZE),
                            lambda b, j, pt: (pt[b, n_pages - 1 - (j * pp + t)], 0, 0))

    per_seq = lambda shape: pl.BlockSpec((1,) + shape, lambda b, j, pt: (b, 0, 0))
    in_specs = ([per_seq((1, ATTN_W)), per_seq((1, ATTN_W)), per_seq((1, ATTN_W)), per_seq((N_HEADS, 1))]
                + [page_spec(t, ATTN_W) for t in range(pp)] * 2
                + [page_spec(t, N_HEADS) for t in range(pp)])
    grid_spec = pltpu.PrefetchScalarGridSpec(
        num_scalar_prefetch=1, grid=(bd, steps), in_specs=in_specs,
        out_specs=per_seq((1, ATTN_W)),
        scratch_shapes=[pltpu.VMEM((N_HEADS, 1), F32), pltpu.VMEM((N_HEADS, 1), F32),
                        pltpu.VMEM((N_HEADS, 1), F32), pltpu.VMEM((N_HEADS, ATTN_W), F32)],
    )
    out = pl.pallas_call(
        functools.partial(_attn_sample_kernel, pp=pp), grid_spec=grid_spec,
        out_shape=jax.ShapeDtypeStruct((bd, 1, ATTN_W), BF16),
        compiler_params=pltpu.CompilerParams(dimension_semantics=("arbitrary", "arbitrary"),
                                             vmem_limit_bytes=VMEM_LIMIT),
        name="attn_sample",
    )(page_table, qb.reshape(bd, 1, ATTN_W), k_new.reshape(bd, 1, ATTN_W), v_new.reshape(bd, 1, ATTN_W),
      lf_new.reshape(bd, N_HEADS, 1), *([kc] * pp), *([vc] * pp), *([fc] * pp))
    return out.reshape(bd, ATTN_W)


def _mix_kernel(x_ref, a_ref, y_ref, sg_ref, wao_ref, wco_ref, wout_ref, gffn_ref, wr_ref, br_ref,
                x1_ref, h2_ref, gates_ref):
    d = x_ref.shape[-1]
    ao = _dot(a_ref[...], wao_ref[...])
    co = _dot(y_ref[...], wco_ref[...])
    merged = sg_ref[:, :d] * ao + sg_ref[:, d:] * co
    x1 = x_ref[...] + _dot(merged.astype(BF16), wout_ref[...])
    x1_ref[...] = x1
    h2 = _rms(x1, gffn_ref[...]).astype(BF16)
    h2_ref[...] = h2

    logits = _dot(h2, wr_ref[...]) + br_ref[...]
    lane = lax.broadcasted_iota(jnp.int32, logits.shape, 1)
    neg = -jnp.inf
    big = jnp.int32(4 * LANES)
    gl = jnp.where(lane < N_GROUPS, logits, neg)
    gmax = jnp.max(gl, axis=-1, keepdims=True)
    gval = 1.0 / jnp.sum(jnp.exp(gl - gmax), axis=-1, keepdims=True)
    gidx = jnp.min(jnp.where(gl == gmax, lane, big), axis=-1, keepdims=True)
    ex = lane - EXPERT_LANE0
    in_group = (ex >= gidx * EXP_PER_GROUP) & (ex < (gidx + 1) * EXP_PER_GROUP)
    el = jnp.where(in_group, logits, neg)
    v1 = jnp.max(el, axis=-1, keepdims=True)
    i1 = jnp.min(jnp.where(el == v1, lane, big), axis=-1, keepdims=True)
    el2 = jnp.where(lane == i1, neg, el)
    v2 = jnp.max(el2, axis=-1, keepdims=True)
    i2 = jnp.min(jnp.where(el2 == v2, lane, big), axis=-1, keepdims=True)
    e2 = jnp.exp(v2 - v1)
    w1 = gval / (1.0 + e2)
    w2 = gval * e2 / (1.0 + e2)
    gates_ref[...] = jnp.where(lane == i1, w1, 0.0) + jnp.where(lane == i2, w2, 0.0)


def _mix(x2d, attn, y, sg, mw, tm):
    n, d = x2d.shape
    row = lambda w: pl.BlockSpec((tm, w), lambda i: (i, 0))
    return pl.pallas_call(
        _mix_kernel, grid=(n // tm,),
        in_specs=[row(d), row(ATTN_W), row(CONV_CH), row(2 * d)] + [_const_spec(a.shape) for a in mw],
        out_specs=(row(d), row(d), row(LANES)),
        out_shape=(jax.ShapeDtypeStruct((n, d), F32), jax.ShapeDtypeStruct((n, d), BF16),
                   jax.ShapeDtypeStruct((n, LANES), F32)),
        compiler_params=pltpu.CompilerParams(dimension_semantics=("arbitrary",), vmem_limit_bytes=VMEM_LIMIT),
        name="mix",
    )(x2d, attn, y, sg, *mw)


def _moe_kernel(x1_ref, h2_ref, gates_ref, p_ref, wg_ref, wu_ref, wd_ref, gple_ref, wpg_ref, wpp_ref,
                o_ref, acc_ref, *, eb):
    jb = pl.program_id(1)

    @pl.when(jb == 0)
    def _():
        acc_ref[...] = jnp.zeros_like(acc_ref)

    h2 = h2_ref[...]
    gates = gates_ref[...]
    lane = lax.broadcasted_iota(jnp.int32, gates.shape, 1)
    for e in range(eb):
        gcol = jnp.sum(jnp.where(lane == EXPERT_LANE0 + jb * eb + e, gates, 0.0), axis=-1, keepdims=True)
        hg = _dot(h2, wg_ref[e])
        hu = _dot(h2, wu_ref[e])
        he = (hg * _sigmoid(hg)) * hu * gcol
        acc_ref[...] += _dot(he.astype(BF16), wd_ref[e])

    @pl.when(jb == pl.num_programs(1) - 1)
    def _():
        x2 = x1_ref[...] + acc_ref[...]
        g = _sigmoid(_dot(_rms(x2, gple_ref[...]).astype(BF16), wpg_ref[...]))
        o_ref[...] = x2 + g * _dot(p_ref[...].astype(BF16), wpp_ref[...])


def _moe_ple(x1, h2, gates, p2d, ew, pw, tm, eb):
    n, d = x1.shape
    wg, wu, wd = ew
    row = lambda w: pl.BlockSpec((tm, w), lambda i, j: (i, 0))
    return pl.pallas_call(
        functools.partial(_moe_kernel, eb=eb), grid=(n // tm, N_EXPERTS // eb),
        in_specs=[row(d), row(d), row(LANES), row(p2d.shape[1]),
                  pl.BlockSpec((eb, d, EXPERT_FF), lambda i, j: (j, 0, 0)),
                  pl.BlockSpec((eb, d, EXPERT_FF), lambda i, j: (j, 0, 0)),
                  pl.BlockSpec((eb, EXPERT_FF, d), lambda i, j: (j, 0, 0))]
                 + [_const_spec(a.shape) for a in pw],
        out_specs=row(d), out_shape=jax.ShapeDtypeStruct((n, d), F32),
        scratch_shapes=[pltpu.VMEM((tm, d), F32)],
        compiler_params=pltpu.CompilerParams(dimension_semantics=("arbitrary", "arbitrary"),
                                             vmem_limit_bytes=VMEM_LIMIT),
        name="moe_ple",
    )(x1, h2, gates, p2d, wg, wu, wd, *pw)


def _tile(n, pref):
    return pref if n % pref == 0 else n


def kernel(x_prompt, x_sample, cache_k, cache_v, cache_logf, state_conv, page_table, p_prompt, p_sample, norm_mix_g, w_in, b_forget, q_norm_g, k_norm_g, w_attn_o, conv_dw_w, conv_dw_b, conv_ln_g, conv_ln_b, w_conv_o, w_out, norm_ffn_g, w_router_group, b_router_group, w_router_expert, b_router_expert, w_exp_gate, w_exp_up, w_exp_down, norm_ple_g, w_ple_gate, w_ple_proj):
    depth = w_in.shape[0]
    assert depth == 1
    li = 0
    batch, seq, d = x_prompt.shape
    bd, dec_seq, _ = x_sample.shape
    assert dec_seq == 1
    n_pages = page_table.shape[1]

    pw = _prep_proj_weights(norm_mix_g[li], w_in[li], b_forget[li], q_norm_g[li], k_norm_g[li])
    conv_w = (conv_dw_w[li], conv_dw_b[li].reshape(1, CONV_CH), conv_ln_g[li].reshape(1, CONV_CH),
              conv_ln_b[li].reshape(1, CONV_CH))
    wr = jnp.zeros((d, LANES), F32)
    wr = wr.at[:, :N_GROUPS].set(w_router_group[li]).at[:, EXPERT_LANE0:EXPERT_LANE0 + N_EXPERTS].set(
        w_router_expert[li]).astype(BF16)
    br = jnp.zeros((1, LANES), F32)
    br = br.at[0, :N_GROUPS].set(b_router_group[li]).at[0, EXPERT_LANE0:EXPERT_LANE0 + N_EXPERTS].set(
        b_router_expert[li])
    mw = (w_attn_o[li].astype(BF16), w_conv_o[li].astype(BF16), w_out[li].astype(BF16),
          norm_ffn_g[li].reshape(1, d), wr, br)
    ew = (w_exp_gate[li].astype(BF16), w_exp_up[li].astype(BF16), w_exp_down[li].astype(BF16))
    plew = (norm_ple_g[li].reshape(1, d), w_ple_gate[li].astype(BF16), w_ple_proj[li].astype(BF16))

    xp = x_prompt.reshape(batch * seq, d)
    tm = _tile(seq, 512)
    qa, ka, va, kt, vt, lf, y, sg, utail = _proj_prompt(xp, pw, conv_w, batch, seq, tm)
    attn = _attn_prompt(qa, ka, va, batch, seq, _tile(seq, 1024), 1024).reshape(batch * seq, ATTN_W)
    x1, h2, gates = _mix(xp, attn, y, sg, mw, tm)
    yp = _moe_ple(x1, h2, gates, p_prompt[li].reshape(batch * seq, -1), ew, plew, _tile(batch * seq, 1024), 4)
    y_prompt = yp.reshape(batch, seq, d)
    to_rows = lambda t: t.reshape(batch, N_HEADS, HEAD_DIM, seq).transpose(0, 3, 1, 2)[None]
    new_k_prompt = to_rows(kt)
    new_v_prompt = to_rows(vt)
    new_logf_prompt = lf.reshape(1, batch, seq, N_HEADS)
    new_conv_prompt = utail[:, CONV_HALO - (CONV_K - 1):, :].reshape(1, batch, CONV_K - 1, CONV_CH)

    xs = x_sample.reshape(bd, d)
    state_t = state_conv[li].transpose(1, 0, 2)
    qs, ks, vs, lfs, us, ys, sgs = _proj_sample(xs, pw, state_t, conv_w)
    attn_s = _attn_sample(page_table, qs, ks, vs, lfs, cache_k[li], cache_v[li], cache_logf[li],
                          16 if n_pages % 16 == 0 else 8)
    x1s, h2s, gates_s = _mix(xs, attn_s, ys, sgs, mw, bd)
    ysm = _moe_ple(x1s, h2s, gates_s, p_sample[li].reshape(bd, -1), ew, plew, bd, 4)
    y_sample = ysm.reshape(bd, 1, d)
    new_k_sample = ks.reshape(1, bd, 1, N_HEADS, HEAD_DIM)
    new_v_sample = vs.reshape(1, bd, 1, N_HEADS, HEAD_DIM)
    new_logf_sample = lfs.reshape(1, bd, 1, N_HEADS)
    new_conv_sample = jnp.concatenate([state_conv[li][:, 1:, :], us[:, None, :]], axis=1)[None]

    return (y_prompt, y_sample, new_k_prompt, new_v_prompt, new_logf_prompt, new_conv_prompt,
            new_k_sample, new_v_sample, new_logf_sample, new_conv_sample)
```

```python
import functools

import jax
import jax.numpy as jnp
from jax import lax
from jax.experimental import pallas as pl
from jax.experimental.pallas import tpu as pltpu

F32 = jnp.float32
BF16 = jnp.bfloat16

N_HEADS = 8
HEAD_DIM = 64
ATTN_W = N_HEADS * HEAD_DIM
CONV_CH = 512
CONV_K = 31
N_GROUPS = 4
EXP_PER_GROUP = 8
N_EXPERTS = N_GROUPS * EXP_PER_GROUP
EXPERT_FF = 256
PAGE_SIZE = 128
EPS = 1e-6

LANES = 128
CONV_HALO = 32
VMEM_LIMIT = 56 * 1024 * 1024
EXPERT_LANE0 = 32
LOG2E = 1.4426950408889634
AUG_K0 = 64
AUG_Q0 = 88
MOE_TILE = 1024
MOE_BLOCK_ROWS = 320


def _const_spec(shape):
    nd = len(shape)
    return pl.BlockSpec(shape, lambda *_: (0,) * nd, pipeline_mode=pl.Buffered(1))


def _sigmoid(x):
    return 1.0 / (1.0 + jnp.exp(-x))


def _log_sigmoid(x):
    return -(jnp.maximum(-x, 0.0) + jnp.log1p(jnp.exp(-jnp.abs(x))))


def _rms(x, g):
    return x * lax.rsqrt(jnp.mean(x * x, axis=-1, keepdims=True) + EPS) * g


def _dot(a, b):
    return jnp.dot(a, b, preferred_element_type=F32)


def _project(x, gmix, wqkv, wf, wglu, wgate, bf, qg, kg, gsum):
    hb = _rms(x, gmix).astype(BF16)
    zqkv = _dot(hb, wqkv)
    zq = zqkv[:, :ATTN_W]
    zk = zqkv[:, ATTN_W:2 * ATTN_W]
    zv = zqkv[:, 2 * ATTN_W:]

    def head_norm(z, g):
        ss = _dot((z * z).astype(BF16), gsum)
        return z * lax.rsqrt(ss * (1.0 / HEAD_DIM) + EPS) * g

    q = head_norm(zq, qg) * (HEAD_DIM ** -0.5 * LOG2E)
    k = head_norm(zk, kg)
    zf = _dot(hb, wf) + bf
    lane = lax.broadcasted_iota(jnp.int32, zf.shape, 1)
    logf = jnp.where(lane < N_HEADS, _log_sigmoid(zf), 0.0)
    zglu = _dot(hb, wglu)
    u = zglu[:, :CONV_CH] * _sigmoid(zglu[:, CONV_CH:])
    sg = _sigmoid(_dot(hb, wgate))
    return q, k, zv, logf, u, sg


def _ln_silu(y, g, b):
    mu = jnp.mean(y, axis=-1, keepdims=True)
    d = y - mu
    var = jnp.mean(d * d, axis=-1, keepdims=True)
    z = d * lax.rsqrt(var + EPS) * g + b
    return z * _sigmoid(z)


def _split3(x):
    a = x.astype(BF16)
    r = x - a.astype(F32)
    b = r.astype(BF16)
    c = (r - b.astype(F32)).astype(BF16)
    return a, b, c


def _proj_prompt_kernel(x_ref, gmix_ref, wqkv_ref, wf_ref, wglu_ref, wgate_ref, bf_ref, qg_ref, kg_ref,
                        gsum_ref, tri_ref, wdw_ref, bdw_ref, lng_ref, lnb_ref,
                        qa_ref, ka_ref, va_ref, kt_ref, vt_ref, lf_ref, y_ref, sg_ref, utail_ref,
                        ext_ref, carry_ref, *, tiles_per_seq, tm):
    i = pl.program_id(0)

    @pl.when(i % tiles_per_seq == 0)
    def _():
        ext_ref[0:CONV_HALO, :] = jnp.zeros((CONV_HALO, CONV_CH), F32)
        carry_ref[...] = jnp.zeros_like(carry_ref)

    q, k, v, logf, u, sg = _project(x_ref[...], gmix_ref[...], wqkv_ref[...], wf_ref[...], wglu_ref[...],
                                    wgate_ref[...], bf_ref[...], qg_ref[...], kg_ref[...], gsum_ref[...])
    sg_ref[...] = sg
    lf_ref[...] = logf[:, :N_HEADS]

    tri = tri_ref[...]
    a, b, c3 = _split3(logf)
    c = _dot(tri, a) + _dot(tri, b) + _dot(tri, c3) + carry_ref[0:1, :]
    carry_ref[0:1, :] = c[tm - 1:tm, :]

    ca, cb, cc = (t.astype(F32) for t in _split3(c * LOG2E))
    tail_k = -(pltpu.roll(ca, AUG_K0, 1) + pltpu.roll(cb, AUG_K0 + 8, 1) + pltpu.roll(cc, AUG_K0 + 16, 1))
    tail_q = pltpu.roll(ca, AUG_Q0, 1) + pltpu.roll(cb, AUG_Q0 + 8, 1) + pltpu.roll(cc, AUG_Q0 + 16, 1)
    lane = lax.broadcasted_iota(jnp.int32, (1, LANES), 1)
    low = lane < HEAD_DIM
    for h in range(N_HEADS):
        blk = slice((h // 2) * LANES, (h // 2 + 1) * LANES)
        qh, kh = q[:, blk], k[:, blk]
        if h % 2:
            qh, kh = pltpu.roll(qh, HEAD_DIM, 1), pltpu.roll(kh, HEAD_DIM, 1)
        pick_k = ((lane == AUG_K0 + h) | (lane == AUG_K0 + 8 + h) | (lane == AUG_K0 + 16 + h)).astype(F32)
        pick_q = ((lane == AUG_Q0 + h) | (lane == AUG_Q0 + 8 + h) | (lane == AUG_Q0 + 16 + h)).astype(F32)
        qa_ref[0, h] = jnp.where(low, qh, tail_q + pick_k).astype(BF16)
        ka_ref[0, h] = jnp.where(low, kh, tail_k + pick_q).astype(BF16)

    kt_ref[0] = k.T
    vt = v.T
    vt_ref[0] = vt
    row = lax.broadcasted_iota(jnp.int32, (LANES - HEAD_DIM, tm), 0)
    ones_row = jnp.where(row == 0, 1.0, 0.0)
    for h in range(N_HEADS):
        va_ref[0, h] = jnp.concatenate([vt[h * HEAD_DIM:(h + 1) * HEAD_DIM], ones_row], axis=0).astype(BF16)

    ext_ref[CONV_HALO:CONV_HALO + tm, :] = u
    off = CONV_HALO - (CONV_K - 1)
    rc = min(tm, 128)
    for r0 in range(0, tm, rc):
        for l0 in range(0, CONV_CH, LANES):
            acc = jnp.zeros((rc, LANES), F32)
            for j in range(CONV_K):
                acc = acc + ext_ref[off + j + r0:off + j + r0 + rc, l0:l0 + LANES] * wdw_ref[j:j + 1, l0:l0 + LANES]
            ext_ref[CONV_HALO + tm + r0:CONV_HALO + tm + r0 + rc, l0:l0 + LANES] = acc
    yc = ext_ref[CONV_HALO + tm:CONV_HALO + 2 * tm, :] + bdw_ref[...]
    y_ref[...] = _ln_silu(yc, lng_ref[...], lnb_ref[...]).astype(BF16)
    tail = u[tm - CONV_HALO:, :]
    ext_ref[0:CONV_HALO, :] = tail
    utail_ref[0] = tail


def _proj_sample_kernel(x_ref, gmix_ref, wqkv_ref, wf_ref, wglu_ref, wgate_ref, bf_ref, qg_ref, kg_ref,
                        gsum_ref, st_ref, wdw_ref, bdw_ref, lng_ref, lnb_ref,
                        qb_ref, k32_ref, v32_ref, lf_ref, u_ref, y_ref, sg_ref):
    q, k, v, logf, u, sg = _project(x_ref[...], gmix_ref[...], wqkv_ref[...], wf_ref[...], wglu_ref[...],
                                    wgate_ref[...], bf_ref[...], qg_ref[...], kg_ref[...], gsum_ref[...])
    qb_ref[...] = q.astype(BF16)
    k32_ref[...] = k
    v32_ref[...] = v
    sg_ref[...] = sg
    lf_ref[...] = logf[:, :N_HEADS]
    u_ref[...] = u
    acc = u * wdw_ref[CONV_K - 1:CONV_K, :]
    for j in range(CONV_K - 1):
        acc = acc + st_ref[j] * wdw_ref[j:j + 1, :]
    y_ref[...] = _ln_silu(acc + bdw_ref[...], lng_ref[...], lnb_ref[...]).astype(BF16)


def _prep_proj_weights(norm_mix_g, w_in, b_forget, q_norm_g, k_norm_g):
    d = w_in.shape[0]
    o_f = 3 * ATTN_W
    o_glu = o_f + N_HEADS
    o_gate = o_glu + 2 * CONV_CH
    wqkv = w_in[:, :o_f].astype(BF16)
    wf = jnp.pad(w_in[:, o_f:o_glu], ((0, 0), (0, LANES - N_HEADS))).astype(BF16)
    wglu = w_in[:, o_glu:o_gate].astype(BF16)
    wgate = w_in[:, o_gate:].astype(BF16)
    bf = jnp.pad(b_forget, (0, LANES - N_HEADS)).reshape(1, LANES)
    qg = jnp.tile(q_norm_g, N_HEADS).reshape(1, ATTN_W)
    kg = jnp.tile(k_norm_g, N_HEADS).reshape(1, ATTN_W)
    hid = jnp.arange(ATTN_W) // HEAD_DIM
    gsum = (hid[:, None] == hid[None, :]).astype(BF16)
    return (norm_mix_g.reshape(1, d), wqkv, wf, wglu, wgate, bf, qg, kg, gsum)


def _proj_prompt(x2d, pw, conv_w, batch, seq, tm):
    n, d = x2d.shape
    tps = seq // tm
    tri = (jnp.arange(tm)[:, None] >= jnp.arange(tm)[None, :]).astype(BF16)
    wdw, bdw, lng, lnb = conv_w
    row = lambda w: pl.BlockSpec((tm, w), lambda i: (i, 0))
    in_specs = ([row(d)] + [_const_spec(a.shape) for a in pw] + [_const_spec(tri.shape)]
                + [_const_spec(a.shape) for a in conv_w])
    head_rows = jax.ShapeDtypeStruct((batch, N_HEADS, seq, LANES), BF16)
    head_rows_spec = pl.BlockSpec((1, N_HEADS, tm, LANES), lambda i: (i // tps, 0, i % tps, 0))
    seq_minor = jax.ShapeDtypeStruct((batch, ATTN_W, seq), F32)
    seq_minor_spec = pl.BlockSpec((1, ATTN_W, tm), lambda i: (i // tps, 0, i % tps))
    out_shape = (
        head_rows, head_rows, jax.ShapeDtypeStruct((batch, N_HEADS, LANES, seq), BF16), seq_minor, seq_minor,
        jax.ShapeDtypeStruct((n, N_HEADS), F32), jax.ShapeDtypeStruct((n, CONV_CH), BF16),
        jax.ShapeDtypeStruct((n, 2 * d), F32), jax.ShapeDtypeStruct((batch, CONV_HALO, CONV_CH), F32),
    )
    out_specs = (head_rows_spec, head_rows_spec,
                 pl.BlockSpec((1, N_HEADS, LANES, tm), lambda i: (i // tps, 0, 0, i % tps)),
                 seq_minor_spec, seq_minor_spec, row(N_HEADS),
                 row(CONV_CH), row(2 * d), pl.BlockSpec((1, CONV_HALO, CONV_CH), lambda i: (i // tps, 0, 0)))
    return pl.pallas_call(
        functools.partial(_proj_prompt_kernel, tiles_per_seq=tps, tm=tm),
        grid=(n // tm,), in_specs=in_specs, out_specs=out_specs, out_shape=out_shape,
        scratch_shapes=[pltpu.VMEM((CONV_HALO + 2 * tm, CONV_CH), F32), pltpu.VMEM((8, LANES), F32)],
        compiler_params=pltpu.CompilerParams(dimension_semantics=("arbitrary",), vmem_limit_bytes=VMEM_LIMIT),
        name="proj_prompt",
    )(x2d, *pw, tri, wdw, bdw, lng, lnb)


def _proj_sample(x2d, pw, state_t, conv_w):
    n, d = x2d.shape
    args = (x2d,) + tuple(pw) + (state_t,) + tuple(conv_w)
    full = lambda a: pl.BlockSpec(a.shape, lambda i, nd=a.ndim: (0,) * nd)
    out_shape = (
        jax.ShapeDtypeStruct((n, ATTN_W), BF16), jax.ShapeDtypeStruct((n, ATTN_W), F32),
        jax.ShapeDtypeStruct((n, ATTN_W), F32), jax.ShapeDtypeStruct((n, N_HEADS), F32),
        jax.ShapeDtypeStruct((n, CONV_CH), F32), jax.ShapeDtypeStruct((n, CONV_CH), BF16),
        jax.ShapeDtypeStruct((n, 2 * d), F32),
    )
    return pl.pallas_call(
        _proj_sample_kernel, grid=(1,), in_specs=[full(a) for a in args],
        out_specs=tuple(full(s) for s in out_shape), out_shape=out_shape,
        compiler_params=pltpu.CompilerParams(dimension_semantics=("arbitrary",), vmem_limit_bytes=VMEM_LIMIT),
        name="proj_sample",
    )(*args)


def _attn_prompt_kernel(q_ref, k_ref, v_ref, o_ref, *, tq, tk):
    i = pl.program_id(2)
    heads = range(2)
    qs = [q_ref[0, h] for h in heads]
    per_q = tq // tk

    def block(j, carry, diag_chunk=None):
        start = pl.multiple_of(j * tk, tk)
        new = []
        for h in heads:
            m, acc = carry[h]
            ks = k_ref[0, h, pl.ds(start, tk), :]
            st = lax.dot_general(ks, qs[h], (((1,), (1,)), ((), ())), preferred_element_type=F32)
            if diag_chunk is not None:
                key = lax.broadcasted_iota(jnp.int32, (tk, tq), 0) + diag_chunk * tk
                qry = lax.broadcasted_iota(jnp.int32, (tk, tq), 1)
                st = jnp.where(key <= qry, st, -jnp.inf)
            m_new = jnp.maximum(m, jnp.max(st, axis=0, keepdims=True))
            alpha = jnp.exp2(m - m_new)
            pt = jnp.exp2(st - m_new).astype(BF16)
            vs = v_ref[0, h, :, pl.ds(start, tk)]
            new.append((m_new, alpha * acc + _dot(vs, pt)))
        return tuple(new)

    init = tuple((jnp.full((1, tq), -jnp.inf, F32), jnp.zeros((LANES, tq), F32)) for _ in heads)
    carry = lax.fori_loop(0, i * per_q, block, init)
    for c in range(per_q):
        carry = block(i * per_q + c, carry, diag_chunk=c)
    outs = []
    for h in heads:
        acc = carry[h][1]
        outs.append(acc[:HEAD_DIM] / acc[HEAD_DIM:HEAD_DIM + 1])
    o_ref[0] = jnp.concatenate(outs, axis=0).T.astype(BF16)


def _attn_prompt(qa, ka, va, batch, seq, tq, tk):
    hp = N_HEADS // 2
    return pl.pallas_call(
        functools.partial(_attn_prompt_kernel, tq=tq, tk=tk),
        grid=(batch, hp, seq // tq),
        in_specs=[
            pl.BlockSpec((1, 2, tq, LANES), lambda b, h, i: (b, h, i, 0)),
            pl.BlockSpec((1, 2, seq, LANES), lambda b, h, i: (b, h, 0, 0)),
            pl.BlockSpec((1, 2, LANES, seq), lambda b, h, i: (b, h, 0, 0)),
        ],
        out_specs=pl.BlockSpec((1, tq, LANES), lambda b, h, i: (b, i, h)),
        out_shape=jax.ShapeDtypeStruct((batch, seq, ATTN_W), BF16),
        compiler_params=pltpu.CompilerParams(dimension_semantics=("arbitrary", "arbitrary", "arbitrary"),
                                             vmem_limit_bytes=VMEM_LIMIT),
        name="attn_prompt",
    )(qa, ka, va)


def _attn_sample_kernel(pt_ref, q_ref, kn_ref, vn_ref, lfn_ref, *refs, pp):
    k_refs = refs[:pp]
    v_refs = refs[pp:2 * pp]
    f_refs = refs[2 * pp:3 * pp]
    o_ref = refs[3 * pp]
    m_ref, l_ref, r_ref, acc_ref = refs[3 * pp + 1:]
    j = pl.program_id(1)
    hrow = lax.broadcasted_iota(jnp.int32, (N_HEADS, ATTN_W), 0)
    hlane = lax.broadcasted_iota(jnp.int32, (N_HEADS, ATTN_W), 1) // HEAD_DIM
    diag = hrow == hlane
    qf = jnp.where(diag, q_ref[0].astype(F32), 0.0)
    qm = qf.astype(BF16)

    @pl.when(j == 0)
    def _():
        m_ref[...] = jnp.sum(qf * kn_ref[0], axis=-1, keepdims=True)
        l_ref[...] = jnp.ones_like(l_ref)
        r_ref[...] = lfn_ref[0]
        acc_ref[...] = jnp.broadcast_to(vn_ref[0], (N_HEADS, ATTN_W))

    srow = lax.broadcasted_iota(jnp.int32, (PAGE_SIZE, 2 * PAGE_SIZE), 0)
    scol = lax.broadcasted_iota(jnp.int32, (PAGE_SIZE, 2 * PAGE_SIZE), 1)
    suffix = jnp.where((srow > scol) | (scol >= PAGE_SIZE), 1.0, 0.0).astype(F32)

    r = r_ref[...]
    parts = []
    for t in range(pp):
        sx = jnp.dot(f_refs[t][0], suffix, precision=lax.Precision.HIGHEST, preferred_element_type=F32)
        bias = r + sx[:, :PAGE_SIZE]
        r = r + sx[:, PAGE_SIZE:PAGE_SIZE + 1]
        parts.append(_dot(qm, k_refs[t][0].astype(BF16)) + bias * LOG2E)
    r_ref[...] = r
    s = jnp.concatenate(parts, axis=1)
    m = m_ref[...]
    m_new = jnp.maximum(m, jnp.max(s, axis=-1, keepdims=True))
    alpha = jnp.exp2(m - m_new)
    p = jnp.exp2(s - m_new)
    l_ref[...] = alpha * l_ref[...] + jnp.sum(p, axis=-1, keepdims=True)
    m_ref[...] = m_new
    pb = p.astype(BF16)
    pv = jnp.zeros((N_HEADS, ATTN_W), F32)
    for t in range(pp):
        pv = pv + lax.dot_general(pb[:, t * PAGE_SIZE:(t + 1) * PAGE_SIZE], v_refs[t][0].astype(BF16),
                                  (((1,), (1,)), ((), ())), preferred_element_type=F32)
    acc_ref[...] = alpha * acc_ref[...] + pv

    @pl.when(j == pl.num_programs(1) - 1)
    def _():
        o = jnp.where(diag, acc_ref[...] / l_ref[...], 0.0)
        o_ref[0] = jnp.sum(o, axis=0, keepdims=True).astype(BF16)


def _attn_sample(page_table, qb, k_new, v_new, lf_new, cache_k, cache_v, cache_logf, pp):
    bd, n_pages = page_table.shape
    n_pool = cache_k.shape[0]
    kc = cache_k.transpose(0, 2, 3, 1).reshape(n_pool, ATTN_W, PAGE_SIZE)
    vc = cache_v.transpose(0, 2, 3, 1).reshape(n_pool, ATTN_W, PAGE_SIZE)
    fc = cache_logf.transpose(0, 2, 1)
    steps = n_pages // pp

    def page_spec(t, rows):
        return pl.BlockSpec((1, rows, PAGE_SIZE),
                            lambda b, j, pt: (pt[b, n_pages - 1 - (j * pp + t)], 0, 0))

    per_seq = lambda shape: pl.BlockSpec((1,) + shape, lambda b, j, pt: (b, 0, 0))
    in_specs = ([per_seq((1, ATTN_W)), per_seq((1, ATTN_W)), per_seq((1, ATTN_W)), per_seq((N_HEADS, 1))]
                + [page_spec(t, ATTN_W) for t in range(pp)] * 2
                + [page_spec(t, N_HEADS) for t in range(pp)])
    grid_spec = pltpu.PrefetchScalarGridSpec(
        num_scalar_prefetch=1, grid=(bd, steps), in_specs=in_specs,
        out_specs=per_seq((1, ATTN_W)),
        scratch_shapes=[pltpu.VMEM((N_HEADS, 1), F32), pltpu.VMEM((N_HEADS, 1), F32),
                        pltpu.VMEM((N_HEADS, 1), F32), pltpu.VMEM((N_HEADS, ATTN_W), F32)],
    )
    out = pl.pallas_call(
        functools.partial(_attn_sample_kernel, pp=pp), grid_spec=grid_spec,
        out_shape=jax.ShapeDtypeStruct((bd, 1, ATTN_W), BF16),
        compiler_params=pltpu.CompilerParams(dimension_semantics=("arbitrary", "arbitrary"),
                                             vmem_limit_bytes=VMEM_LIMIT),
        name="attn_sample",
    )(page_table, qb.reshape(bd, 1, ATTN_W), k_new.reshape(bd, 1, ATTN_W), v_new.reshape(bd, 1, ATTN_W),
      lf_new.reshape(bd, N_HEADS, 1), *([kc] * pp), *([vc] * pp), *([fc] * pp))
    return out.reshape(bd, ATTN_W)


def _mix_kernel(x_ref, a_ref, y_ref, sg_ref, wao_ref, wco_ref, wout_ref, gffn_ref, wr_ref, br_ref,
                x1_ref, h2_ref, gates_ref):
    d = x_ref.shape[-1]
    ao = _dot(a_ref[...], wao_ref[...])
    co = _dot(y_ref[...], wco_ref[...])
    merged = sg_ref[:, :d] * ao + sg_ref[:, d:] * co
    x1 = x_ref[...] + _dot(merged.astype(BF16), wout_ref[...])
    x1_ref[...] = x1
    h2 = _rms(x1, gffn_ref[...]).astype(BF16)
    h2_ref[...] = h2

    logits = _dot(h2, wr_ref[...]) + br_ref[...]
    lane = lax.broadcasted_iota(jnp.int32, logits.shape, 1)
    neg = -jnp.inf
    big = jnp.int32(4 * LANES)
    gl = jnp.where(lane < N_GROUPS, logits, neg)
    gmax = jnp.max(gl, axis=-1, keepdims=True)
    gval = 1.0 / jnp.sum(jnp.exp(gl - gmax), axis=-1, keepdims=True)
    gidx = jnp.min(jnp.where(gl == gmax, lane, big), axis=-1, keepdims=True)
    ex = lane - EXPERT_LANE0
    in_group = (ex >= gidx * EXP_PER_GROUP) & (ex < (gidx + 1) * EXP_PER_GROUP)
    el = jnp.where(in_group, logits, neg)
    v1 = jnp.max(el, axis=-1, keepdims=True)
    i1 = jnp.min(jnp.where(el == v1, lane, big), axis=-1, keepdims=True)
    el2 = jnp.where(lane == i1, neg, el)
    v2 = jnp.max(el2, axis=-1, keepdims=True)
    i2 = jnp.min(jnp.where(el2 == v2, lane, big), axis=-1, keepdims=True)
    e2 = jnp.exp(v2 - v1)
    w1 = gval / (1.0 + e2)
    w2 = gval * e2 / (1.0 + e2)
    gates_ref[...] = (jnp.where(lane == i1, w1, 0.0) + jnp.where(lane == i2, w2, 0.0)
                      + jnp.where(lane == gidx, 1.0, 0.0))


def _mix(x2d, attn, y, sg, mw, tm):
    n, d = x2d.shape
    row = lambda w: pl.BlockSpec((tm, w), lambda i: (i, 0))
    return pl.pallas_call(
        _mix_kernel, grid=(n // tm,),
        in_specs=[row(d), row(ATTN_W), row(CONV_CH), row(2 * d)] + [_const_spec(a.shape) for a in mw],
        out_specs=(row(d), row(d), row(LANES)),
        out_shape=(jax.ShapeDtypeStruct((n, d), F32), jax.ShapeDtypeStruct((n, d), BF16),
                   jax.ShapeDtypeStruct((n, LANES), F32)),
        compiler_params=pltpu.CompilerParams(dimension_semantics=("arbitrary",), vmem_limit_bytes=VMEM_LIMIT),
        name="mix",
    )(x2d, attn, y, sg, *mw)


def _moe_kernel(x1_ref, h2_ref, gates_ref, p_ref, wg_ref, wu_ref, wd_ref, gple_ref, wpg_ref, wpp_ref,
                o_ref, acc_ref, *, eb):
    jb = pl.program_id(1)

    @pl.when(jb == 0)
    def _():
        acc_ref[...] = jnp.zeros_like(acc_ref)

    h2 = h2_ref[...]
    gates = gates_ref[...]
    lane = lax.broadcasted_iota(jnp.int32, gates.shape, 1)
    for e in range(eb):
        gcol = jnp.sum(jnp.where(lane == EXPERT_LANE0 + jb * eb + e, gates, 0.0), axis=-1, keepdims=True)
        hg = _dot(h2, wg_ref[e])
        hu = _dot(h2, wu_ref[e])
        he = (hg * _sigmoid(hg)) * hu * gcol
        acc_ref[...] += _dot(he.astype(BF16), wd_ref[e])

    @pl.when(jb == pl.num_programs(1) - 1)
    def _():
        x2 = x1_ref[...] + acc_ref[...]
        g = _sigmoid(_dot(_rms(x2, gple_ref[...]).astype(BF16), wpg_ref[...]))
        o_ref[...] = x2 + g * _dot(p_ref[...].astype(BF16), wpp_ref[...])


def _moe_ple(x1, h2, gates, p2d, ew, pw, tm, eb):
    n, d = x1.shape
    wg, wu, wd = ew
    row = lambda w: pl.BlockSpec((tm, w), lambda i, j: (i, 0))
    return pl.pallas_call(
        functools.partial(_moe_kernel, eb=eb), grid=(n // tm, N_EXPERTS // eb),
        in_specs=[row(d), row(d), row(LANES), row(p2d.shape[1]),
                  pl.BlockSpec((eb, d, EXPERT_FF), lambda i, j: (j, 0, 0)),
                  pl.BlockSpec((eb, d, EXPERT_FF), lambda i, j: (j, 0, 0)),
                  pl.BlockSpec((eb, EXPERT_FF, d), lambda i, j: (j, 0, 0))]
                 + [_const_spec(a.shape) for a in pw],
        out_specs=row(d), out_shape=jax.ShapeDtypeStruct((n, d), F32),
        scratch_shapes=[pltpu.VMEM((tm, d), F32)],
        compiler_params=pltpu.CompilerParams(dimension_semantics=("arbitrary", "arbitrary"),
                                             vmem_limit_bytes=VMEM_LIMIT),
        name="moe_ple",
    )(x1, h2, gates, p2d, wg, wu, wd, *pw)


def _moe_group_kernel(h2_ref, gates_ref, tri_ref, wg_ref, wu_ref, wd_ref, o_ref, route_ref, gsplit_ref,
                      *, tm, rows):
    g = pl.program_id(1)

    @pl.when(g == 0)
    def _():
        o_ref[...] = jnp.zeros_like(o_ref)
        gates = gates_ref[...]
        onehot_t = gates.T[0:8]
        route_ref[0:8, :] = onehot_t
        route_ref[8:16, :] = _dot(onehot_t.astype(BF16), tri_ref[...])
        a, b, c = _split3(gates)
        gsplit_ref[...] = jnp.concatenate([a, b, c], axis=1)

    member = route_ref[pl.ds(g, 1), :]
    rank = route_ref[pl.ds(8 + g, 1), :].astype(jnp.int32)
    count = jnp.sum(member).astype(jnp.int32)
    slot = lax.broadcasted_iota(jnp.int32, (rows, tm), 0)
    lane = lax.broadcasted_iota(jnp.int32, (1, LANES), 1)

    def block(b, carry):
        pick = jnp.where(slot == rank - b * rows, member, 0.0).astype(BF16)
        hc = _dot(pick, h2_ref[...]).astype(BF16)
        gc3 = _dot(pick, gsplit_ref[...])
        gc = gc3[:, :LANES] + gc3[:, LANES:2 * LANES] + gc3[:, 2 * LANES:]
        yb = jnp.zeros((rows, o_ref.shape[-1]), F32)
        for e in range(EXP_PER_GROUP):
            col = jnp.sum(jnp.where(lane == EXPERT_LANE0 + g * EXP_PER_GROUP + e, gc, 0.0), axis=-1, keepdims=True)
            hg = _dot(hc, wg_ref[e])
            hu = _dot(hc, wu_ref[e])
            he = (hg * _sigmoid(hg)) * hu * col
            yb = yb + _dot(he.astype(BF16), wd_ref[e])
        o_ref[...] += lax.dot_general(pick, yb.astype(BF16), (((0,), (0,)), ((), ())),
                                      preferred_element_type=F32)
        return carry

    lax.fori_loop(0, (count + rows - 1) // rows, block, 0)


def _moe_group(h2, gates, ew, tm, rows):
    n, d = h2.shape
    wg, wu, wd = ew
    tri = (jnp.arange(tm)[:, None] < jnp.arange(tm)[None, :]).astype(BF16)
    row = lambda w: pl.BlockSpec((tm, w), lambda i, g: (i, 0))
    grp = lambda a, b: pl.BlockSpec((EXP_PER_GROUP, a, b), lambda i, g: (g, 0, 0))
    return pl.pallas_call(
        functools.partial(_moe_group_kernel, tm=tm, rows=rows), grid=(n // tm, N_GROUPS),
        in_specs=[row(d), row(LANES), _const_spec(tri.shape), grp(d, EXPERT_FF), grp(d, EXPERT_FF),
                  grp(EXPERT_FF, d)],
        out_specs=row(d), out_shape=jax.ShapeDtypeStruct((n, d), F32),
        scratch_shapes=[pltpu.VMEM((16, tm), F32), pltpu.VMEM((tm, 3 * LANES), BF16)],
        compiler_params=pltpu.CompilerParams(dimension_semantics=("arbitrary", "arbitrary"),
                                             vmem_limit_bytes=VMEM_LIMIT),
        name="moe_group",
    )(h2, gates, tri, wg, wu, wd)


def _ple_kernel(x1_ref, y_ref, p_ref, gple_ref, wpg_ref, wpp_ref, o_ref):
    x2 = x1_ref[...] + y_ref[...]
    g = _sigmoid(_dot(_rms(x2, gple_ref[...]).astype(BF16), wpg_ref[...]))
    o_ref[...] = x2 + g * _dot(p_ref[...].astype(BF16), wpp_ref[...])


def _ple(x1, ymoe, p2d, pw, tm):
    n, d = x1.shape
    row = lambda w: pl.BlockSpec((tm, w), lambda i: (i, 0))
    return pl.pallas_call(
        _ple_kernel, grid=(n // tm,),
        in_specs=[row(d), row(d), row(p2d.shape[1])] + [_const_spec(a.shape) for a in pw],
        out_specs=row(d), out_shape=jax.ShapeDtypeStruct((n, d), F32),
        compiler_params=pltpu.CompilerParams(dimension_semantics=("arbitrary",), vmem_limit_bytes=VMEM_LIMIT),
        name="ple",
    )(x1, ymoe, p2d, *pw)


def _tile(n, pref):
    return pref if n % pref == 0 else n


def kernel(x_prompt, x_sample, cache_k, cache_v, cache_logf, state_conv, page_table, p_prompt, p_sample, norm_mix_g, w_in, b_forget, q_norm_g, k_norm_g, w_attn_o, conv_dw_w, conv_dw_b, conv_ln_g, conv_ln_b, w_conv_o, w_out, norm_ffn_g, w_router_group, b_router_group, w_router_expert, b_router_expert, w_exp_gate, w_exp_up, w_exp_down, norm_ple_g, w_ple_gate, w_ple_proj):
    depth = w_in.shape[0]
    assert depth == 1
    li = 0
    batch, seq, d = x_prompt.shape
    bd, dec_seq, _ = x_sample.shape
    assert dec_seq == 1
    n_pages = page_table.shape[1]

    pw = _prep_proj_weights(norm_mix_g[li], w_in[li], b_forget[li], q_norm_g[li], k_norm_g[li])
    conv_w = (conv_dw_w[li], conv_dw_b[li].reshape(1, CONV_CH), conv_ln_g[li].reshape(1, CONV_CH),
              conv_ln_b[li].reshape(1, CONV_CH))
    wr = jnp.zeros((d, LANES), F32)
    wr = wr.at[:, :N_GROUPS].set(w_router_group[li]).at[:, EXPERT_LANE0:EXPERT_LANE0 + N_EXPERTS].set(
        w_router_expert[li]).astype(BF16)
    br = jnp.zeros((1, LANES), F32)
    br = br.at[0, :N_GROUPS].set(b_router_group[li]).at[0, EXPERT_LANE0:EXPERT_LANE0 + N_EXPERTS].set(
        b_router_expert[li])
    mw = (w_attn_o[li].astype(BF16), w_conv_o[li].astype(BF16), w_out[li].astype(BF16),
          norm_ffn_g[li].reshape(1, d), wr, br)
    ew = (w_exp_gate[li].astype(BF16), w_exp_up[li].astype(BF16), w_exp_down[li].astype(BF16))
    plew = (norm_ple_g[li].reshape(1, d), w_ple_gate[li].astype(BF16), w_ple_proj[li].astype(BF16))

    xp = x_prompt.reshape(batch * seq, d)
    tm = _tile(seq, 512)
    qa, ka, va, kt, vt, lf, y, sg, utail = _proj_prompt(xp, pw, conv_w, batch, seq, tm)
    attn = _attn_prompt(qa, ka, va, batch, seq, _tile(seq, 1024), 1024).reshape(batch * seq, ATTN_W)
    x1, h2, gates = _mix(xp, attn, y, sg, mw, tm)
    ymoe = _moe_group(h2, gates, ew, _tile(batch * seq, MOE_TILE), MOE_BLOCK_ROWS)
    yp = _ple(x1, ymoe, p_prompt[li].reshape(batch * seq, -1), plew, tm)
    y_prompt = yp.reshape(batch, seq, d)
    to_rows = lambda t: t.reshape(batch, N_HEADS, HEAD_DIM, seq).transpose(0, 3, 1, 2)[None]
    new_k_prompt = to_rows(kt)
    new_v_prompt = to_rows(vt)
    new_logf_prompt = lf.reshape(1, batch, seq, N_HEADS)
    new_conv_prompt = utail[:, CONV_HALO - (CONV_K - 1):, :].reshape(1, batch, CONV_K - 1, CONV_CH)

    xs = x_sample.reshape(bd, d)
    state_t = state_conv[li].transpose(1, 0, 2)
    qs, ks, vs, lfs, us, ys, sgs = _proj_sample(xs, pw, state_t, conv_w)
    attn_s = _attn_sample(page_table, qs, ks, vs, lfs, cache_k[li], cache_v[li], cache_logf[li],
                          16 if n_pages % 16 == 0 else 8)
    x1s, h2s, gates_s = _mix(xs, attn_s, ys, sgs, mw, bd)
    ysm = _moe_ple(x1s, h2s, gates_s, p_sample[li].reshape(bd, -1), ew, plew, bd, 4)
    y_sample = ysm.reshape(bd, 1, d)
    new_k_sample = ks.reshape(1, bd, 1, N_HEADS, HEAD_DIM)
    new_v_sample = vs.reshape(1, bd, 1, N_HEADS, HEAD_DIM)
    new_logf_sample = lfs.reshape(1, bd, 1, N_HEADS)
    new_conv_sample = jnp.concatenate([state_conv[li][:, 1:, :], us[:, None, :]], axis=1)[None]

    return (y_prompt, y_sample, new_k_prompt, new_v_prompt, new_logf_prompt, new_conv_prompt,
            new_k_sample, new_v_sample, new_logf_sample, new_conv_sample)
```

```python
import functools

import jax
import jax.numpy as jnp
from jax import lax
from jax.experimental import pallas as pl
from jax.experimental.pallas import tpu as pltpu

F32 = jnp.float32
BF16 = jnp.bfloat16

N_HEADS = 8
HEAD_DIM = 64
ATTN_W = N_HEADS * HEAD_DIM
CONV_CH = 512
CONV_K = 31
N_GROUPS = 4
EXP_PER_GROUP = 8
N_EXPERTS = N_GROUPS * EXP_PER_GROUP
EXPERT_FF = 256
PAGE_SIZE = 128
EPS = 1e-6

LANES = 128
SUBLANES = 8
CONV_HALO = 32
VMEM_LIMIT = 56 * 1024 * 1024
EXPERT_LANE0 = 32
LOG2E = 1.4426950408889634
AUG_K0 = 64
AUG_Q0 = 88
MOE_TILE = 1024
MOE_BLOCK_ROWS = 320


def _const_spec(shape):
    nd = len(shape)
    return pl.BlockSpec(shape, lambda *_: (0,) * nd, pipeline_mode=pl.Buffered(1))


def _sigmoid(x):
    return 0.5 * jnp.tanh(0.5 * x) + 0.5


def _log_sigmoid(x):
    return -(jnp.maximum(-x, 0.0) + jnp.log1p(jnp.exp(-jnp.abs(x))))


def _rms(x, g):
    return x * lax.rsqrt(jnp.mean(x * x, axis=-1, keepdims=True) + EPS) * g


def _dot(a, b):
    return jnp.dot(a, b, preferred_element_type=F32)


def _project(x, gmix, wqkv, wf, wglu, wgate, bf, qg, kg, gsum):
    hb = _rms(x, gmix).astype(BF16)
    zqkv = _dot(hb, wqkv)
    zq = zqkv[:, :ATTN_W]
    zk = zqkv[:, ATTN_W:2 * ATTN_W]
    zv = zqkv[:, 2 * ATTN_W:]

    def head_norm(z, g):
        ss = _dot((z * z).astype(BF16), gsum)
        return z * lax.rsqrt(ss * (1.0 / HEAD_DIM) + EPS) * g

    q = head_norm(zq, qg) * (HEAD_DIM ** -0.5 * LOG2E)
    k = head_norm(zk, kg)
    zf = _dot(hb, wf) + bf
    lane = lax.broadcasted_iota(jnp.int32, zf.shape, 1)
    logf = jnp.where(lane < N_HEADS, _log_sigmoid(zf), 0.0)
    zglu = _dot(hb, wglu)
    u = zglu[:, :CONV_CH] * _sigmoid(zglu[:, CONV_CH:])
    sg = _sigmoid(_dot(hb, wgate))
    return q, k, zv, logf, u, sg


def _ln_silu(y, g, b):
    mu = jnp.mean(y, axis=-1, keepdims=True)
    d = y - mu
    var = jnp.mean(d * d, axis=-1, keepdims=True)
    z = d * lax.rsqrt(var + EPS) * g + b
    return z * _sigmoid(z)


def _split3(x):
    a = x.astype(BF16)
    r = x - a.astype(F32)
    b = r.astype(BF16)
    c = (r - b.astype(F32)).astype(BF16)
    return a, b, c


def _proj_prompt_kernel(x_ref, gmix_ref, wqkv_ref, wf_ref, wglu_ref, wgate_ref, bf_ref, qg_ref, kg_ref,
                        gsum_ref, tri_ref, wdw_ref, bdw_ref, lng_ref, lnb_ref,
                        qa_ref, ka_ref, va_ref, kt_ref, vt_ref, lf_ref, y_ref, sg_ref, utail_ref,
                        ext_ref, carry_ref, shift_ref, *, tiles_per_seq, tm):
    i = pl.program_id(0)

    @pl.when(i % tiles_per_seq == 0)
    def _():
        ext_ref[0:CONV_HALO, :] = jnp.zeros((CONV_HALO, CONV_CH), F32)
        carry_ref[...] = jnp.zeros_like(carry_ref)

    q, k, v, logf, u, sg = _project(x_ref[...], gmix_ref[...], wqkv_ref[...], wf_ref[...], wglu_ref[...],
                                    wgate_ref[...], bf_ref[...], qg_ref[...], kg_ref[...], gsum_ref[...])
    sg_ref[...] = sg
    lf_ref[...] = logf[:, :N_HEADS]

    tri = tri_ref[...]
    a, b, c3 = _split3(logf)
    c = _dot(tri, a) + _dot(tri, b) + _dot(tri, c3) + carry_ref[0:1, :]
    carry_ref[0:1, :] = c[tm - 1:tm, :]

    ca, cb, cc = (t.astype(F32) for t in _split3(c * LOG2E))
    tail_k = -(pltpu.roll(ca, AUG_K0, 1) + pltpu.roll(cb, AUG_K0 + 8, 1) + pltpu.roll(cc, AUG_K0 + 16, 1))
    tail_q = pltpu.roll(ca, AUG_Q0, 1) + pltpu.roll(cb, AUG_Q0 + 8, 1) + pltpu.roll(cc, AUG_Q0 + 16, 1)
    lane = lax.broadcasted_iota(jnp.int32, (1, LANES), 1)
    low = lane < HEAD_DIM
    for h in range(N_HEADS):
        blk = slice((h // 2) * LANES, (h // 2 + 1) * LANES)
        qh, kh = q[:, blk], k[:, blk]
        if h % 2:
            qh, kh = pltpu.roll(qh, HEAD_DIM, 1), pltpu.roll(kh, HEAD_DIM, 1)
        pick_k = ((lane == AUG_K0 + h) | (lane == AUG_K0 + 8 + h) | (lane == AUG_K0 + 16 + h)).astype(F32)
        pick_q = ((lane == AUG_Q0 + h) | (lane == AUG_Q0 + 8 + h) | (lane == AUG_Q0 + 16 + h)).astype(F32)
        qa_ref[0, h] = jnp.where(low, qh, tail_q + pick_k).astype(BF16)
        ka_ref[0, h] = jnp.where(low, kh, tail_k + pick_q).astype(BF16)

    kt_ref[0] = k.T
    vt = v.T
    vt_ref[0] = vt
    row = lax.broadcasted_iota(jnp.int32, (LANES - HEAD_DIM, tm), 0)
    ones_row = jnp.where(row == 0, 1.0, 0.0)
    for h in range(N_HEADS):
        va_ref[0, h] = jnp.concatenate([vt[h * HEAD_DIM:(h + 1) * HEAD_DIM], ones_row], axis=0).astype(BF16)

    ext_ref[CONV_HALO:CONV_HALO + tm, :] = u
    off = CONV_HALO - (CONV_K - 1)
    span = tm + CONV_HALO - SUBLANES
    for s in range(1, SUBLANES):
        shift_ref[s - 1] = ext_ref[s:s + span, :]
    rc = min(tm, 128)
    for r0 in range(0, tm, rc):
        for l0 in range(0, CONV_CH, LANES):
            acc = jnp.zeros((rc, LANES), F32)
            for j in range(CONV_K):
                s = (off + j) % SUBLANES
                a0 = off + j - s + r0
                src = ext_ref if s == 0 else shift_ref.at[s - 1]
                acc = acc + src[a0:a0 + rc, l0:l0 + LANES] * wdw_ref[j:j + 1, l0:l0 + LANES]
            ext_ref[CONV_HALO + tm + r0:CONV_HALO + tm + r0 + rc, l0:l0 + LANES] = acc
    yc = ext_ref[CONV_HALO + tm:CONV_HALO + 2 * tm, :] + bdw_ref[...]
    y_ref[...] = _ln_silu(yc, lng_ref[...], lnb_ref[...]).astype(BF16)
    tail = u[tm - CONV_HALO:, :]
    ext_ref[0:CONV_HALO, :] = tail
    utail_ref[0] = tail


def _proj_sample_kernel(x_ref, gmix_ref, wqkv_ref, wf_ref, wglu_ref, wgate_ref, bf_ref, qg_ref, kg_ref,
                        gsum_ref, st_ref, wdw_ref, bdw_ref, lng_ref, lnb_ref,
                        qb_ref, k32_ref, v32_ref, lf_ref, u_ref, y_ref, sg_ref):
    q, k, v, logf, u, sg = _project(x_ref[...], gmix_ref[...], wqkv_ref[...], wf_ref[...], wglu_ref[...],
                                    wgate_ref[...], bf_ref[...], qg_ref[...], kg_ref[...], gsum_ref[...])
    qb_ref[...] = q.astype(BF16)
    k32_ref[...] = k
    v32_ref[...] = v
    sg_ref[...] = sg
    lf_ref[...] = logf[:, :N_HEADS]
    u_ref[...] = u
    acc = u * wdw_ref[CONV_K - 1:CONV_K, :]
    for j in range(CONV_K - 1):
        acc = acc + st_ref[j] * wdw_ref[j:j + 1, :]
    y_ref[...] = _ln_silu(acc + bdw_ref[...], lng_ref[...], lnb_ref[...]).astype(BF16)


def _prep_proj_weights(norm_mix_g, w_in, b_forget, q_norm_g, k_norm_g):
    d = w_in.shape[0]
    o_f = 3 * ATTN_W
    o_glu = o_f + N_HEADS
    o_gate = o_glu + 2 * CONV_CH
    wqkv = w_in[:, :o_f].astype(BF16)
    wf = jnp.pad(w_in[:, o_f:o_glu], ((0, 0), (0, LANES - N_HEADS))).astype(BF16)
    wglu = w_in[:, o_glu:o_gate].astype(BF16)
    wgate = w_in[:, o_gate:].astype(BF16)
    bf = jnp.pad(b_forget, (0, LANES - N_HEADS)).reshape(1, LANES)
    qg = jnp.tile(q_norm_g, N_HEADS).reshape(1, ATTN_W)
    kg = jnp.tile(k_norm_g, N_HEADS).reshape(1, ATTN_W)
    hid = jnp.arange(ATTN_W) // HEAD_DIM
    gsum = (hid[:, None] == hid[None, :]).astype(BF16)
    return (norm_mix_g.reshape(1, d), wqkv, wf, wglu, wgate, bf, qg, kg, gsum)


def _proj_prompt(x2d, pw, conv_w, batch, seq, tm):
    n, d = x2d.shape
    tps = seq // tm
    tri = (jnp.arange(tm)[:, None] >= jnp.arange(tm)[None, :]).astype(BF16)
    wdw, bdw, lng, lnb = conv_w
    row = lambda w: pl.BlockSpec((tm, w), lambda i: (i, 0))
    in_specs = ([row(d)] + [_const_spec(a.shape) for a in pw] + [_const_spec(tri.shape)]
                + [_const_spec(a.shape) for a in conv_w])
    head_rows = jax.ShapeDtypeStruct((batch, N_HEADS, seq, LANES), BF16)
    head_rows_spec = pl.BlockSpec((1, N_HEADS, tm, LANES), lambda i: (i // tps, 0, i % tps, 0))
    seq_minor = jax.ShapeDtypeStruct((batch, ATTN_W, seq), F32)
    seq_minor_spec = pl.BlockSpec((1, ATTN_W, tm), lambda i: (i // tps, 0, i % tps))
    out_shape = (
        head_rows, head_rows, jax.ShapeDtypeStruct((batch, N_HEADS, LANES, seq), BF16), seq_minor, seq_minor,
        jax.ShapeDtypeStruct((n, N_HEADS), F32), jax.ShapeDtypeStruct((n, CONV_CH), BF16),
        jax.ShapeDtypeStruct((n, 2 * d), F32), jax.ShapeDtypeStruct((batch, CONV_HALO, CONV_CH), F32),
    )
    out_specs = (head_rows_spec, head_rows_spec,
                 pl.BlockSpec((1, N_HEADS, LANES, tm), lambda i: (i // tps, 0, 0, i % tps)),
                 seq_minor_spec, seq_minor_spec, row(N_HEADS),
                 row(CONV_CH), row(2 * d), pl.BlockSpec((1, CONV_HALO, CONV_CH), lambda i: (i // tps, 0, 0)))
    return pl.pallas_call(
        functools.partial(_proj_prompt_kernel, tiles_per_seq=tps, tm=tm),
        grid=(n // tm,), in_specs=in_specs, out_specs=out_specs, out_shape=out_shape,
        scratch_shapes=[pltpu.VMEM((CONV_HALO + 2 * tm, CONV_CH), F32), pltpu.VMEM((SUBLANES, LANES), F32),
                        pltpu.VMEM((SUBLANES - 1, tm + CONV_HALO - SUBLANES, CONV_CH), F32)],
        compiler_params=pltpu.CompilerParams(dimension_semantics=("arbitrary",), vmem_limit_bytes=VMEM_LIMIT),
        name="proj_prompt",
    )(x2d, *pw, tri, wdw, bdw, lng, lnb)


def _proj_sample(x2d, pw, state_t, conv_w):
    n, d = x2d.shape
    args = (x2d,) + tuple(pw) + (state_t,) + tuple(conv_w)
    full = lambda a: pl.BlockSpec(a.shape, lambda i, nd=a.ndim: (0,) * nd)
    out_shape = (
        jax.ShapeDtypeStruct((n, ATTN_W), BF16), jax.ShapeDtypeStruct((n, ATTN_W), F32),
        jax.ShapeDtypeStruct((n, ATTN_W), F32), jax.ShapeDtypeStruct((n, N_HEADS), F32),
        jax.ShapeDtypeStruct((n, CONV_CH), F32), jax.ShapeDtypeStruct((n, CONV_CH), BF16),
        jax.ShapeDtypeStruct((n, 2 * d), F32),
    )
    return pl.pallas_call(
        _proj_sample_kernel, grid=(1,), in_specs=[full(a) for a in args],
        out_specs=tuple(full(s) for s in out_shape), out_shape=out_shape,
        compiler_params=pltpu.CompilerParams(dimension_semantics=("arbitrary",), vmem_limit_bytes=VMEM_LIMIT),
        name="proj_sample",
    )(*args)


def _attn_prompt_kernel(q_ref, k_ref, v_ref, o_ref, *, tq, tk):
    i = pl.program_id(2)
    heads = range(2)
    qs = [q_ref[0, h] for h in heads]
    per_q = tq // tk

    def block(j, carry, diag_chunk=None):
        start = pl.multiple_of(j * tk, tk)
        new = []
        for h in heads:
            m, acc = carry[h]
            ks = k_ref[0, h, pl.ds(start, tk), :]
            st = lax.dot_general(ks, qs[h], (((1,), (1,)), ((), ())), preferred_element_type=F32)
            if diag_chunk is not None:
                key = lax.broadcasted_iota(jnp.int32, (tk, tq), 0) + diag_chunk * tk
                qry = lax.broadcasted_iota(jnp.int32, (tk, tq), 1)
                st = jnp.where(key <= qry, st, -jnp.inf)
            m_new = jnp.maximum(m, jnp.max(st, axis=0, keepdims=True))
            alpha = jnp.exp2(m - m_new)
            pt = jnp.exp2(st - m_new).astype(BF16)
            vs = v_ref[0, h, :, pl.ds(start, tk)]
            new.append((m_new, alpha * acc + _dot(vs, pt)))
        return tuple(new)

    init = tuple((jnp.full((1, tq), -jnp.inf, F32), jnp.zeros((LANES, tq), F32)) for _ in heads)
    carry = lax.fori_loop(0, i * per_q, block, init)
    for c in range(per_q):
        carry = block(i * per_q + c, carry, diag_chunk=c)
    outs = []
    for h in heads:
        acc = carry[h][1]
        outs.append(acc[:HEAD_DIM] / acc[HEAD_DIM:HEAD_DIM + 1])
    o_ref[0] = jnp.concatenate(outs, axis=0).T.astype(BF16)


def _attn_prompt(qa, ka, va, batch, seq, tq, tk):
    hp = N_HEADS // 2
    return pl.pallas_call(
        functools.partial(_attn_prompt_kernel, tq=tq, tk=tk),
        grid=(batch, hp, seq // tq),
        in_specs=[
            pl.BlockSpec((1, 2, tq, LANES), lambda b, h, i: (b, h, i, 0)),
            pl.BlockSpec((1, 2, seq, LANES), lambda b, h, i: (b, h, 0, 0)),
            pl.BlockSpec((1, 2, LANES, seq), lambda b, h, i: (b, h, 0, 0)),
        ],
        out_specs=pl.BlockSpec((1, tq, LANES), lambda b, h, i: (b, i, h)),
        out_shape=jax.ShapeDtypeStruct((batch, seq, ATTN_W), BF16),
        compiler_params=pltpu.CompilerParams(dimension_semantics=("arbitrary", "arbitrary", "arbitrary"),
                                             vmem_limit_bytes=VMEM_LIMIT),
        name="attn_prompt",
    )(qa, ka, va)


def _attn_sample_kernel(pt_ref, q_ref, kn_ref, vn_ref, lfn_ref, *refs, pp):
    k_refs = refs[:pp]
    v_refs = refs[pp:2 * pp]
    f_refs = refs[2 * pp:3 * pp]
    o_ref = refs[3 * pp]
    m_ref, l_ref, r_ref, acc_ref, qrep_ref = refs[3 * pp + 1:]
    j = pl.program_id(1)
    hrow = lax.broadcasted_iota(jnp.int32, (N_HEADS, ATTN_W), 0)
    hlane = lax.broadcasted_iota(jnp.int32, (N_HEADS, ATTN_W), 1) // HEAD_DIM
    diag = hrow == hlane

    @pl.when(j == 0)
    def _():
        qrow = q_ref[0].astype(F32)
        m_ref[...] = jnp.sum(jnp.where(diag, qrow, 0.0) * kn_ref[0], axis=-1, keepdims=True)
        l_ref[...] = jnp.ones_like(l_ref)
        r_ref[...] = lfn_ref[0]
        qrep_ref[...] = jnp.broadcast_to(qrow, (LANES, ATTN_W)).T
        vrep = jnp.broadcast_to(vn_ref[0], (LANES, ATTN_W)).T
        lane0 = lax.broadcasted_iota(jnp.int32, (ATTN_W, LANES), 1) == 0
        acc_ref[...] = jnp.where(lane0, vrep, 0.0)

    srow = lax.broadcasted_iota(jnp.int32, (PAGE_SIZE, 2 * PAGE_SIZE), 0)
    scol = lax.broadcasted_iota(jnp.int32, (PAGE_SIZE, 2 * PAGE_SIZE), 1)
    suffix = jnp.where((srow > scol) | (scol >= PAGE_SIZE), 1.0, 0.0).astype(F32)
    lf_all = jnp.concatenate([f_refs[t][0] for t in range(pp)], axis=0)
    sx = jnp.dot(lf_all, suffix, precision=lax.Precision.HIGHEST, preferred_element_type=F32)

    def head_rows(h):
        return slice(h * HEAD_DIM, (h + 1) * HEAD_DIM)

    r = r_ref[...]
    parts = []
    for t in range(pp):
        bias = r + sx[t * N_HEADS:(t + 1) * N_HEADS, :PAGE_SIZE]
        r = r + sx[t * N_HEADS:(t + 1) * N_HEADS, PAGE_SIZE:PAGE_SIZE + 1]
        rows = []
        for h in range(N_HEADS):
            prod = k_refs[t][0, head_rows(h), :] * qrep_ref[head_rows(h), :]
            rows.append(jnp.sum(prod, axis=0, keepdims=True))
        parts.append(jnp.concatenate(rows, axis=0) + bias * LOG2E)
    r_ref[...] = r
    s = jnp.concatenate(parts, axis=1)
    m = m_ref[...]
    m_new = jnp.maximum(m, jnp.max(s, axis=-1, keepdims=True))
    alpha = jnp.exp2(m - m_new)
    p = jnp.exp2(s - m_new)
    l_ref[...] = alpha * l_ref[...] + jnp.sum(p, axis=-1, keepdims=True)
    m_ref[...] = m_new
    for h in range(N_HEADS):
        pv = jnp.zeros((HEAD_DIM, PAGE_SIZE), F32)
        for t in range(pp):
            pv = pv + v_refs[t][0, head_rows(h), :] * p[h:h + 1, t * PAGE_SIZE:(t + 1) * PAGE_SIZE]
        acc_ref[head_rows(h), :] = alpha[h:h + 1, :] * acc_ref[head_rows(h), :] + pv

    @pl.when(j == pl.num_programs(1) - 1)
    def _():
        o = jnp.sum(acc_ref[...].T, axis=0, keepdims=True)
        inv_l = jnp.sum(jnp.where(diag, 1.0 / l_ref[...], 0.0), axis=0, keepdims=True)
        o_ref[0] = (o * inv_l).astype(BF16)


def _attn_sample(page_table, qb, k_new, v_new, lf_new, cache_k, cache_v, cache_logf, pp):
    bd, n_pages = page_table.shape
    n_pool = cache_k.shape[0]
    kc = cache_k.transpose(0, 2, 3, 1).reshape(n_pool, ATTN_W, PAGE_SIZE)
    vc = cache_v.transpose(0, 2, 3, 1).reshape(n_pool, ATTN_W, PAGE_SIZE)
    fc = cache_logf.transpose(0, 2, 1)
    steps = n_pages // pp

    def page_spec(t, rows):
        return pl.BlockSpec((1, rows, PAGE_SIZE),
                            lambda b, j, pt: (pt[b, n_pages - 1 - (j * pp + t)], 0, 0))

    per_seq = lambda shape: pl.BlockSpec((1,) + shape, lambda b, j, pt: (b, 0, 0))
    in_specs = ([per_seq((1, ATTN_W)), per_seq((1, ATTN_W)), per_seq((1, ATTN_W)), per_seq((N_HEADS, 1))]
                + [page_spec(t, ATTN_W) for t in range(pp)] * 2
                + [page_spec(t, N_HEADS) for t in range(pp)])
    grid_spec = pltpu.PrefetchScalarGridSpec(
        num_scalar_prefetch=1, grid=(bd, steps), in_specs=in_specs,
        out_specs=per_seq((1, ATTN_W)),
        scratch_shapes=[pltpu.VMEM((N_HEADS, 1), F32), pltpu.VMEM((N_HEADS, 1), F32),
                        pltpu.VMEM((N_HEADS, 1), F32), pltpu.VMEM((ATTN_W, LANES), F32),
                        pltpu.VMEM((ATTN_W, LANES), F32)],
    )
    out = pl.pallas_call(
        functools.partial(_attn_sample_kernel, pp=pp), grid_spec=grid_spec,
        out_shape=jax.ShapeDtypeStruct((bd, 1, ATTN_W), BF16),
        compiler_params=pltpu.CompilerParams(dimension_semantics=("arbitrary", "arbitrary"),
                                             vmem_limit_bytes=VMEM_LIMIT),
        name="attn_sample",
    )(page_table, qb.reshape(bd, 1, ATTN_W), k_new.reshape(bd, 1, ATTN_W), v_new.reshape(bd, 1, ATTN_W),
      lf_new.reshape(bd, N_HEADS, 1), *([kc] * pp), *([vc] * pp), *([fc] * pp))
    return out.reshape(bd, ATTN_W)


def _mix_kernel(x_ref, a_ref, y_ref, sg_ref, wao_ref, wco_ref, wout_ref, gffn_ref, wr_ref, br_ref,
                x1_ref, h2_ref, gates_ref):
    d = x_ref.shape[-1]
    ao = _dot(a_ref[...], wao_ref[...])
    co = _dot(y_ref[...], wco_ref[...])
    merged = sg_ref[:, :d] * ao + sg_ref[:, d:] * co
    x1 = x_ref[...] + _dot(merged.astype(BF16), wout_ref[...])
    x1_ref[...] = x1
    h2 = _rms(x1, gffn_ref[...]).astype(BF16)
    h2_ref[...] = h2

    logits = _dot(h2, wr_ref[...]) + br_ref[...]
    lane = lax.broadcasted_iota(jnp.int32, logits.shape, 1)
    neg = -jnp.inf
    big = jnp.int32(4 * LANES)
    gl = jnp.where(lane < N_GROUPS, logits, neg)
    gmax = jnp.max(gl, axis=-1, keepdims=True)
    gval = 1.0 / jnp.sum(jnp.exp(gl - gmax), axis=-1, keepdims=True)
    gidx = jnp.min(jnp.where(gl == gmax, lane, big), axis=-1, keepdims=True)
    ex = lane - EXPERT_LANE0
    in_group = (ex >= gidx * EXP_PER_GROUP) & (ex < (gidx + 1) * EXP_PER_GROUP)
    el = jnp.where(in_group, logits, neg)
    v1 = jnp.max(el, axis=-1, keepdims=True)
    i1 = jnp.min(jnp.where(el == v1, lane, big), axis=-1, keepdims=True)
    el2 = jnp.where(lane == i1, neg, el)
    v2 = jnp.max(el2, axis=-1, keepdims=True)
    i2 = jnp.min(jnp.where(el2 == v2, lane, big), axis=-1, keepdims=True)
    e2 = jnp.exp(v2 - v1)
    w1 = gval / (1.0 + e2)
    w2 = gval * e2 / (1.0 + e2)
    gates_ref[...] = (jnp.where(lane == i1, w1, 0.0) + jnp.where(lane == i2, w2, 0.0)
                      + jnp.where(lane == gidx, 1.0, 0.0))


def _mix(x2d, attn, y, sg, mw, tm):
    n, d = x2d.shape
    row = lambda w: pl.BlockSpec((tm, w), lambda i: (i, 0))
    return pl.pallas_call(
        _mix_kernel, grid=(n // tm,),
        in_specs=[row(d), row(ATTN_W), row(CONV_CH), row(2 * d)] + [_const_spec(a.shape) for a in mw],
        out_specs=(row(d), row(d), row(LANES)),
        out_shape=(jax.ShapeDtypeStruct((n, d), F32), jax.ShapeDtypeStruct((n, d), BF16),
                   jax.ShapeDtypeStruct((n, LANES), F32)),
        compiler_params=pltpu.CompilerParams(dimension_semantics=("arbitrary",), vmem_limit_bytes=VMEM_LIMIT),
        name="mix",
    )(x2d, attn, y, sg, *mw)


def _moe_kernel(x1_ref, h2_ref, gates_ref, p_ref, wg_ref, wu_ref, wd_ref, gple_ref, wpg_ref, wpp_ref,
                o_ref, acc_ref, *, eb):
    jb = pl.program_id(1)

    @pl.when(jb == 0)
    def _():
        acc_ref[...] = jnp.zeros_like(acc_ref)

    h2 = h2_ref[...]
    gates = gates_ref[...]
    lane = lax.broadcasted_iota(jnp.int32, gates.shape, 1)
    for e in range(eb):
        gcol = jnp.sum(jnp.where(lane == EXPERT_LANE0 + jb * eb + e, gates, 0.0), axis=-1, keepdims=True)
        hg = _dot(h2, wg_ref[e])
        hu = _dot(h2, wu_ref[e])
        he = (hg * _sigmoid(hg)) * hu * gcol
        acc_ref[...] += _dot(he.astype(BF16), wd_ref[e])

    @pl.when(jb == pl.num_programs(1) - 1)
    def _():
        x2 = x1_ref[...] + acc_ref[...]
        g = _sigmoid(_dot(_rms(x2, gple_ref[...]).astype(BF16), wpg_ref[...]))
        o_ref[...] = x2 + g * _dot(p_ref[...].astype(BF16), wpp_ref[...])


def _moe_ple(x1, h2, gates, p2d, ew, pw, tm, eb):
    n, d = x1.shape
    wg, wu, wd = ew
    row = lambda w: pl.BlockSpec((tm, w), lambda i, j: (i, 0))
    return pl.pallas_call(
        functools.partial(_moe_kernel, eb=eb), grid=(n // tm, N_EXPERTS // eb),
        in_specs=[row(d), row(d), row(LANES), row(p2d.shape[1]),
                  pl.BlockSpec((eb, d, EXPERT_FF), lambda i, j: (j, 0, 0)),
                  pl.BlockSpec((eb, d, EXPERT_FF), lambda i, j: (j, 0, 0)),
                  pl.BlockSpec((eb, EXPERT_FF, d), lambda i, j: (j, 0, 0))]
                 + [_const_spec(a.shape) for a in pw],
        out_specs=row(d), out_shape=jax.ShapeDtypeStruct((n, d), F32),
        scratch_shapes=[pltpu.VMEM((tm, d), F32)],
        compiler_params=pltpu.CompilerParams(dimension_semantics=("arbitrary", "arbitrary"),
                                             vmem_limit_bytes=VMEM_LIMIT),
        name="moe_ple",
    )(x1, h2, gates, p2d, wg, wu, wd, *pw)


def _moe_group_kernel(h2_ref, gates_ref, tri_ref, wg_ref, wu_ref, wd_ref, o_ref, route_ref, gsplit_ref,
                      *, tm, rows):
    g = pl.program_id(1)

    @pl.when(g == 0)
    def _():
        o_ref[...] = jnp.zeros_like(o_ref)
        gates = gates_ref[...]
        onehot_t = gates.T[0:8]
        route_ref[0:8, :] = onehot_t
        route_ref[8:16, :] = _dot(onehot_t.astype(BF16), tri_ref[...])
        a, b, c = _split3(gates)
        gsplit_ref[...] = jnp.concatenate([a, b, c], axis=1)

    member = route_ref[pl.ds(g, 1), :]
    rank = route_ref[pl.ds(8 + g, 1), :].astype(jnp.int32)
    count = jnp.sum(member).astype(jnp.int32)
    slot = lax.broadcasted_iota(jnp.int32, (rows, tm), 0)
    lane = lax.broadcasted_iota(jnp.int32, (1, LANES), 1)

    def block(b, carry):
        pick = jnp.where(slot == rank - b * rows, member, 0.0).astype(BF16)
        hc = _dot(pick, h2_ref[...]).astype(BF16)
        gc3 = _dot(pick, gsplit_ref[...])
        gc = gc3[:, :LANES] + gc3[:, LANES:2 * LANES] + gc3[:, 2 * LANES:]
        yb = jnp.zeros((rows, o_ref.shape[-1]), F32)
        for e in range(EXP_PER_GROUP):
            col = jnp.sum(jnp.where(lane == EXPERT_LANE0 + g * EXP_PER_GROUP + e, gc, 0.0), axis=-1, keepdims=True)
            hg = _dot(hc, wg_ref[e])
            hu = _dot(hc, wu_ref[e])
            he = (hg * _sigmoid(hg)) * hu * col
            yb = yb + _dot(he.astype(BF16), wd_ref[e])
        o_ref[...] += lax.dot_general(pick, yb.astype(BF16), (((0,), (0,)), ((), ())),
                                      preferred_element_type=F32)
        return carry

    lax.fori_loop(0, (count + rows - 1) // rows, block, 0)


def _moe_group(h2, gates, ew, tm, rows):
    n, d = h2.shape
    wg, wu, wd = ew
    tri = (jnp.arange(tm)[:, None] < jnp.arange(tm)[None, :]).astype(BF16)
    row = lambda w: pl.BlockSpec((tm, w), lambda i, g: (i, 0))
    grp = lambda a, b: pl.BlockSpec((EXP_PER_GROUP, a, b), lambda i, g: (g, 0, 0))
    return pl.pallas_call(
        functools.partial(_moe_group_kernel, tm=tm, rows=rows), grid=(n // tm, N_GROUPS),
        in_specs=[row(d), row(LANES), _const_spec(tri.shape), grp(d, EXPERT_FF), grp(d, EXPERT_FF),
                  grp(EXPERT_FF, d)],
        out_specs=row(d), out_shape=jax.ShapeDtypeStruct((n, d), F32),
        scratch_shapes=[pltpu.VMEM((16, tm), F32), pltpu.VMEM((tm, 3 * LANES), BF16)],
        compiler_params=pltpu.CompilerParams(dimension_semantics=("arbitrary", "arbitrary"),
                                             vmem_limit_bytes=VMEM_LIMIT),
        name="moe_group",
    )(h2, gates, tri, wg, wu, wd)


def _ple_kernel(x1_ref, y_ref, p_ref, gple_ref, wpg_ref, wpp_ref, o_ref):
    x2 = x1_ref[...] + y_ref[...]
    g = _sigmoid(_dot(_rms(x2, gple_ref[...]).astype(BF16), wpg_ref[...]))
    o_ref[...] = x2 + g * _dot(p_ref[...].astype(BF16), wpp_ref[...])


def _ple(x1, ymoe, p2d, pw, tm):
    n, d = x1.shape
    row = lambda w: pl.BlockSpec((tm, w), lambda i: (i, 0))
    return pl.pallas_call(
        _ple_kernel, grid=(n // tm,),
        in_specs=[row(d), row(d), row(p2d.shape[1])] + [_const_spec(a.shape) for a in pw],
        out_specs=row(d), out_shape=jax.ShapeDtypeStruct((n, d), F32),
        compiler_params=pltpu.CompilerParams(dimension_semantics=("arbitrary",), vmem_limit_bytes=VMEM_LIMIT),
        name="ple",
    )(x1, ymoe, p2d, *pw)


def _tile(n, pref):
    return pref if n % pref == 0 else n


def kernel(x_prompt, x_sample, cache_k, cache_v, cache_logf, state_conv, page_table, p_prompt, p_sample, norm_mix_g, w_in, b_forget, q_norm_g, k_norm_g, w_attn_o, conv_dw_w, conv_dw_b, conv_ln_g, conv_ln_b, w_conv_o, w_out, norm_ffn_g, w_router_group, b_router_group, w_router_expert, b_router_expert, w_exp_gate, w_exp_up, w_exp_down, norm_ple_g, w_ple_gate, w_ple_proj):
    depth = w_in.shape[0]
    assert depth == 1
    li = 0
    batch, seq, d = x_prompt.shape
    bd, dec_seq, _ = x_sample.shape
    assert dec_seq == 1
    n_pages = page_table.shape[1]

    pw = _prep_proj_weights(norm_mix_g[li], w_in[li], b_forget[li], q_norm_g[li], k_norm_g[li])
    conv_w = (conv_dw_w[li], conv_dw_b[li].reshape(1, CONV_CH), conv_ln_g[li].reshape(1, CONV_CH),
              conv_ln_b[li].reshape(1, CONV_CH))
    wr = jnp.zeros((d, LANES), F32)
    wr = wr.at[:, :N_GROUPS].set(w_router_group[li]).at[:, EXPERT_LANE0:EXPERT_LANE0 + N_EXPERTS].set(
        w_router_expert[li]).astype(BF16)
    br = jnp.zeros((1, LANES), F32)
    br = br.at[0, :N_GROUPS].set(b_router_group[li]).at[0, EXPERT_LANE0:EXPERT_LANE0 + N_EXPERTS].set(
        b_router_expert[li])
    mw = (w_attn_o[li].astype(BF16), w_conv_o[li].astype(BF16), w_out[li].astype(BF16),
          norm_ffn_g[li].reshape(1, d), wr, br)
    ew = (w_exp_gate[li].astype(BF16), w_exp_up[li].astype(BF16), w_exp_down[li].astype(BF16))
    plew = (norm_ple_g[li].reshape(1, d), w_ple_gate[li].astype(BF16), w_ple_proj[li].astype(BF16))

    xp = x_prompt.reshape(batch * seq, d)
    tm = _tile(seq, 512)
    qa, ka, va, kt, vt, lf, y, sg, utail = _proj_prompt(xp, pw, conv_w, batch, seq, tm)
    attn = _attn_prompt(qa, ka, va, batch, seq, _tile(seq, 1024), 1024).reshape(batch * seq, ATTN_W)
    x1, h2, gates = _mix(xp, attn, y, sg, mw, tm)
    ymoe = _moe_group(h2, gates, ew, _tile(batch * seq, MOE_TILE), MOE_BLOCK_ROWS)
    yp = _ple(x1, ymoe, p_prompt[li].reshape(batch * seq, -1), plew, tm)
    y_prompt = yp.reshape(batch, seq, d)
    to_rows = lambda t: t.reshape(batch, N_HEADS, HEAD_DIM, seq).transpose(0, 3, 1, 2)[None]
    new_k_prompt = to_rows(kt)
    new_v_prompt = to_rows(vt)
    new_logf_prompt = lf.reshape(1, batch, seq, N_HEADS)
    new_conv_prompt = utail[:, CONV_HALO - (CONV_K - 1):, :].reshape(1, batch, CONV_K - 1, CONV_CH)

    xs = x_sample.reshape(bd, d)
    state_t = state_conv[li].transpose(1, 0, 2)
    qs, ks, vs, lfs, us, ys, sgs = _proj_sample(xs, pw, state_t, conv_w)
    attn_s = _attn_sample(page_table, qs, ks, vs, lfs, cache_k[li], cache_v[li], cache_logf[li],
                          16 if n_pages % 16 == 0 else 8)
    x1s, h2s, gates_s = _mix(xs, attn_s, ys, sgs, mw, bd)
    ysm = _moe_ple(x1s, h2s, gates_s, p_sample[li].reshape(bd, -1), ew, plew, bd, 4)
    y_sample = ysm.reshape(bd, 1, d)
    new_k_sample = ks.reshape(1, bd, 1, N_HEADS, HEAD_DIM)
    new_v_sample = vs.reshape(1, bd, 1, N_HEADS, HEAD_DIM)
    new_logf_sample = lfs.reshape(1, bd, 1, N_HEADS)
    new_conv_sample = jnp.concatenate([state_conv[li][:, 1:, :], us[:, None, :]], axis=1)[None]

    return (y_prompt, y_sample, new_k_prompt, new_v_prompt, new_logf_prompt, new_conv_prompt,
            new_k_sample, new_v_sample, new_logf_sample, new_conv_sample)
```

```python
import functools

import jax
import jax.numpy as jnp
from jax import lax
from jax.experimental import pallas as pl
from jax.experimental.pallas import tpu as pltpu

F32 = jnp.float32
BF16 = jnp.bfloat16

N_HEADS = 8
HEAD_DIM = 64
ATTN_W = N_HEADS * HEAD_DIM
CONV_CH = 512
CONV_K = 31
N_GROUPS = 4
EXP_PER_GROUP = 8
N_EXPERTS = N_GROUPS * EXP_PER_GROUP
EXPERT_FF = 256
PAGE_SIZE = 128
EPS = 1e-6

LANES = 128
SUBLANES = 8
CONV_HALO = 32
VMEM_LIMIT = 56 * 1024 * 1024
EXPERT_LANE0 = 32
LOG2E = 1.4426950408889634
AUG_K0 = 64
AUG_Q0 = 88
MOE_TILE = 1024
MOE_BLOCK_ROWS = 288


def _const_spec(shape):
    nd = len(shape)
    return pl.BlockSpec(shape, lambda *_: (0,) * nd, pipeline_mode=pl.Buffered(1))


def _sigmoid(x):
    return 0.5 * jnp.tanh(0.5 * x) + 0.5


def _log_sigmoid(x):
    return -(jnp.maximum(-x, 0.0) + jnp.log1p(jnp.exp(-jnp.abs(x))))


def _rms(x, g):
    return x * lax.rsqrt(jnp.mean(x * x, axis=-1, keepdims=True) + EPS) * g


def _dot(a, b):
    return jnp.dot(a, b, preferred_element_type=F32)


def _project(x, gmix, wqkv, wf, wglu, wgate, bf, qg, kg, gsum):
    hb = _rms(x, gmix).astype(BF16)
    zqkv = _dot(hb, wqkv)
    zq = zqkv[:, :ATTN_W]
    zk = zqkv[:, ATTN_W:2 * ATTN_W]
    zv = zqkv[:, 2 * ATTN_W:]

    def head_norm(z, g):
        ss = _dot((z * z).astype(BF16), gsum)
        return z * lax.rsqrt(ss * (1.0 / HEAD_DIM) + EPS) * g

    q = head_norm(zq, qg) * (HEAD_DIM ** -0.5 * LOG2E)
    k = head_norm(zk, kg)
    zf = _dot(hb, wf) + bf
    lane = lax.broadcasted_iota(jnp.int32, zf.shape, 1)
    logf = jnp.where(lane < N_HEADS, _log_sigmoid(zf), 0.0)
    zglu = _dot(hb, wglu)
    u = zglu[:, :CONV_CH] * _sigmoid(zglu[:, CONV_CH:])
    sg = _sigmoid(_dot(hb, wgate))
    return q, k, zv, logf, u, sg


def _ln_silu(y, g, b):
    mu = jnp.mean(y, axis=-1, keepdims=True)
    d = y - mu
    var = jnp.mean(d * d, axis=-1, keepdims=True)
    z = d * lax.rsqrt(var + EPS) * g + b
    return z * _sigmoid(z)


def _split3(x):
    a = x.astype(BF16)
    r = x - a.astype(F32)
    b = r.astype(BF16)
    c = (r - b.astype(F32)).astype(BF16)
    return a, b, c


def _proj_prompt_kernel(x_ref, gmix_ref, wqkv_ref, wf_ref, wglu_ref, wgate_ref, bf_ref, qg_ref, kg_ref,
                        gsum_ref, tri_ref, wdw_ref, bdw_ref, lng_ref, lnb_ref,
                        qa_ref, ka_ref, va_ref, kt_ref, vt_ref, lf_ref, y_ref, sg_ref, utail_ref,
                        ext_ref, carry_ref, shift_ref, *, tiles_per_seq, tm):
    i = pl.program_id(0)

    @pl.when(i % tiles_per_seq == 0)
    def _():
        ext_ref[0:CONV_HALO, :] = jnp.zeros((CONV_HALO, CONV_CH), F32)
        carry_ref[...] = jnp.zeros_like(carry_ref)

    q, k, v, logf, u, sg = _project(x_ref[...], gmix_ref[...], wqkv_ref[...], wf_ref[...], wglu_ref[...],
                                    wgate_ref[...], bf_ref[...], qg_ref[...], kg_ref[...], gsum_ref[...])
    sg_ref[...] = sg
    lf_ref[...] = logf[:, :N_HEADS]

    tri = tri_ref[...]
    a, b, c3 = _split3(logf)
    c = _dot(tri, a) + _dot(tri, b) + _dot(tri, c3) + carry_ref[0:1, :]
    carry_ref[0:1, :] = c[tm - 1:tm, :]

    ca, cb, cc = (t.astype(F32) for t in _split3(c * LOG2E))
    tail_k = -(pltpu.roll(ca, AUG_K0, 1) + pltpu.roll(cb, AUG_K0 + 8, 1) + pltpu.roll(cc, AUG_K0 + 16, 1))
    tail_q = pltpu.roll(ca, AUG_Q0, 1) + pltpu.roll(cb, AUG_Q0 + 8, 1) + pltpu.roll(cc, AUG_Q0 + 16, 1)
    lane = lax.broadcasted_iota(jnp.int32, (1, LANES), 1)
    low = lane < HEAD_DIM
    for h in range(N_HEADS):
        blk = slice((h // 2) * LANES, (h // 2 + 1) * LANES)
        qh, kh = q[:, blk], k[:, blk]
        if h % 2:
            qh, kh = pltpu.roll(qh, HEAD_DIM, 1), pltpu.roll(kh, HEAD_DIM, 1)
        pick_k = ((lane == AUG_K0 + h) | (lane == AUG_K0 + 8 + h) | (lane == AUG_K0 + 16 + h)).astype(F32)
        pick_q = ((lane == AUG_Q0 + h) | (lane == AUG_Q0 + 8 + h) | (lane == AUG_Q0 + 16 + h)).astype(F32)
        qa_ref[0, h] = jnp.where(low, qh, tail_q + pick_k).astype(BF16)
        ka_ref[0, h] = jnp.where(low, kh, tail_k + pick_q).astype(BF16)

    kt_ref[0] = k.T
    vt = v.T
    vt_ref[0] = vt
    row = lax.broadcasted_iota(jnp.int32, (LANES - HEAD_DIM, tm), 0)
    ones_row = jnp.where(row == 0, 1.0, 0.0)
    for h in range(N_HEADS):
        va_ref[0, h] = jnp.concatenate([vt[h * HEAD_DIM:(h + 1) * HEAD_DIM], ones_row], axis=0).astype(BF16)

    ext_ref[CONV_HALO:CONV_HALO + tm, :] = u
    off = CONV_HALO - (CONV_K - 1)
    span = tm + CONV_HALO - SUBLANES
    for s in range(1, SUBLANES):
        shift_ref[s - 1] = ext_ref[s:s + span, :]
    rc = min(tm, 128)
    for r0 in range(0, tm, rc):
        for l0 in range(0, CONV_CH, LANES):
            acc = jnp.zeros((rc, LANES), F32)
            for j in range(CONV_K):
                s = (off + j) % SUBLANES
                a0 = off + j - s + r0
                src = ext_ref if s == 0 else shift_ref.at[s - 1]
                acc = acc + src[a0:a0 + rc, l0:l0 + LANES] * wdw_ref[j:j + 1, l0:l0 + LANES]
            ext_ref[CONV_HALO + tm + r0:CONV_HALO + tm + r0 + rc, l0:l0 + LANES] = acc
    yc = ext_ref[CONV_HALO + tm:CONV_HALO + 2 * tm, :] + bdw_ref[...]
    y_ref[...] = _ln_silu(yc, lng_ref[...], lnb_ref[...]).astype(BF16)
    tail = u[tm - CONV_HALO:, :]
    ext_ref[0:CONV_HALO, :] = tail
    utail_ref[0] = tail


def _proj_sample_kernel(x_ref, gmix_ref, wqkv_ref, wf_ref, wglu_ref, wgate_ref, bf_ref, qg_ref, kg_ref,
                        gsum_ref, st_ref, wdw_ref, bdw_ref, lng_ref, lnb_ref,
                        qb_ref, k32_ref, v32_ref, lf_ref, u_ref, y_ref, sg_ref):
    q, k, v, logf, u, sg = _project(x_ref[...], gmix_ref[...], wqkv_ref[...], wf_ref[...], wglu_ref[...],
                                    wgate_ref[...], bf_ref[...], qg_ref[...], kg_ref[...], gsum_ref[...])
    qb_ref[...] = q.astype(BF16)
    k32_ref[...] = k
    v32_ref[...] = v
    sg_ref[...] = sg
    lf_ref[...] = logf[:, :N_HEADS]
    u_ref[...] = u
    acc = u * wdw_ref[CONV_K - 1:CONV_K, :]
    for j in range(CONV_K - 1):
        acc = acc + st_ref[j] * wdw_ref[j:j + 1, :]
    y_ref[...] = _ln_silu(acc + bdw_ref[...], lng_ref[...], lnb_ref[...]).astype(BF16)


def _prep_proj_weights(norm_mix_g, w_in, b_forget, q_norm_g, k_norm_g):
    d = w_in.shape[0]
    o_f = 3 * ATTN_W
    o_glu = o_f + N_HEADS
    o_gate = o_glu + 2 * CONV_CH
    wqkv = w_in[:, :o_f].astype(BF16)
    wf = jnp.pad(w_in[:, o_f:o_glu], ((0, 0), (0, LANES - N_HEADS))).astype(BF16)
    wglu = w_in[:, o_glu:o_gate].astype(BF16)
    wgate = w_in[:, o_gate:].astype(BF16)
    bf = jnp.pad(b_forget, (0, LANES - N_HEADS)).reshape(1, LANES)
    qg = jnp.tile(q_norm_g, N_HEADS).reshape(1, ATTN_W)
    kg = jnp.tile(k_norm_g, N_HEADS).reshape(1, ATTN_W)
    hid = jnp.arange(ATTN_W) // HEAD_DIM
    gsum = (hid[:, None] == hid[None, :]).astype(BF16)
    return (norm_mix_g.reshape(1, d), wqkv, wf, wglu, wgate, bf, qg, kg, gsum)


def _proj_prompt(x2d, pw, conv_w, batch, seq, tm):
    n, d = x2d.shape
    tps = seq // tm
    tri = (jnp.arange(tm)[:, None] >= jnp.arange(tm)[None, :]).astype(BF16)
    wdw, bdw, lng, lnb = conv_w
    row = lambda w: pl.BlockSpec((tm, w), lambda i: (i, 0))
    in_specs = ([row(d)] + [_const_spec(a.shape) for a in pw] + [_const_spec(tri.shape)]
                + [_const_spec(a.shape) for a in conv_w])
    head_rows = jax.ShapeDtypeStruct((batch, N_HEADS, seq, LANES), BF16)
    head_rows_spec = pl.BlockSpec((1, N_HEADS, tm, LANES), lambda i: (i // tps, 0, i % tps, 0))
    seq_minor = jax.ShapeDtypeStruct((batch, ATTN_W, seq), F32)
    seq_minor_spec = pl.BlockSpec((1, ATTN_W, tm), lambda i: (i // tps, 0, i % tps))
    out_shape = (
        head_rows, head_rows, jax.ShapeDtypeStruct((batch, N_HEADS, LANES, seq), BF16), seq_minor, seq_minor,
        jax.ShapeDtypeStruct((n, N_HEADS), F32), jax.ShapeDtypeStruct((n, CONV_CH), BF16),
        jax.ShapeDtypeStruct((n, 2 * d), F32), jax.ShapeDtypeStruct((batch, CONV_HALO, CONV_CH), F32),
    )
    out_specs = (head_rows_spec, head_rows_spec,
                 pl.BlockSpec((1, N_HEADS, LANES, tm), lambda i: (i // tps, 0, 0, i % tps)),
                 seq_minor_spec, seq_minor_spec, row(N_HEADS),
                 row(CONV_CH), row(2 * d), pl.BlockSpec((1, CONV_HALO, CONV_CH), lambda i: (i // tps, 0, 0)))
    return pl.pallas_call(
        functools.partial(_proj_prompt_kernel, tiles_per_seq=tps, tm=tm),
        grid=(n // tm,), in_specs=in_specs, out_specs=out_specs, out_shape=out_shape,
        scratch_shapes=[pltpu.VMEM((CONV_HALO + 2 * tm, CONV_CH), F32), pltpu.VMEM((SUBLANES, LANES), F32),
                        pltpu.VMEM((SUBLANES - 1, tm + CONV_HALO - SUBLANES, CONV_CH), F32)],
        compiler_params=pltpu.CompilerParams(dimension_semantics=("arbitrary",), vmem_limit_bytes=VMEM_LIMIT),
        name="proj_prompt",
    )(x2d, *pw, tri, wdw, bdw, lng, lnb)


def _proj_sample(x2d, pw, state_t, conv_w):
    n, d = x2d.shape
    args = (x2d,) + tuple(pw) + (state_t,) + tuple(conv_w)
    full = lambda a: pl.BlockSpec(a.shape, lambda i, nd=a.ndim: (0,) * nd)
    out_shape = (
        jax.ShapeDtypeStruct((n, ATTN_W), BF16), jax.ShapeDtypeStruct((n, ATTN_W), F32),
        jax.ShapeDtypeStruct((n, ATTN_W), F32), jax.ShapeDtypeStruct((n, N_HEADS), F32),
        jax.ShapeDtypeStruct((n, CONV_CH), F32), jax.ShapeDtypeStruct((n, CONV_CH), BF16),
        jax.ShapeDtypeStruct((n, 2 * d), F32),
    )
    return pl.pallas_call(
        _proj_sample_kernel, grid=(1,), in_specs=[full(a) for a in args],
        out_specs=tuple(full(s) for s in out_shape), out_shape=out_shape,
        compiler_params=pltpu.CompilerParams(dimension_semantics=("arbitrary",), vmem_limit_bytes=VMEM_LIMIT),
        name="proj_sample",
    )(*args)


def _attn_fused_kernel(pt_ref, q_ref, k_ref, v_ref, qs_ref, kn_ref, vn_ref, lfn_ref, kc_hbm, vc_hbm, fc_hbm,
                       o_ref, os_ref, kbuf, vbuf, fbuf, sems, m_ref, l_ref, r_ref, acc_ref, qrep_ref, cnt_ref,
                       *, tq, pp, chunks_per_seq, n_pages, n_chunks):
    b_id, h_id, i = pl.program_id(0), pl.program_id(1), pl.program_id(2)
    heads = range(2)
    qs = [q_ref[0, h] for h in heads]

    def page_copies(n, slot):
        seq = n // chunks_per_seq
        c = n % chunks_per_seq
        copies = []
        for t in range(pp):
            idx = n_pages - 1 - jnp.minimum(c * pp + t, n_pages - 1)
            page = pt_ref[seq, idx]
            copies.append(pltpu.make_async_copy(kc_hbm.at[page], kbuf.at[slot, t], sems.at[slot, 0]))
            copies.append(pltpu.make_async_copy(vc_hbm.at[page], vbuf.at[slot, t], sems.at[slot, 1]))
            copies.append(pltpu.make_async_copy(fc_hbm.at[page], fbuf.at[slot, t], sems.at[slot, 2]))
        return copies

    @pl.when((b_id == 0) & (h_id == 0) & (i == 0))
    def _():
        cnt_ref[0] = 0
        for cp in page_copies(0, 0):
            cp.start()

    hrow = lax.broadcasted_iota(jnp.int32, (N_HEADS, ATTN_W), 0)
    hlane = lax.broadcasted_iota(jnp.int32, (N_HEADS, ATTN_W), 1) // HEAD_DIM
    diag = hrow == hlane
    srow = lax.broadcasted_iota(jnp.int32, (PAGE_SIZE, 2 * PAGE_SIZE), 0)
    scol = lax.broadcasted_iota(jnp.int32, (PAGE_SIZE, 2 * PAGE_SIZE), 1)
    suffix = jnp.where((srow > scol) | (scol >= PAGE_SIZE), 1.0, 0.0).astype(F32)

    def head_rows(h):
        return slice(h * HEAD_DIM, (h + 1) * HEAD_DIM)

    def decode_enter():
        n = cnt_ref[0]
        slot = n % 2
        seq = n // chunks_per_seq
        for cp in page_copies(n, slot):
            cp.wait()

        @pl.when(n + 1 < n_chunks)
        def _():
            for cp in page_copies(n + 1, 1 - slot):
                cp.start()

        @pl.when(n % chunks_per_seq == 0)
        def _():
            qrow = qs_ref[seq].astype(F32)
            m_ref[...] = jnp.sum(jnp.where(diag, qrow, 0.0) * kn_ref[seq], axis=-1, keepdims=True)
            l_ref[...] = jnp.ones_like(l_ref)
            r_ref[...] = lfn_ref[seq]
            qrep_ref[...] = jnp.broadcast_to(qrow, (LANES, ATTN_W)).T
            vrep = jnp.broadcast_to(vn_ref[seq], (LANES, ATTN_W)).T
            lane0 = lax.broadcasted_iota(jnp.int32, (ATTN_W, LANES), 1) == 0
            acc_ref[...] = jnp.where(lane0, vrep, 0.0)

    def decode_chunk():
        n = cnt_ref[0]
        slot = n % 2
        c = n % chunks_per_seq
        lf_all = jnp.concatenate([fbuf[slot, t] for t in range(pp)], axis=0)
        sx = jnp.dot(lf_all, suffix, precision=lax.Precision.HIGHEST, preferred_element_type=F32)
        r = r_ref[...]
        parts = []
        for t in range(pp):
            valid = c * pp + t < n_pages
            bias = r + sx[t * N_HEADS:(t + 1) * N_HEADS, :PAGE_SIZE]
            r = r + jnp.where(valid, sx[t * N_HEADS:(t + 1) * N_HEADS, PAGE_SIZE:PAGE_SIZE + 1], 0.0)
            rows = []
            for h in range(N_HEADS):
                prod = kbuf[slot, t, head_rows(h), :] * qrep_ref[head_rows(h), :]
                rows.append(jnp.sum(prod, axis=0, keepdims=True))
            parts.append(jnp.where(valid, jnp.concatenate(rows, axis=0) + bias * LOG2E, -jnp.inf))
        r_ref[...] = r
        s = jnp.concatenate(parts, axis=1)
        m = m_ref[...]
        m_new = jnp.maximum(m, jnp.max(s, axis=-1, keepdims=True))
        alpha = jnp.exp2(m - m_new)
        p = jnp.exp2(s - m_new)
        l_ref[...] = alpha * l_ref[...] + jnp.sum(p, axis=-1, keepdims=True)
        m_ref[...] = m_new
        for h in range(N_HEADS):
            pv = jnp.zeros((HEAD_DIM, PAGE_SIZE), F32)
            for t in range(pp):
                pv = pv + vbuf[slot, t, head_rows(h), :] * p[h:h + 1, t * PAGE_SIZE:(t + 1) * PAGE_SIZE]
            acc_ref[head_rows(h), :] = alpha[h:h + 1, :] * acc_ref[head_rows(h), :] + pv

    def decode_leave():
        n = cnt_ref[0]

        @pl.when(n % chunks_per_seq == chunks_per_seq - 1)
        def _():
            o = jnp.sum(acc_ref[...].T, axis=0, keepdims=True)
            inv_l = jnp.sum(jnp.where(diag, 1.0 / l_ref[...], 0.0), axis=0, keepdims=True)
            os_ref[n // chunks_per_seq] = (o * inv_l).astype(BF16)

        cnt_ref[0] = n + 1

    def block(j, carry, masked):
        start = pl.multiple_of(j * tq, tq)
        decode_enter()
        decode_chunk()
        new = []
        for h in heads:
            m, acc = carry[h]
            ks = k_ref[0, h, pl.ds(start, tq), :]
            st = lax.dot_general(ks, qs[h], (((1,), (1,)), ((), ())), preferred_element_type=F32)
            if masked:
                key = lax.broadcasted_iota(jnp.int32, (tq, tq), 0)
                qry = lax.broadcasted_iota(jnp.int32, (tq, tq), 1)
                st = jnp.where(key <= qry, st, -jnp.inf)
            m_new = jnp.maximum(m, jnp.max(st, axis=0, keepdims=True))
            alpha = jnp.exp2(m - m_new)
            pt = jnp.exp2(st - m_new).astype(BF16)
            vs = v_ref[0, h, :, pl.ds(start, tq)]
            new.append((m_new, alpha * acc + _dot(vs, pt)))
        decode_leave()
        return tuple(new)

    init = tuple((jnp.full((1, tq), -jnp.inf, F32), jnp.zeros((LANES, tq), F32)) for _ in heads)
    carry = lax.fori_loop(0, i, lambda j, c: block(j, c, False), init)
    carry = block(i, carry, True)
    outs = []
    for h in heads:
        acc = carry[h][1]
        outs.append(acc[:HEAD_DIM] / acc[HEAD_DIM:HEAD_DIM + 1])
    o_ref[0] = jnp.concatenate(outs, axis=0).T.astype(BF16)


def _attn_fused(qa, ka, va, batch, seq, tq, page_table, qb, k_new, v_new, lf_new, cache_k, cache_v, cache_logf):
    hp = N_HEADS // 2
    nq = seq // tq
    bd, n_pages = page_table.shape
    n_chunks = batch * hp * nq * (nq + 1) // 2
    assert n_chunks % bd == 0
    chunks_per_seq = n_chunks // bd
    pp = -(-n_pages // chunks_per_seq)
    n_pool = cache_k.shape[0]
    kc = cache_k.transpose(0, 2, 3, 1).reshape(n_pool, ATTN_W, PAGE_SIZE)
    vc = cache_v.transpose(0, 2, 3, 1).reshape(n_pool, ATTN_W, PAGE_SIZE)
    fc = cache_logf.transpose(0, 2, 1)
    whole = lambda shape: pl.BlockSpec(shape, lambda b, h, i, pt: (0,) * len(shape))
    hbm = pl.BlockSpec(memory_space=pl.ANY)
    grid_spec = pltpu.PrefetchScalarGridSpec(
        num_scalar_prefetch=1, grid=(batch, hp, nq),
        in_specs=[
            pl.BlockSpec((1, 2, tq, LANES), lambda b, h, i, pt: (b, h, i, 0)),
            pl.BlockSpec((1, 2, seq, LANES), lambda b, h, i, pt: (b, h, 0, 0)),
            pl.BlockSpec((1, 2, LANES, seq), lambda b, h, i, pt: (b, h, 0, 0)),
            whole((bd, 1, ATTN_W)), whole((bd, 1, ATTN_W)), whole((bd, 1, ATTN_W)), whole((bd, N_HEADS, 1)),
            hbm, hbm, hbm,
        ],
        out_specs=(pl.BlockSpec((1, tq, LANES), lambda b, h, i, pt: (b, i, h)), whole((bd, 1, ATTN_W))),
        scratch_shapes=[
            pltpu.VMEM((2, pp, ATTN_W, PAGE_SIZE), F32), pltpu.VMEM((2, pp, ATTN_W, PAGE_SIZE), F32),
            pltpu.VMEM((2, pp, N_HEADS, PAGE_SIZE), F32), pltpu.SemaphoreType.DMA((2, 3)),
            pltpu.VMEM((N_HEADS, 1), F32), pltpu.VMEM((N_HEADS, 1), F32), pltpu.VMEM((N_HEADS, 1), F32),
            pltpu.VMEM((ATTN_W, LANES), F32), pltpu.VMEM((ATTN_W, LANES), F32), pltpu.SMEM((1,), jnp.int32),
        ],
    )
    attn, attn_s = pl.pallas_call(
        functools.partial(_attn_fused_kernel, tq=tq, pp=pp, chunks_per_seq=chunks_per_seq, n_pages=n_pages,
                          n_chunks=n_chunks),
        grid_spec=grid_spec,
        out_shape=(jax.ShapeDtypeStruct((batch, seq, ATTN_W), BF16), jax.ShapeDtypeStruct((bd, 1, ATTN_W), BF16)),
        compiler_params=pltpu.CompilerParams(dimension_semantics=("arbitrary", "arbitrary", "arbitrary"),
                                             vmem_limit_bytes=VMEM_LIMIT),
        name="attn_fused",
    )(page_table, qa, ka, va, qb.reshape(bd, 1, ATTN_W), k_new.reshape(bd, 1, ATTN_W),
      v_new.reshape(bd, 1, ATTN_W), lf_new.reshape(bd, N_HEADS, 1), kc, vc, fc)
    return attn, attn_s.reshape(bd, ATTN_W)


def _mix_kernel(x_ref, a_ref, y_ref, sg_ref, wao_ref, wco_ref, wout_ref, gffn_ref, wr_ref, br_ref,
                x1_ref, h2_ref, gates_ref):
    d = x_ref.shape[-1]
    ao = _dot(a_ref[...], wao_ref[...])
    co = _dot(y_ref[...], wco_ref[...])
    merged = sg_ref[:, :d] * ao + sg_ref[:, d:] * co
    x1 = x_ref[...] + _dot(merged.astype(BF16), wout_ref[...])
    x1_ref[...] = x1
    h2 = _rms(x1, gffn_ref[...]).astype(BF16)
    h2_ref[...] = h2

    logits = _dot(h2, wr_ref[...]) + br_ref[...]
    lane = lax.broadcasted_iota(jnp.int32, logits.shape, 1)
    neg = -jnp.inf
    big = jnp.int32(4 * LANES)
    gl = jnp.where(lane < N_GROUPS, logits, neg)
    gmax = jnp.max(gl, axis=-1, keepdims=True)
    gval = 1.0 / jnp.sum(jnp.exp(gl - gmax), axis=-1, keepdims=True)
    gidx = jnp.min(jnp.where(gl == gmax, lane, big), axis=-1, keepdims=True)
    ex = lane - EXPERT_LANE0
    in_group = (ex >= gidx * EXP_PER_GROUP) & (ex < (gidx + 1) * EXP_PER_GROUP)
    el = jnp.where(in_group, logits, neg)
    v1 = jnp.max(el, axis=-1, keepdims=True)
    i1 = jnp.min(jnp.where(el == v1, lane, big), axis=-1, keepdims=True)
    el2 = jnp.where(lane == i1, neg, el)
    v2 = jnp.max(el2, axis=-1, keepdims=True)
    i2 = jnp.min(jnp.where(el2 == v2, lane, big), axis=-1, keepdims=True)
    e2 = jnp.exp(v2 - v1)
    w1 = gval / (1.0 + e2)
    w2 = gval * e2 / (1.0 + e2)
    gates_ref[...] = (jnp.where(lane == i1, w1, 0.0) + jnp.where(lane == i2, w2, 0.0)
                      + jnp.where(lane == gidx, 1.0, 0.0))


def _mix(x2d, attn, y, sg, mw, tm):
    n, d = x2d.shape
    row = lambda w: pl.BlockSpec((tm, w), lambda i: (i, 0))
    return pl.pallas_call(
        _mix_kernel, grid=(n // tm,),
        in_specs=[row(d), row(ATTN_W), row(CONV_CH), row(2 * d)] + [_const_spec(a.shape) for a in mw],
        out_specs=(row(d), row(d), row(LANES)),
        out_shape=(jax.ShapeDtypeStruct((n, d), F32), jax.ShapeDtypeStruct((n, d), BF16),
                   jax.ShapeDtypeStruct((n, LANES), F32)),
        compiler_params=pltpu.CompilerParams(dimension_semantics=("arbitrary",), vmem_limit_bytes=VMEM_LIMIT),
        name="mix",
    )(x2d, attn, y, sg, *mw)


def _moe_kernel(x1_ref, h2_ref, gates_ref, p_ref, wg_ref, wu_ref, wd_ref, gple_ref, wpg_ref, wpp_ref,
                o_ref, acc_ref, *, eb):
    jb = pl.program_id(1)

    @pl.when(jb == 0)
    def _():
        acc_ref[...] = jnp.zeros_like(acc_ref)

    h2 = h2_ref[...]
    gates = gates_ref[...]
    lane = lax.broadcasted_iota(jnp.int32, gates.shape, 1)
    for e in range(eb):
        gcol = jnp.sum(jnp.where(lane == EXPERT_LANE0 + jb * eb + e, gates, 0.0), axis=-1, keepdims=True)
        hg = _dot(h2, wg_ref[e])
        hu = _dot(h2, wu_ref[e])
        he = (hg * _sigmoid(hg)) * hu * gcol
        acc_ref[...] += _dot(he.astype(BF16), wd_ref[e])

    @pl.when(jb == pl.num_programs(1) - 1)
    def _():
        x2 = x1_ref[...] + acc_ref[...]
        g = _sigmoid(_dot(_rms(x2, gple_ref[...]).astype(BF16), wpg_ref[...]))
        o_ref[...] = x2 + g * _dot(p_ref[...].astype(BF16), wpp_ref[...])


def _moe_ple(x1, h2, gates, p2d, ew, pw, tm, eb):
    n, d = x1.shape
    wg, wu, wd = ew
    row = lambda w: pl.BlockSpec((tm, w), lambda i, j: (i, 0))
    return pl.pallas_call(
        functools.partial(_moe_kernel, eb=eb), grid=(n // tm, N_EXPERTS // eb),
        in_specs=[row(d), row(d), row(LANES), row(p2d.shape[1]),
                  pl.BlockSpec((eb, d, EXPERT_FF), lambda i, j: (j, 0, 0)),
                  pl.BlockSpec((eb, d, EXPERT_FF), lambda i, j: (j, 0, 0)),
                  pl.BlockSpec((eb, EXPERT_FF, d), lambda i, j: (j, 0, 0))]
                 + [_const_spec(a.shape) for a in pw],
        out_specs=row(d), out_shape=jax.ShapeDtypeStruct((n, d), F32),
        scratch_shapes=[pltpu.VMEM((tm, d), F32)],
        compiler_params=pltpu.CompilerParams(dimension_semantics=("arbitrary", "arbitrary"),
                                             vmem_limit_bytes=VMEM_LIMIT),
        name="moe_ple",
    )(x1, h2, gates, p2d, wg, wu, wd, *pw)


def _moe_group_kernel(h2_ref, gates_ref, tri_ref, wg_ref, wu_ref, wd_ref, o_ref, route_ref, gsplit_ref,
                      *, tm, rows):
    g = pl.program_id(1)

    @pl.when(g == 0)
    def _():
        o_ref[...] = jnp.zeros_like(o_ref)
        gates = gates_ref[...]
        onehot_t = gates.T[0:8]
        route_ref[0:8, :] = onehot_t
        route_ref[8:16, :] = _dot(onehot_t.astype(BF16), tri_ref[...])
        a, b, c = _split3(gates)
        gsplit_ref[...] = jnp.concatenate([a, b, c], axis=1)

    member = route_ref[pl.ds(g, 1), :]
    rank = route_ref[pl.ds(8 + g, 1), :].astype(jnp.int32)
    count = jnp.sum(member).astype(jnp.int32)
    slot = lax.broadcasted_iota(jnp.int32, (rows, tm), 0)
    lane = lax.broadcasted_iota(jnp.int32, (1, LANES), 1)

    def block(b, carry):
        pick = jnp.where(slot == rank - b * rows, member, 0.0).astype(BF16)
        hc = _dot(pick, h2_ref[...]).astype(BF16)
        gc3 = _dot(pick, gsplit_ref[...])
        gc = gc3[:, :LANES] + gc3[:, LANES:2 * LANES] + gc3[:, 2 * LANES:]
        yb = jnp.zeros((rows, o_ref.shape[-1]), F32)
        for e in range(EXP_PER_GROUP):
            col = jnp.sum(jnp.where(lane == EXPERT_LANE0 + g * EXP_PER_GROUP + e, gc, 0.0), axis=-1, keepdims=True)
            hg = _dot(hc, wg_ref[e])
            hu = _dot(hc, wu_ref[e])
            he = (hg * _sigmoid(hg)) * hu * col
            yb = yb + _dot(he.astype(BF16), wd_ref[e])
        o_ref[...] += lax.dot_general(pick, yb.astype(BF16), (((0,), (0,)), ((), ())),
                                      preferred_element_type=F32)
        return carry

    lax.fori_loop(0, (count + rows - 1) // rows, block, 0)


def _moe_group(h2, gates, ew, tm, rows):
    n, d = h2.shape
    wg, wu, wd = ew
    tri = (jnp.arange(tm)[:, None] < jnp.arange(tm)[None, :]).astype(BF16)
    row = lambda w: pl.BlockSpec((tm, w), lambda i, g: (i, 0))
    grp = lambda a, b: pl.BlockSpec((EXP_PER_GROUP, a, b), lambda i, g: (g, 0, 0))
    return pl.pallas_call(
        functools.partial(_moe_group_kernel, tm=tm, rows=rows), grid=(n // tm, N_GROUPS),
        in_specs=[row(d), row(LANES), _const_spec(tri.shape), grp(d, EXPERT_FF), grp(d, EXPERT_FF),
                  grp(EXPERT_FF, d)],
        out_specs=row(d), out_shape=jax.ShapeDtypeStruct((n, d), F32),
        scratch_shapes=[pltpu.VMEM((16, tm), F32), pltpu.VMEM((tm, 3 * LANES), BF16)],
        compiler_params=pltpu.CompilerParams(dimension_semantics=("arbitrary", "arbitrary"),
                                             vmem_limit_bytes=VMEM_LIMIT),
        name="moe_group",
    )(h2, gates, tri, wg, wu, wd)


def _ple_kernel(x1_ref, y_ref, p_ref, gple_ref, wpg_ref, wpp_ref, o_ref):
    x2 = x1_ref[...] + y_ref[...]
    g = _sigmoid(_dot(_rms(x2, gple_ref[...]).astype(BF16), wpg_ref[...]))
    o_ref[...] = x2 + g * _dot(p_ref[...].astype(BF16), wpp_ref[...])


def _ple(x1, ymoe, p2d, pw, tm):
    n, d = x1.shape
    row = lambda w: pl.BlockSpec((tm, w), lambda i: (i, 0))
    return pl.pallas_call(
        _ple_kernel, grid=(n // tm,),
        in_specs=[row(d), row(d), row(p2d.shape[1])] + [_const_spec(a.shape) for a in pw],
        out_specs=row(d), out_shape=jax.ShapeDtypeStruct((n, d), F32),
        compiler_params=pltpu.CompilerParams(dimension_semantics=("arbitrary",), vmem_limit_bytes=VMEM_LIMIT),
        name="ple",
    )(x1, ymoe, p2d, *pw)


def _tile(n, pref):
    return pref if n % pref == 0 else n


def kernel(x_prompt, x_sample, cache_k, cache_v, cache_logf, state_conv, page_table, p_prompt, p_sample, norm_mix_g, w_in, b_forget, q_norm_g, k_norm_g, w_attn_o, conv_dw_w, conv_dw_b, conv_ln_g, conv_ln_b, w_conv_o, w_out, norm_ffn_g, w_router_group, b_router_group, w_router_expert, b_router_expert, w_exp_gate, w_exp_up, w_exp_down, norm_ple_g, w_ple_gate, w_ple_proj):
    depth = w_in.shape[0]
    assert depth == 1
    li = 0
    batch, seq, d = x_prompt.shape
    bd, dec_seq, _ = x_sample.shape
    assert dec_seq == 1
    n_pages = page_table.shape[1]

    pw = _prep_proj_weights(norm_mix_g[li], w_in[li], b_forget[li], q_norm_g[li], k_norm_g[li])
    conv_w = (conv_dw_w[li], conv_dw_b[li].reshape(1, CONV_CH), conv_ln_g[li].reshape(1, CONV_CH),
              conv_ln_b[li].reshape(1, CONV_CH))
    wr = jnp.zeros((d, LANES), F32)
    wr = wr.at[:, :N_GROUPS].set(w_router_group[li]).at[:, EXPERT_LANE0:EXPERT_LANE0 + N_EXPERTS].set(
        w_router_expert[li]).astype(BF16)
    br = jnp.zeros((1, LANES), F32)
    br = br.at[0, :N_GROUPS].set(b_router_group[li]).at[0, EXPERT_LANE0:EXPERT_LANE0 + N_EXPERTS].set(
        b_router_expert[li])
    mw = (w_attn_o[li].astype(BF16), w_conv_o[li].astype(BF16), w_out[li].astype(BF16),
          norm_ffn_g[li].reshape(1, d), wr, br)
    ew = (w_exp_gate[li].astype(BF16), w_exp_up[li].astype(BF16), w_exp_down[li].astype(BF16))
    plew = (norm_ple_g[li].reshape(1, d), w_ple_gate[li].astype(BF16), w_ple_proj[li].astype(BF16))

    xp = x_prompt.reshape(batch * seq, d)
    tm = _tile(seq, 512)
    qa, ka, va, kt, vt, lf, y, sg, utail = _proj_prompt(xp, pw, conv_w, batch, seq, tm)
    xs = x_sample.reshape(bd, d)
    state_t = state_conv[li].transpose(1, 0, 2)
    qs, ks, vs, lfs, us, ys, sgs = _proj_sample(xs, pw, state_t, conv_w)
    attn, attn_s = _attn_fused(qa, ka, va, batch, seq, _tile(seq, 1024), page_table, qs, ks, vs, lfs,
                               cache_k[li], cache_v[li], cache_logf[li])
    attn = attn.reshape(batch * seq, ATTN_W)

    x1, h2, gates = _mix(xp, attn, y, sg, mw, tm)
    ymoe = _moe_group(h2, gates, ew, _tile(batch * seq, MOE_TILE), MOE_BLOCK_ROWS)
    yp = _ple(x1, ymoe, p_prompt[li].reshape(batch * seq, -1), plew, tm)
    y_prompt = yp.reshape(batch, seq, d)
    to_rows = lambda t: t.reshape(batch, N_HEADS, HEAD_DIM, seq).transpose(0, 3, 1, 2)[None]
    new_k_prompt = to_rows(kt)
    new_v_prompt = to_rows(vt)
    new_logf_prompt = lf.reshape(1, batch, seq, N_HEADS)
    new_conv_prompt = utail[:, CONV_HALO - (CONV_K - 1):, :].reshape(1, batch, CONV_K - 1, CONV_CH)

    x1s, h2s, gates_s = _mix(xs, attn_s, ys, sgs, mw, bd)
    ysm = _moe_ple(x1s, h2s, gates_s, p_sample[li].reshape(bd, -1), ew, plew, bd, 4)
    y_sample = ysm.reshape(bd, 1, d)
    new_k_sample = ks.reshape(1, bd, 1, N_HEADS, HEAD_DIM)
    new_v_sample = vs.reshape(1, bd, 1, N_HEADS, HEAD_DIM)
    new_logf_sample = lfs.reshape(1, bd, 1, N_HEADS)
    new_conv_sample = jnp.concatenate([state_conv[li][:, 1:, :], us[:, None, :]], axis=1)[None]

    return (y_prompt, y_sample, new_k_prompt, new_v_prompt, new_logf_prompt, new_conv_prompt,
            new_k_sample, new_v_sample, new_logf_sample, new_conv_sample)
```

```python
import functools

import jax
import jax.numpy as jnp
from jax import lax
from jax.experimental import pallas as pl
from jax.experimental.pallas import tpu as pltpu

F32 = jnp.float32
BF16 = jnp.bfloat16

N_HEADS = 8
HEAD_DIM = 64
ATTN_W = N_HEADS * HEAD_DIM
CONV_CH = 512
CONV_K = 31
N_GROUPS = 4
EXP_PER_GROUP = 8
N_EXPERTS = N_GROUPS * EXP_PER_GROUP
EXPERT_FF = 256
PAGE_SIZE = 128
EPS = 1e-6

LANES = 128
SUBLANES = 8
CONV_HALO = 32
VMEM_LIMIT = 56 * 1024 * 1024
EXPERT_LANE0 = 32
LOG2E = 1.4426950408889634
AUG_K0 = 64
AUG_Q0 = 88
VT_ROWS = HEAD_DIM + 16
MOE_TILE = 1024
MOE_BLOCK_ROWS = 320


def _const_spec(shape):
    nd = len(shape)
    return pl.BlockSpec(shape, lambda *_: (0,) * nd, pipeline_mode=pl.Buffered(1))


def _sigmoid(x):
    return 0.5 * jnp.tanh(0.5 * x) + 0.5


def _log_sigmoid(x):
    return -(jnp.maximum(-x, 0.0) + jnp.log1p(jnp.exp(-jnp.abs(x))))


def _rms(x, g):
    return x * lax.rsqrt(jnp.mean(x * x, axis=-1, keepdims=True) + EPS) * g


def _dot(a, b):
    return jnp.dot(a, b, preferred_element_type=F32)


def _project(x, gmix, wqkv, wf, wglu, wgate, bf, qg, kg, gsum):
    hb = _rms(x, gmix).astype(BF16)
    zqkv = _dot(hb, wqkv)
    zq = zqkv[:, :ATTN_W]
    zk = zqkv[:, ATTN_W:2 * ATTN_W]
    zv = zqkv[:, 2 * ATTN_W:]

    def head_norm(z, g):
        ss = _dot((z * z).astype(BF16), gsum)
        return z * lax.rsqrt(ss * (1.0 / HEAD_DIM) + EPS) * g

    q = head_norm(zq, qg) * (HEAD_DIM ** -0.5 * LOG2E)
    k = head_norm(zk, kg)
    zf = _dot(hb, wf) + bf
    lane = lax.broadcasted_iota(jnp.int32, zf.shape, 1)
    logf = jnp.where(lane < N_HEADS, _log_sigmoid(zf), 0.0)
    zglu = _dot(hb, wglu)
    u = zglu[:, :CONV_CH] * _sigmoid(zglu[:, CONV_CH:])
    sg = _sigmoid(_dot(hb, wgate))
    return q, k, zv, logf, u, sg


def _ln_silu(y, g, b):
    mu = jnp.mean(y, axis=-1, keepdims=True)
    d = y - mu
    var = jnp.mean(d * d, axis=-1, keepdims=True)
    z = d * lax.rsqrt(var + EPS) * g + b
    return z * _sigmoid(z)


def _split3(x):
    a = x.astype(BF16)
    r = x - a.astype(F32)
    b = r.astype(BF16)
    c = (r - b.astype(F32)).astype(BF16)
    return a, b, c


def _proj_prompt_kernel(x_ref, gmix_ref, wqkv_ref, wf_ref, wglu_ref, wgate_ref, bf_ref, qg_ref, kg_ref,
                        gsum_ref, tri_ref, wdw_ref, bdw_ref, lng_ref, lnb_ref,
                        qa_ref, ka_ref, va_ref, kt_ref, vt_ref, lf_ref, y_ref, sg_ref, utail_ref,
                        ext_ref, carry_ref, shift_ref, *, tiles_per_seq, tm):
    i = pl.program_id(0)

    @pl.when(i % tiles_per_seq == 0)
    def _():
        ext_ref[0:CONV_HALO, :] = jnp.zeros((CONV_HALO, CONV_CH), F32)
        carry_ref[...] = jnp.zeros_like(carry_ref)

    q, k, v, logf, u, sg = _project(x_ref[...], gmix_ref[...], wqkv_ref[...], wf_ref[...], wglu_ref[...],
                                    wgate_ref[...], bf_ref[...], qg_ref[...], kg_ref[...], gsum_ref[...])
    sg_ref[...] = sg
    lf_ref[...] = logf[:, :N_HEADS]

    tri = tri_ref[...]
    a, b, c3 = _split3(logf)
    c = _dot(tri, a) + _dot(tri, b) + _dot(tri, c3) + carry_ref[0:1, :]
    carry_ref[0:1, :] = c[tm - 1:tm, :]

    ca, cb, cc = (t.astype(F32) for t in _split3(c * LOG2E))
    tail_k = -(pltpu.roll(ca, AUG_K0, 1) + pltpu.roll(cb, AUG_K0 + 8, 1) + pltpu.roll(cc, AUG_K0 + 16, 1))
    tail_q = pltpu.roll(ca, AUG_Q0, 1) + pltpu.roll(cb, AUG_Q0 + 8, 1) + pltpu.roll(cc, AUG_Q0 + 16, 1)
    lane = lax.broadcasted_iota(jnp.int32, (1, LANES), 1)
    low = lane < HEAD_DIM
    for h in range(N_HEADS):
        blk = slice((h // 2) * LANES, (h // 2 + 1) * LANES)
        qh, kh = q[:, blk], k[:, blk]
        if h % 2:
            qh, kh = pltpu.roll(qh, HEAD_DIM, 1), pltpu.roll(kh, HEAD_DIM, 1)
        pick_k = ((lane == AUG_K0 + h) | (lane == AUG_K0 + 8 + h) | (lane == AUG_K0 + 16 + h)).astype(F32)
        pick_q = ((lane == AUG_Q0 + h) | (lane == AUG_Q0 + 8 + h) | (lane == AUG_Q0 + 16 + h)).astype(F32)
        qa_ref[0, h] = jnp.where(low, qh, tail_q + pick_k).astype(BF16)
        ka_ref[0, h] = jnp.where(low, kh, tail_k + pick_q).astype(BF16)

    kt_ref[0] = k.T
    vt = v.T
    vt_ref[0] = vt
    row = lax.broadcasted_iota(jnp.int32, (VT_ROWS - HEAD_DIM, tm), 0)
    ones_row = jnp.where(row == 0, 1.0, 0.0)
    for h in range(N_HEADS):
        va_ref[0, h] = jnp.concatenate([vt[h * HEAD_DIM:(h + 1) * HEAD_DIM], ones_row], axis=0).astype(BF16)

    ext_ref[CONV_HALO:CONV_HALO + tm, :] = u
    off = CONV_HALO - (CONV_K - 1)
    span = tm + CONV_HALO - SUBLANES
    for s in range(1, SUBLANES):
        shift_ref[s - 1] = ext_ref[s:s + span, :]
    rc = min(tm, 128)
    for r0 in range(0, tm, rc):
        for l0 in range(0, CONV_CH, LANES):
            acc = jnp.zeros((rc, LANES), F32)
            for j in range(CONV_K):
                s = (off + j) % SUBLANES
                a0 = off + j - s + r0
                src = ext_ref if s == 0 else shift_ref.at[s - 1]
                acc = acc + src[a0:a0 + rc, l0:l0 + LANES] * wdw_ref[j:j + 1, l0:l0 + LANES]
            ext_ref[CONV_HALO + tm + r0:CONV_HALO + tm + r0 + rc, l0:l0 + LANES] = acc
    yc = ext_ref[CONV_HALO + tm:CONV_HALO + 2 * tm, :] + bdw_ref[...]
    y_ref[...] = _ln_silu(yc, lng_ref[...], lnb_ref[...]).astype(BF16)
    tail = u[tm - CONV_HALO:, :]
    ext_ref[0:CONV_HALO, :] = tail
    utail_ref[0] = tail


def _proj_sample_kernel(x_ref, gmix_ref, wqkv_ref, wf_ref, wglu_ref, wgate_ref, bf_ref, qg_ref, kg_ref,
                        gsum_ref, st_ref, wdw_ref, bdw_ref, lng_ref, lnb_ref,
                        qb_ref, k32_ref, v32_ref, lf_ref, u_ref, y_ref, sg_ref):
    q, k, v, logf, u, sg = _project(x_ref[...], gmix_ref[...], wqkv_ref[...], wf_ref[...], wglu_ref[...],
                                    wgate_ref[...], bf_ref[...], qg_ref[...], kg_ref[...], gsum_ref[...])
    qb_ref[...] = q.astype(BF16)
    k32_ref[...] = k
    v32_ref[...] = v
    sg_ref[...] = sg
    lf_ref[...] = logf[:, :N_HEADS]
    u_ref[...] = u
    acc = u * wdw_ref[CONV_K - 1:CONV_K, :]
    for j in range(CONV_K - 1):
        acc = acc + st_ref[j] * wdw_ref[j:j + 1, :]
    y_ref[...] = _ln_silu(acc + bdw_ref[...], lng_ref[...], lnb_ref[...]).astype(BF16)


def _prep_proj_weights(norm_mix_g, w_in, b_forget, q_norm_g, k_norm_g):
    d = w_in.shape[0]
    o_f = 3 * ATTN_W
    o_glu = o_f + N_HEADS
    o_gate = o_glu + 2 * CONV_CH
    wqkv = w_in[:, :o_f].astype(BF16)
    wf = jnp.pad(w_in[:, o_f:o_glu], ((0, 0), (0, LANES - N_HEADS))).astype(BF16)
    wglu = w_in[:, o_glu:o_gate].astype(BF16)
    wgate = w_in[:, o_gate:].astype(BF16)
    bf = jnp.pad(b_forget, (0, LANES - N_HEADS)).reshape(1, LANES)
    qg = jnp.tile(q_norm_g, N_HEADS).reshape(1, ATTN_W)
    kg = jnp.tile(k_norm_g, N_HEADS).reshape(1, ATTN_W)
    hid = jnp.arange(ATTN_W) // HEAD_DIM
    gsum = (hid[:, None] == hid[None, :]).astype(BF16)
    return (norm_mix_g.reshape(1, d), wqkv, wf, wglu, wgate, bf, qg, kg, gsum)


def _proj_prompt(x2d, pw, conv_w, batch, seq, tm):
    n, d = x2d.shape
    tps = seq // tm
    tri = (jnp.arange(tm)[:, None] >= jnp.arange(tm)[None, :]).astype(BF16)
    wdw, bdw, lng, lnb = conv_w
    row = lambda w: pl.BlockSpec((tm, w), lambda i: (i, 0))
    in_specs = ([row(d)] + [_const_spec(a.shape) for a in pw] + [_const_spec(tri.shape)]
                + [_const_spec(a.shape) for a in conv_w])
    head_rows = jax.ShapeDtypeStruct((batch, N_HEADS, seq, LANES), BF16)
    head_rows_spec = pl.BlockSpec((1, N_HEADS, tm, LANES), lambda i: (i // tps, 0, i % tps, 0))
    seq_minor = jax.ShapeDtypeStruct((batch, ATTN_W, seq), F32)
    seq_minor_spec = pl.BlockSpec((1, ATTN_W, tm), lambda i: (i // tps, 0, i % tps))
    out_shape = (
        head_rows, head_rows, jax.ShapeDtypeStruct((batch, N_HEADS, VT_ROWS, seq), BF16), seq_minor, seq_minor,
        jax.ShapeDtypeStruct((n, N_HEADS), F32), jax.ShapeDtypeStruct((n, CONV_CH), BF16),
        jax.ShapeDtypeStruct((n, 2 * d), F32), jax.ShapeDtypeStruct((batch, CONV_HALO, CONV_CH), F32),
    )
    out_specs = (head_rows_spec, head_rows_spec,
                 pl.BlockSpec((1, N_HEADS, VT_ROWS, tm), lambda i: (i // tps, 0, 0, i % tps)),
                 seq_minor_spec, seq_minor_spec, row(N_HEADS),
                 row(CONV_CH), row(2 * d), pl.BlockSpec((1, CONV_HALO, CONV_CH), lambda i: (i // tps, 0, 0)))
    return pl.pallas_call(
        functools.partial(_proj_prompt_kernel, tiles_per_seq=tps, tm=tm),
        grid=(n // tm,), in_specs=in_specs, out_specs=out_specs, out_shape=out_shape,
        scratch_shapes=[pltpu.VMEM((CONV_HALO + 2 * tm, CONV_CH), F32), pltpu.VMEM((SUBLANES, LANES), F32),
                        pltpu.VMEM((SUBLANES - 1, tm + CONV_HALO - SUBLANES, CONV_CH), F32)],
        compiler_params=pltpu.CompilerParams(dimension_semantics=("arbitrary",), vmem_limit_bytes=VMEM_LIMIT),
        name="proj_prompt",
    )(x2d, *pw, tri, wdw, bdw, lng, lnb)


def _proj_sample(x2d, pw, state_t, conv_w):
    n, d = x2d.shape
    args = (x2d,) + tuple(pw) + (state_t,) + tuple(conv_w)
    full = lambda a: pl.BlockSpec(a.shape, lambda i, nd=a.ndim: (0,) * nd)
    out_shape = (
        jax.ShapeDtypeStruct((n, ATTN_W), BF16), jax.ShapeDtypeStruct((n, ATTN_W), F32),
        jax.ShapeDtypeStruct((n, ATTN_W), F32), jax.ShapeDtypeStruct((n, N_HEADS), F32),
        jax.ShapeDtypeStruct((n, CONV_CH), F32), jax.ShapeDtypeStruct((n, CONV_CH), BF16),
        jax.ShapeDtypeStruct((n, 2 * d), F32),
    )
    return pl.pallas_call(
        _proj_sample_kernel, grid=(1,), in_specs=[full(a) for a in args],
        out_specs=tuple(full(s) for s in out_shape), out_shape=out_shape,
        compiler_params=pltpu.CompilerParams(dimension_semantics=("arbitrary",), vmem_limit_bytes=VMEM_LIMIT),
        name="proj_sample",
    )(*args)


def _attn_fused_kernel(pt_ref, q_ref, k_ref, v_ref, qs_ref, kn_ref, vn_ref, lfn_ref, kc_hbm, vc_hbm, fc_hbm,
                       o_ref, os_ref, kbuf, vbuf, fbuf, sems, m_ref, l_ref, r_ref, acc_ref, qrep_ref, cnt_ref,
                       *, tq, pp, chunks_per_seq, n_pages, n_chunks):
    b_id, h_id, i = pl.program_id(0), pl.program_id(1), pl.program_id(2)
    heads = range(2)
    qs = [q_ref[0, h] for h in heads]

    def page_copies(n, slot):
        seq = n // chunks_per_seq
        c = n % chunks_per_seq
        copies = []
        for t in range(pp):
            idx = n_pages - 1 - jnp.minimum(c * pp + t, n_pages - 1)
            page = pt_ref[seq, idx]
            copies.append(pltpu.make_async_copy(kc_hbm.at[page], kbuf.at[slot, t], sems.at[slot, 0]))
            copies.append(pltpu.make_async_copy(vc_hbm.at[page], vbuf.at[slot, t], sems.at[slot, 1]))
            copies.append(pltpu.make_async_copy(fc_hbm.at[page], fbuf.at[slot, t], sems.at[slot, 2]))
        return copies

    @pl.when((b_id == 0) & (h_id == 0) & (i == 0))
    def _():
        cnt_ref[0] = 0
        for cp in page_copies(0, 0):
            cp.start()

    hrow = lax.broadcasted_iota(jnp.int32, (N_HEADS, ATTN_W), 0)
    hlane = lax.broadcasted_iota(jnp.int32, (N_HEADS, ATTN_W), 1) // HEAD_DIM
    diag = hrow == hlane
    srow = lax.broadcasted_iota(jnp.int32, (PAGE_SIZE, 2 * PAGE_SIZE), 0)
    scol = lax.broadcasted_iota(jnp.int32, (PAGE_SIZE, 2 * PAGE_SIZE), 1)
    suffix = jnp.where((srow > scol) | (scol >= PAGE_SIZE), 1.0, 0.0).astype(F32)

    def head_rows(h):
        return slice(h * HEAD_DIM, (h + 1) * HEAD_DIM)

    def decode_enter():
        n = cnt_ref[0]
        slot = n % 2
        seq = n // chunks_per_seq
        for cp in page_copies(n, slot):
            cp.wait()

        @pl.when(n + 1 < n_chunks)
        def _():
            for cp in page_copies(n + 1, 1 - slot):
                cp.start()

        @pl.when(n % chunks_per_seq == 0)
        def _():
            qrow = qs_ref[seq].astype(F32)
            m_ref[...] = jnp.sum(jnp.where(diag, qrow, 0.0) * kn_ref[seq], axis=-1, keepdims=True)
            l_ref[...] = jnp.ones_like(l_ref)
            r_ref[...] = lfn_ref[seq]
            qrep_ref[...] = jnp.broadcast_to(qrow, (LANES, ATTN_W)).T
            vrep = jnp.broadcast_to(vn_ref[seq], (LANES, ATTN_W)).T
            lane0 = lax.broadcasted_iota(jnp.int32, (ATTN_W, LANES), 1) == 0
            acc_ref[...] = jnp.where(lane0, vrep, 0.0)

    def decode_chunk():
        n = cnt_ref[0]
        slot = n % 2
        c = n % chunks_per_seq
        lf_all = jnp.concatenate([fbuf[slot, t] for t in range(pp)], axis=0)
        sx = jnp.dot(lf_all, suffix, precision=lax.Precision.HIGHEST, preferred_element_type=F32)
        r = r_ref[...]
        parts = []
        for t in range(pp):
            valid = c * pp + t < n_pages
            bias = r + sx[t * N_HEADS:(t + 1) * N_HEADS, :PAGE_SIZE]
            r = r + jnp.where(valid, sx[t * N_HEADS:(t + 1) * N_HEADS, PAGE_SIZE:PAGE_SIZE + 1], 0.0)
            rows = []
            for h in range(N_HEADS):
                prod = kbuf[slot, t, head_rows(h), :] * qrep_ref[head_rows(h), :]
                rows.append(jnp.sum(prod, axis=0, keepdims=True))
            parts.append(jnp.where(valid, jnp.concatenate(rows, axis=0) + bias * LOG2E, -jnp.inf))
        r_ref[...] = r
        s = jnp.concatenate(parts, axis=1)
        m = m_ref[...]
        m_new = jnp.maximum(m, jnp.max(s, axis=-1, keepdims=True))
        alpha = jnp.exp2(m - m_new)
        p = jnp.exp2(s - m_new)
        l_ref[...] = alpha * l_ref[...] + jnp.sum(p, axis=-1, keepdims=True)
        m_ref[...] = m_new
        for h in range(N_HEADS):
            pv = jnp.zeros((HEAD_DIM, PAGE_SIZE), F32)
            for t in range(pp):
                pv = pv + vbuf[slot, t, head_rows(h), :] * p[h:h + 1, t * PAGE_SIZE:(t + 1) * PAGE_SIZE]
            acc_ref[head_rows(h), :] = alpha[h:h + 1, :] * acc_ref[head_rows(h), :] + pv

    def decode_leave():
        n = cnt_ref[0]

        @pl.when(n % chunks_per_seq == chunks_per_seq - 1)
        def _():
            o = jnp.sum(acc_ref[...].T, axis=0, keepdims=True)
            inv_l = jnp.sum(jnp.where(diag, 1.0 / l_ref[...], 0.0), axis=0, keepdims=True)
            os_ref[n // chunks_per_seq] = (o * inv_l).astype(BF16)

        cnt_ref[0] = n + 1

    def block(j, carry, masked):
        start = pl.multiple_of(j * tq, tq)
        decode_enter()
        decode_chunk()
        new = []
        for h in heads:
            m, acc = carry[h]
            ks = k_ref[0, h, pl.ds(start, tq), :]
            st = lax.dot_general(ks, qs[h], (((1,), (1,)), ((), ())), preferred_element_type=F32)
            if masked:
                key = lax.broadcasted_iota(jnp.int32, (tq, tq), 0)
                qry = lax.broadcasted_iota(jnp.int32, (tq, tq), 1)
                st = jnp.where(key <= qry, st, -jnp.inf)
            m_new = jnp.maximum(m, jnp.max(st, axis=0, keepdims=True))
            alpha = jnp.exp2(m - m_new)
            pt = jnp.exp2(st - m_new).astype(BF16)
            vs = v_ref[0, h, :, pl.ds(start, tq)]
            new.append((m_new, alpha * acc + _dot(vs, pt)))
        decode_leave()
        return tuple(new)

    init = tuple((jnp.full((1, tq), -jnp.inf, F32), jnp.zeros((VT_ROWS, tq), F32)) for _ in heads)
    carry = lax.fori_loop(0, i, lambda j, c: block(j, c, False), init)
    carry = block(i, carry, True)
    outs = []
    for h in heads:
        acc = carry[h][1]
        outs.append(acc[:HEAD_DIM] / acc[HEAD_DIM:HEAD_DIM + 1])
    o_ref[0] = jnp.concatenate(outs, axis=0).T.astype(BF16)


def _attn_fused(qa, ka, va, batch, seq, tq, page_table, qb, k_new, v_new, lf_new, cache_k, cache_v, cache_logf):
    hp = N_HEADS // 2
    nq = seq // tq
    bd, n_pages = page_table.shape
    n_chunks = batch * hp * nq * (nq + 1) // 2
    assert n_chunks % bd == 0
    chunks_per_seq = n_chunks // bd
    pp = -(-n_pages // chunks_per_seq)
    n_pool = cache_k.shape[0]
    kc = cache_k.transpose(0, 2, 3, 1).reshape(n_pool, ATTN_W, PAGE_SIZE)
    vc = cache_v.transpose(0, 2, 3, 1).reshape(n_pool, ATTN_W, PAGE_SIZE)
    fc = cache_logf.transpose(0, 2, 1)
    whole = lambda shape: pl.BlockSpec(shape, lambda b, h, i, pt: (0,) * len(shape))
    hbm = pl.BlockSpec(memory_space=pl.ANY)
    grid_spec = pltpu.PrefetchScalarGridSpec(
        num_scalar_prefetch=1, grid=(batch, hp, nq),
        in_specs=[
            pl.BlockSpec((1, 2, tq, LANES), lambda b, h, i, pt: (b, h, i, 0)),
            pl.BlockSpec((1, 2, seq, LANES), lambda b, h, i, pt: (b, h, 0, 0)),
            pl.BlockSpec((1, 2, VT_ROWS, seq), lambda b, h, i, pt: (b, h, 0, 0)),
            whole((bd, 1, ATTN_W)), whole((bd, 1, ATTN_W)), whole((bd, 1, ATTN_W)), whole((bd, N_HEADS, 1)),
            hbm, hbm, hbm,
        ],
        out_specs=(pl.BlockSpec((1, tq, LANES), lambda b, h, i, pt: (b, i, h)), whole((bd, 1, ATTN_W))),
        scratch_shapes=[
            pltpu.VMEM((2, pp, ATTN_W, PAGE_SIZE), F32), pltpu.VMEM((2, pp, ATTN_W, PAGE_SIZE), F32),
            pltpu.VMEM((2, pp, N_HEADS, PAGE_SIZE), F32), pltpu.SemaphoreType.DMA((2, 3)),
            pltpu.VMEM((N_HEADS, 1), F32), pltpu.VMEM((N_HEADS, 1), F32), pltpu.VMEM((N_HEADS, 1), F32),
            pltpu.VMEM((ATTN_W, LANES), F32), pltpu.VMEM((ATTN_W, LANES), F32), pltpu.SMEM((1,), jnp.int32),
        ],
    )
    attn, attn_s = pl.pallas_call(
        functools.partial(_attn_fused_kernel, tq=tq, pp=pp, chunks_per_seq=chunks_per_seq, n_pages=n_pages,
                          n_chunks=n_chunks),
        grid_spec=grid_spec,
        out_shape=(jax.ShapeDtypeStruct((batch, seq, ATTN_W), BF16), jax.ShapeDtypeStruct((bd, 1, ATTN_W), BF16)),
        compiler_params=pltpu.CompilerParams(dimension_semantics=("arbitrary", "arbitrary", "arbitrary"),
                                             vmem_limit_bytes=VMEM_LIMIT),
        name="attn_fused",
    )(page_table, qa, ka, va, qb.reshape(bd, 1, ATTN_W), k_new.reshape(bd, 1, ATTN_W),
      v_new.reshape(bd, 1, ATTN_W), lf_new.reshape(bd, N_HEADS, 1), kc, vc, fc)
    return attn, attn_s.reshape(bd, ATTN_W)


def _mix_kernel(x_ref, a_ref, y_ref, sg_ref, wao_ref, wco_ref, wout_ref, gffn_ref, wr_ref, br_ref,
                x1_ref, h2_ref, gates_ref):
    d = x_ref.shape[-1]
    ao = _dot(a_ref[...], wao_ref[...])
    co = _dot(y_ref[...], wco_ref[...])
    merged = sg_ref[:, :d] * ao + sg_ref[:, d:] * co
    x1 = x_ref[...] + _dot(merged.astype(BF16), wout_ref[...])
    x1_ref[...] = x1
    h2 = _rms(x1, gffn_ref[...]).astype(BF16)
    h2_ref[...] = h2

    logits = _dot(h2, wr_ref[...]) + br_ref[...]
    lane = lax.broadcasted_iota(jnp.int32, logits.shape, 1)
    neg = -jnp.inf
    big = jnp.int32(4 * LANES)
    gl = jnp.where(lane < N_GROUPS, logits, neg)
    gmax = jnp.max(gl, axis=-1, keepdims=True)
    gval = 1.0 / jnp.sum(jnp.exp(gl - gmax), axis=-1, keepdims=True)
    gidx = jnp.min(jnp.where(gl == gmax, lane, big), axis=-1, keepdims=True)
    ex = lane - EXPERT_LANE0
    in_group = (ex >= gidx * EXP_PER_GROUP) & (ex < (gidx + 1) * EXP_PER_GROUP)
    el = jnp.where(in_group, logits, neg)
    v1 = jnp.max(el, axis=-1, keepdims=True)
    i1 = jnp.min(jnp.where(el == v1, lane, big), axis=-1, keepdims=True)
    el2 = jnp.where(lane == i1, neg, el)
    v2 = jnp.max(el2, axis=-1, keepdims=True)
    i2 = jnp.min(jnp.where(el2 == v2, lane, big), axis=-1, keepdims=True)
    e2 = jnp.exp(v2 - v1)
    w1 = gval / (1.0 + e2)
    w2 = gval * e2 / (1.0 + e2)
    gates_ref[...] = (jnp.where(lane == i1, w1, 0.0) + jnp.where(lane == i2, w2, 0.0)
                      + jnp.where(lane == gidx, 1.0, 0.0))


def _mix(x2d, attn, y, sg, mw, tm):
    n, d = x2d.shape
    row = lambda w: pl.BlockSpec((tm, w), lambda i: (i, 0))
    return pl.pallas_call(
        _mix_kernel, grid=(n // tm,),
        in_specs=[row(d), row(ATTN_W), row(CONV_CH), row(2 * d)] + [_const_spec(a.shape) for a in mw],
        out_specs=(row(d), row(d), row(LANES)),
        out_shape=(jax.ShapeDtypeStruct((n, d), F32), jax.ShapeDtypeStruct((n, d), BF16),
                   jax.ShapeDtypeStruct((n, LANES), F32)),
        compiler_params=pltpu.CompilerParams(dimension_semantics=("arbitrary",), vmem_limit_bytes=VMEM_LIMIT),
        name="mix",
    )(x2d, attn, y, sg, *mw)


def _moe_kernel(x1_ref, h2_ref, gates_ref, p_ref, wg_ref, wu_ref, wd_ref, gple_ref, wpg_ref, wpp_ref,
                o_ref, acc_ref, *, eb):
    jb = pl.program_id(1)

    @pl.when(jb == 0)
    def _():
        acc_ref[...] = jnp.zeros_like(acc_ref)

    h2 = h2_ref[...]
    gates = gates_ref[...]
    lane = lax.broadcasted_iota(jnp.int32, gates.shape, 1)
    for e in range(eb):
        gcol = jnp.sum(jnp.where(lane == EXPERT_LANE0 + jb * eb + e, gates, 0.0), axis=-1, keepdims=True)
        hg = _dot(h2, wg_ref[e])
        hu = _dot(h2, wu_ref[e])
        he = (hg * _sigmoid(hg)) * hu * gcol
        acc_ref[...] += _dot(he.astype(BF16), wd_ref[e])

    @pl.when(jb == pl.num_programs(1) - 1)
    def _():
        x2 = x1_ref[...] + acc_ref[...]
        g = _sigmoid(_dot(_rms(x2, gple_ref[...]).astype(BF16), wpg_ref[...]))
        o_ref[...] = x2 + g * _dot(p_ref[...].astype(BF16), wpp_ref[...])


def _moe_ple(x1, h2, gates, p2d, ew, pw, tm, eb):
    n, d = x1.shape
    wg, wu, wd = ew
    row = lambda w: pl.BlockSpec((tm, w), lambda i, j: (i, 0))
    return pl.pallas_call(
        functools.partial(_moe_kernel, eb=eb), grid=(n // tm, N_EXPERTS // eb),
        in_specs=[row(d), row(d), row(LANES), row(p2d.shape[1]),
                  pl.BlockSpec((eb, d, EXPERT_FF), lambda i, j: (j, 0, 0)),
                  pl.BlockSpec((eb, d, EXPERT_FF), lambda i, j: (j, 0, 0)),
                  pl.BlockSpec((eb, EXPERT_FF, d), lambda i, j: (j, 0, 0))]
                 + [_const_spec(a.shape) for a in pw],
        out_specs=row(d), out_shape=jax.ShapeDtypeStruct((n, d), F32),
        scratch_shapes=[pltpu.VMEM((tm, d), F32)],
        compiler_params=pltpu.CompilerParams(dimension_semantics=("arbitrary", "arbitrary"),
                                             vmem_limit_bytes=VMEM_LIMIT),
        name="moe_ple",
    )(x1, h2, gates, p2d, wg, wu, wd, *pw)


def _moe_group_kernel(h2_ref, gates_ref, tri_ref, wg_ref, wu_ref, wd_ref, o_ref, route_ref, gsplit_ref,
                      *, tm, rows):
    g = pl.program_id(1)

    @pl.when(g == 0)
    def _():
        o_ref[...] = jnp.zeros_like(o_ref)
        gates = gates_ref[...]
        onehot_t = gates.T[0:8]
        route_ref[0:8, :] = onehot_t
        route_ref[8:16, :] = _dot(onehot_t.astype(BF16), tri_ref[...])
        a, b, c = _split3(gates)
        gsplit_ref[...] = jnp.concatenate([a, b, c], axis=1)

    member = route_ref[pl.ds(g, 1), :]
    rank = route_ref[pl.ds(8 + g, 1), :].astype(jnp.int32)
    count = jnp.sum(member).astype(jnp.int32)
    slot = lax.broadcasted_iota(jnp.int32, (rows, tm), 0)
    lane = lax.broadcasted_iota(jnp.int32, (1, LANES), 1)

    def block(b, carry):
        pick = jnp.where(slot == rank - b * rows, member, 0.0).astype(BF16)
        hc = _dot(pick, h2_ref[...]).astype(BF16)
        gc3 = _dot(pick, gsplit_ref[...])
        gc = gc3[:, :LANES] + gc3[:, LANES:2 * LANES] + gc3[:, 2 * LANES:]
        yb = jnp.zeros((rows, o_ref.shape[-1]), F32)
        for e in range(EXP_PER_GROUP):
            col = jnp.sum(jnp.where(lane == EXPERT_LANE0 + g * EXP_PER_GROUP + e, gc, 0.0), axis=-1, keepdims=True)
            hg = _dot(hc, wg_ref[e])
            hu = _dot(hc, wu_ref[e])
            he = (hg * _sigmoid(hg)) * hu * col
            yb = yb + _dot(he.astype(BF16), wd_ref[e])
        o_ref[...] += lax.dot_general(pick, yb.astype(BF16), (((0,), (0,)), ((), ())),
                                      preferred_element_type=F32)
        return carry

    lax.fori_loop(0, (count + rows - 1) // rows, block, 0)


def _moe_group(h2, gates, ew, tm, rows):
    n, d = h2.shape
    wg, wu, wd = ew
    tri = (jnp.arange(tm)[:, None] < jnp.arange(tm)[None, :]).astype(BF16)
    row = lambda w: pl.BlockSpec((tm, w), lambda i, g: (i, 0))
    grp = lambda a, b: pl.BlockSpec((EXP_PER_GROUP, a, b), lambda i, g: (g, 0, 0))
    return pl.pallas_call(
        functools.partial(_moe_group_kernel, tm=tm, rows=rows), grid=(n // tm, N_GROUPS),
        in_specs=[row(d), row(LANES), _const_spec(tri.shape), grp(d, EXPERT_FF), grp(d, EXPERT_FF),
                  grp(EXPERT_FF, d)],
        out_specs=row(d), out_shape=jax.ShapeDtypeStruct((n, d), F32),
        scratch_shapes=[pltpu.VMEM((16, tm), F32), pltpu.VMEM((tm, 3 * LANES), BF16)],
        compiler_params=pltpu.CompilerParams(dimension_semantics=("arbitrary", "arbitrary"),
                                             vmem_limit_bytes=VMEM_LIMIT),
        name="moe_group",
    )(h2, gates, tri, wg, wu, wd)


def _ple_kernel(x1_ref, y_ref, p_ref, gple_ref, wpg_ref, wpp_ref, o_ref):
    x2 = x1_ref[...] + y_ref[...]
    g = _sigmoid(_dot(_rms(x2, gple_ref[...]).astype(BF16), wpg_ref[...]))
    o_ref[...] = x2 + g * _dot(p_ref[...].astype(BF16), wpp_ref[...])


def _ple(x1, ymoe, p2d, pw, tm):
    n, d = x1.shape
    row = lambda w: pl.BlockSpec((tm, w), lambda i: (i, 0))
    return pl.pallas_call(
        _ple_kernel, grid=(n // tm,),
        in_specs=[row(d), row(d), row(p2d.shape[1])] + [_const_spec(a.shape) for a in pw],
        out_specs=row(d), out_shape=jax.ShapeDtypeStruct((n, d), F32),
        compiler_params=pltpu.CompilerParams(dimension_semantics=("arbitrary",), vmem_limit_bytes=VMEM_LIMIT),
        name="ple",
    )(x1, ymoe, p2d, *pw)


def _tile(n, pref):
    return pref if n % pref == 0 else n


def kernel(x_prompt, x_sample, cache_k, cache_v, cache_logf, state_conv, page_table, p_prompt, p_sample, norm_mix_g, w_in, b_forget, q_norm_g, k_norm_g, w_attn_o, conv_dw_w, conv_dw_b, conv_ln_g, conv_ln_b, w_conv_o, w_out, norm_ffn_g, w_router_group, b_router_group, w_router_expert, b_router_expert, w_exp_gate, w_exp_up, w_exp_down, norm_ple_g, w_ple_gate, w_ple_proj):
    depth = w_in.shape[0]
    assert depth == 1
    li = 0
    batch, seq, d = x_prompt.shape
    bd, dec_seq, _ = x_sample.shape
    assert dec_seq == 1
    n_pages = page_table.shape[1]

    pw = _prep_proj_weights(norm_mix_g[li], w_in[li], b_forget[li], q_norm_g[li], k_norm_g[li])
    conv_w = (conv_dw_w[li], conv_dw_b[li].reshape(1, CONV_CH), conv_ln_g[li].reshape(1, CONV_CH),
              conv_ln_b[li].reshape(1, CONV_CH))
    wr = jnp.zeros((d, LANES), F32)
    wr = wr.at[:, :N_GROUPS].set(w_router_group[li]).at[:, EXPERT_LANE0:EXPERT_LANE0 + N_EXPERTS].set(
        w_router_expert[li]).astype(BF16)
    br = jnp.zeros((1, LANES), F32)
    br = br.at[0, :N_GROUPS].set(b_router_group[li]).at[0, EXPERT_LANE0:EXPERT_LANE0 + N_EXPERTS].set(
        b_router_expert[li])
    mw = (w_attn_o[li].astype(BF16), w_conv_o[li].astype(BF16), w_out[li].astype(BF16),
          norm_ffn_g[li].reshape(1, d), wr, br)
    ew = (w_exp_gate[li].astype(BF16), w_exp_up[li].astype(BF16), w_exp_down[li].astype(BF16))
    plew = (norm_ple_g[li].reshape(1, d), w_ple_gate[li].astype(BF16), w_ple_proj[li].astype(BF16))

    xp = x_prompt.reshape(batch * seq, d)
    tm = _tile(seq, 512)
    qa, ka, va, kt, vt, lf, y, sg, utail = _proj_prompt(xp, pw, conv_w, batch, seq, tm)
    xs = x_sample.reshape(bd, d)
    state_t = state_conv[li].transpose(1, 0, 2)
    qs, ks, vs, lfs, us, ys, sgs = _proj_sample(xs, pw, state_t, conv_w)
    attn, attn_s = _attn_fused(qa, ka, va, batch, seq, _tile(seq, 1024), page_table, qs, ks, vs, lfs,
                               cache_k[li], cache_v[li], cache_logf[li])
    attn = attn.reshape(batch * seq, ATTN_W)

    x1, h2, gates = _mix(xp, attn, y, sg, mw, tm)
    ymoe = _moe_group(h2, gates, ew, _tile(batch * seq, MOE_TILE), MOE_BLOCK_ROWS)
    yp = _ple(x1, ymoe, p_prompt[li].reshape(batch * seq, -1), plew, tm)
    y_prompt = yp.reshape(batch, seq, d)
    to_rows = lambda t: t.reshape(batch, N_HEADS, HEAD_DIM, seq).transpose(0, 3, 1, 2)[None]
    new_k_prompt = to_rows(kt)
    new_v_prompt = to_rows(vt)
    new_logf_prompt = lf.reshape(1, batch, seq, N_HEADS)
    new_conv_prompt = utail[:, CONV_HALO - (CONV_K - 1):, :].reshape(1, batch, CONV_K - 1, CONV_CH)

    x1s, h2s, gates_s = _mix(xs, attn_s, ys, sgs, mw, bd)
    ysm = _moe_ple(x1s, h2s, gates_s, p_sample[li].reshape(bd, -1), ew, plew, bd, 4)
    y_sample = ysm.reshape(bd, 1, d)
    new_k_sample = ks.reshape(1, bd, 1, N_HEADS, HEAD_DIM)
    new_v_sample = vs.reshape(1, bd, 1, N_HEADS, HEAD_DIM)
    new_logf_sample = lfs.reshape(1, bd, 1, N_HEADS)
    new_conv_sample = jnp.concatenate([state_conv[li][:, 1:, :], us[:, None, :]], axis=1)[None]

    return (y_prompt, y_sample, new_k_prompt, new_v_prompt, new_logf_prompt, new_conv_prompt,
            new_k_sample, new_v_sample, new_logf_sample, new_conv_sample)
```

```python
import functools

import jax
import jax.numpy as jnp
from jax import lax
from jax.experimental import pallas as pl
from jax.experimental.pallas import tpu as pltpu

F32 = jnp.float32
BF16 = jnp.bfloat16

N_HEADS = 8
HEAD_DIM = 64
ATTN_W = N_HEADS * HEAD_DIM
CONV_CH = 512
CONV_K = 31
N_GROUPS = 4
EXP_PER_GROUP = 8
N_EXPERTS = N_GROUPS * EXP_PER_GROUP
EXPERT_FF = 256
PAGE_SIZE = 128
EPS = 1e-6

LANES = 128
SUBLANES = 8
CONV_HALO = 32
VMEM_LIMIT = 56 * 1024 * 1024
EXPERT_LANE0 = 32
LOG2E = 1.4426950408889634
AUG_K0 = 64
AUG_Q0 = 88
VT_ROWS = HEAD_DIM + 16
EXP2_HEADROOM = 100.0
MOE_TILE = 1024
MOE_BLOCK_ROWS = 320


def _const_spec(shape):
    nd = len(shape)
    return pl.BlockSpec(shape, lambda *_: (0,) * nd, pipeline_mode=pl.Buffered(1))


def _sigmoid(x):
    return 0.5 * jnp.tanh(0.5 * x) + 0.5


def _log_sigmoid(x):
    return -(jnp.maximum(-x, 0.0) + jnp.log1p(jnp.exp(-jnp.abs(x))))


def _rms(x, g):
    return x * lax.rsqrt(jnp.mean(x * x, axis=-1, keepdims=True) + EPS) * g


def _dot(a, b):
    return jnp.dot(a, b, preferred_element_type=F32)


def _project(x, gmix, wqkv, wf, wglu, wgate, bf, qg, kg, gsum):
    hb = _rms(x, gmix).astype(BF16)
    zqkv = _dot(hb, wqkv)
    zq = zqkv[:, :ATTN_W]
    zk = zqkv[:, ATTN_W:2 * ATTN_W]
    zv = zqkv[:, 2 * ATTN_W:]

    def head_norm(z, g):
        ss = _dot((z * z).astype(BF16), gsum)
        return z * lax.rsqrt(ss * (1.0 / HEAD_DIM) + EPS) * g

    q = head_norm(zq, qg) * (HEAD_DIM ** -0.5 * LOG2E)
    k = head_norm(zk, kg)
    zf = _dot(hb, wf) + bf
    lane = lax.broadcasted_iota(jnp.int32, zf.shape, 1)
    logf = jnp.where(lane < N_HEADS, _log_sigmoid(zf), 0.0)
    zglu = _dot(hb, wglu)
    u = zglu[:, :CONV_CH] * _sigmoid(zglu[:, CONV_CH:])
    sg = _sigmoid(_dot(hb, wgate))
    return q, k, zv, logf, u, sg


def _ln_silu(y, g, b):
    mu = jnp.mean(y, axis=-1, keepdims=True)
    d = y - mu
    var = jnp.mean(d * d, axis=-1, keepdims=True)
    z = d * lax.rsqrt(var + EPS) * g + b
    return z * _sigmoid(z)


def _split3(x):
    a = x.astype(BF16)
    r = x - a.astype(F32)
    b = r.astype(BF16)
    c = (r - b.astype(F32)).astype(BF16)
    return a, b, c


def _proj_prompt_kernel(x_ref, gmix_ref, wqkv_ref, wf_ref, wglu_ref, wgate_ref, bf_ref, qg_ref, kg_ref,
                        gsum_ref, tri_ref, wdw_ref, bdw_ref, lng_ref, lnb_ref,
                        qa_ref, ka_ref, va_ref, kt_ref, vt_ref, lf_ref, y_ref, sg_ref, utail_ref,
                        ext_ref, carry_ref, shift_ref, *, tiles_per_seq, tm):
    i = pl.program_id(0)

    @pl.when(i % tiles_per_seq == 0)
    def _():
        ext_ref[0:CONV_HALO, :] = jnp.zeros((CONV_HALO, CONV_CH), F32)
        carry_ref[...] = jnp.zeros_like(carry_ref)

    q, k, v, logf, u, sg = _project(x_ref[...], gmix_ref[...], wqkv_ref[...], wf_ref[...], wglu_ref[...],
                                    wgate_ref[...], bf_ref[...], qg_ref[...], kg_ref[...], gsum_ref[...])
    sg_ref[...] = sg
    lf_ref[...] = logf[:, :N_HEADS]

    tri = tri_ref[...]
    a, b, c3 = _split3(logf)
    c = _dot(tri, a) + _dot(tri, b) + _dot(tri, c3) + carry_ref[0:1, :]
    carry_ref[0:1, :] = c[tm - 1:tm, :]

    ca, cb, cc = (t.astype(F32) for t in _split3(c * LOG2E))
    tail_k = -(pltpu.roll(ca, AUG_K0, 1) + pltpu.roll(cb, AUG_K0 + 8, 1) + pltpu.roll(cc, AUG_K0 + 16, 1))
    tail_q = pltpu.roll(ca, AUG_Q0, 1) + pltpu.roll(cb, AUG_Q0 + 8, 1) + pltpu.roll(cc, AUG_Q0 + 16, 1)
    lane = lax.broadcasted_iota(jnp.int32, (1, LANES), 1)
    low = lane < HEAD_DIM
    for h in range(N_HEADS):
        blk = slice((h // 2) * LANES, (h // 2 + 1) * LANES)
        qh, kh = q[:, blk], k[:, blk]
        if h % 2:
            qh, kh = pltpu.roll(qh, HEAD_DIM, 1), pltpu.roll(kh, HEAD_DIM, 1)
        pick_k = ((lane == AUG_K0 + h) | (lane == AUG_K0 + 8 + h) | (lane == AUG_K0 + 16 + h)).astype(F32)
        pick_q = ((lane == AUG_Q0 + h) | (lane == AUG_Q0 + 8 + h) | (lane == AUG_Q0 + 16 + h)).astype(F32)
        qa_ref[0, h] = jnp.where(low, qh, tail_q + pick_k).astype(BF16)
        ka_ref[0, h] = jnp.where(low, kh, tail_k + pick_q).astype(BF16)

    kt_ref[0] = k.T
    vt = v.T
    vt_ref[0] = vt
    row = lax.broadcasted_iota(jnp.int32, (VT_ROWS - HEAD_DIM, tm), 0)
    ones_row = jnp.where(row == 0, 1.0, 0.0)
    for h in range(N_HEADS):
        va_ref[0, h] = jnp.concatenate([vt[h * HEAD_DIM:(h + 1) * HEAD_DIM], ones_row], axis=0).astype(BF16)

    ext_ref[CONV_HALO:CONV_HALO + tm, :] = u
    off = CONV_HALO - (CONV_K - 1)
    span = tm + CONV_HALO - SUBLANES
    for s in range(1, SUBLANES):
        shift_ref[s - 1] = ext_ref[s:s + span, :]
    rc = min(tm, 128)
    for r0 in range(0, tm, rc):
        for l0 in range(0, CONV_CH, LANES):
            acc = jnp.zeros((rc, LANES), F32)
            for j in range(CONV_K):
                s = (off + j) % SUBLANES
                a0 = off + j - s + r0
                src = ext_ref if s == 0 else shift_ref.at[s - 1]
                acc = acc + src[a0:a0 + rc, l0:l0 + LANES] * wdw_ref[j:j + 1, l0:l0 + LANES]
            ext_ref[CONV_HALO + tm + r0:CONV_HALO + tm + r0 + rc, l0:l0 + LANES] = acc
    yc = ext_ref[CONV_HALO + tm:CONV_HALO + 2 * tm, :] + bdw_ref[...]
    y_ref[...] = _ln_silu(yc, lng_ref[...], lnb_ref[...]).astype(BF16)
    tail = u[tm - CONV_HALO:, :]
    ext_ref[0:CONV_HALO, :] = tail
    utail_ref[0] = tail


def _proj_sample_kernel(x_ref, gmix_ref, wqkv_ref, wf_ref, wglu_ref, wgate_ref, bf_ref, qg_ref, kg_ref,
                        gsum_ref, st_ref, wdw_ref, bdw_ref, lng_ref, lnb_ref,
                        qb_ref, k32_ref, v32_ref, lf_ref, u_ref, y_ref, sg_ref):
    q, k, v, logf, u, sg = _project(x_ref[...], gmix_ref[...], wqkv_ref[...], wf_ref[...], wglu_ref[...],
                                    wgate_ref[...], bf_ref[...], qg_ref[...], kg_ref[...], gsum_ref[...])
    qb_ref[...] = q.astype(BF16)
    k32_ref[...] = k
    v32_ref[...] = v
    sg_ref[...] = sg
    lf_ref[...] = logf[:, :N_HEADS]
    u_ref[...] = u
    acc = u * wdw_ref[CONV_K - 1:CONV_K, :]
    for j in range(CONV_K - 1):
        acc = acc + st_ref[j] * wdw_ref[j:j + 1, :]
    y_ref[...] = _ln_silu(acc + bdw_ref[...], lng_ref[...], lnb_ref[...]).astype(BF16)


def _prep_proj_weights(norm_mix_g, w_in, b_forget, q_norm_g, k_norm_g):
    d = w_in.shape[0]
    o_f = 3 * ATTN_W
    o_glu = o_f + N_HEADS
    o_gate = o_glu + 2 * CONV_CH
    wqkv = w_in[:, :o_f].astype(BF16)
    wf = jnp.pad(w_in[:, o_f:o_glu], ((0, 0), (0, LANES - N_HEADS))).astype(BF16)
    wglu = w_in[:, o_glu:o_gate].astype(BF16)
    wgate = w_in[:, o_gate:].astype(BF16)
    bf = jnp.pad(b_forget, (0, LANES - N_HEADS)).reshape(1, LANES)
    qg = jnp.tile(q_norm_g, N_HEADS).reshape(1, ATTN_W)
    kg = jnp.tile(k_norm_g, N_HEADS).reshape(1, ATTN_W)
    hid = jnp.arange(ATTN_W) // HEAD_DIM
    gsum = (hid[:, None] == hid[None, :]).astype(BF16)
    return (norm_mix_g.reshape(1, d), wqkv, wf, wglu, wgate, bf, qg, kg, gsum)


def _proj_prompt(x2d, pw, conv_w, batch, seq, tm):
    n, d = x2d.shape
    tps = seq // tm
    tri = (jnp.arange(tm)[:, None] >= jnp.arange(tm)[None, :]).astype(BF16)
    wdw, bdw, lng, lnb = conv_w
    row = lambda w: pl.BlockSpec((tm, w), lambda i: (i, 0))
    in_specs = ([row(d)] + [_const_spec(a.shape) for a in pw] + [_const_spec(tri.shape)]
                + [_const_spec(a.shape) for a in conv_w])
    head_rows = jax.ShapeDtypeStruct((batch, N_HEADS, seq, LANES), BF16)
    head_rows_spec = pl.BlockSpec((1, N_HEADS, tm, LANES), lambda i: (i // tps, 0, i % tps, 0))
    seq_minor = jax.ShapeDtypeStruct((batch, ATTN_W, seq), F32)
    seq_minor_spec = pl.BlockSpec((1, ATTN_W, tm), lambda i: (i // tps, 0, i % tps))
    out_shape = (
        head_rows, head_rows, jax.ShapeDtypeStruct((batch, N_HEADS, VT_ROWS, seq), BF16), seq_minor, seq_minor,
        jax.ShapeDtypeStruct((n, N_HEADS), F32), jax.ShapeDtypeStruct((n, CONV_CH), BF16),
        jax.ShapeDtypeStruct((n, 2 * d), F32), jax.ShapeDtypeStruct((batch, CONV_HALO, CONV_CH), F32),
    )
    out_specs = (head_rows_spec, head_rows_spec,
                 pl.BlockSpec((1, N_HEADS, VT_ROWS, tm), lambda i: (i // tps, 0, 0, i % tps)),
                 seq_minor_spec, seq_minor_spec, row(N_HEADS),
                 row(CONV_CH), row(2 * d), pl.BlockSpec((1, CONV_HALO, CONV_CH), lambda i: (i // tps, 0, 0)))
    return pl.pallas_call(
        functools.partial(_proj_prompt_kernel, tiles_per_seq=tps, tm=tm),
        grid=(n // tm,), in_specs=in_specs, out_specs=out_specs, out_shape=out_shape,
        scratch_shapes=[pltpu.VMEM((CONV_HALO + 2 * tm, CONV_CH), F32), pltpu.VMEM((SUBLANES, LANES), F32),
                        pltpu.VMEM((SUBLANES - 1, tm + CONV_HALO - SUBLANES, CONV_CH), F32)],
        compiler_params=pltpu.CompilerParams(dimension_semantics=("arbitrary",), vmem_limit_bytes=VMEM_LIMIT),
        name="proj_prompt",
    )(x2d, *pw, tri, wdw, bdw, lng, lnb)


def _proj_sample(x2d, pw, state_t, conv_w):
    n, d = x2d.shape
    args = (x2d,) + tuple(pw) + (state_t,) + tuple(conv_w)
    full = lambda a: pl.BlockSpec(a.shape, lambda i, nd=a.ndim: (0,) * nd)
    out_shape = (
        jax.ShapeDtypeStruct((n, ATTN_W), BF16), jax.ShapeDtypeStruct((n, ATTN_W), F32),
        jax.ShapeDtypeStruct((n, ATTN_W), F32), jax.ShapeDtypeStruct((n, N_HEADS), F32),
        jax.ShapeDtypeStruct((n, CONV_CH), F32), jax.ShapeDtypeStruct((n, CONV_CH), BF16),
        jax.ShapeDtypeStruct((n, 2 * d), F32),
    )
    return pl.pallas_call(
        _proj_sample_kernel, grid=(1,), in_specs=[full(a) for a in args],
        out_specs=tuple(full(s) for s in out_shape), out_shape=out_shape,
        compiler_params=pltpu.CompilerParams(dimension_semantics=("arbitrary",), vmem_limit_bytes=VMEM_LIMIT),
        name="proj_sample",
    )(*args)


def _attn_fused_kernel(pt_ref, q_ref, k_ref, v_ref, qs_ref, kn_ref, vn_ref, lfn_ref, kc_hbm, vc_hbm, fc_hbm,
                       o_ref, os_ref, kbuf, vbuf, fbuf, sems, m_ref, l_ref, r_ref, acc_ref, qrep_ref, cnt_ref,
                       *, tq, pp, chunks_per_seq, n_pages, n_chunks):
    b_id, h_id, i = pl.program_id(0), pl.program_id(1), pl.program_id(2)
    heads = range(2)
    qs = [q_ref[0, h] for h in heads]

    def page_copies(n, slot):
        seq = n // chunks_per_seq
        c = n % chunks_per_seq
        copies = []
        for t in range(pp):
            idx = n_pages - 1 - jnp.minimum(c * pp + t, n_pages - 1)
            page = pt_ref[seq, idx]
            copies.append(pltpu.make_async_copy(kc_hbm.at[page], kbuf.at[slot, t], sems.at[slot, 0]))
            copies.append(pltpu.make_async_copy(vc_hbm.at[page], vbuf.at[slot, t], sems.at[slot, 1]))
            copies.append(pltpu.make_async_copy(fc_hbm.at[page], fbuf.at[slot, t], sems.at[slot, 2]))
        return copies

    @pl.when((b_id == 0) & (h_id == 0) & (i == 0))
    def _():
        cnt_ref[0] = 0
        for cp in page_copies(0, 0):
            cp.start()

    hrow = lax.broadcasted_iota(jnp.int32, (N_HEADS, ATTN_W), 0)
    hlane = lax.broadcasted_iota(jnp.int32, (N_HEADS, ATTN_W), 1) // HEAD_DIM
    diag = hrow == hlane
    srow = lax.broadcasted_iota(jnp.int32, (PAGE_SIZE, 2 * PAGE_SIZE), 0)
    scol = lax.broadcasted_iota(jnp.int32, (PAGE_SIZE, 2 * PAGE_SIZE), 1)
    suffix = jnp.where((srow > scol) | (scol >= PAGE_SIZE), 1.0, 0.0).astype(F32)

    def head_rows(h):
        return slice(h * HEAD_DIM, (h + 1) * HEAD_DIM)

    def decode_enter():
        n = cnt_ref[0]
        slot = n % 2
        seq = n // chunks_per_seq
        for cp in page_copies(n, slot):
            cp.wait()

        @pl.when(n + 1 < n_chunks)
        def _():
            for cp in page_copies(n + 1, 1 - slot):
                cp.start()

        @pl.when(n % chunks_per_seq == 0)
        def _():
            qrow = qs_ref[seq].astype(F32)
            m_ref[...] = jnp.sum(jnp.where(diag, qrow, 0.0) * kn_ref[seq], axis=-1, keepdims=True)
            l_ref[...] = jnp.ones_like(l_ref)
            r_ref[...] = lfn_ref[seq]
            qrep_ref[...] = jnp.broadcast_to(qrow, (LANES, ATTN_W)).T
            vrep = jnp.broadcast_to(vn_ref[seq], (LANES, ATTN_W)).T
            lane0 = lax.broadcasted_iota(jnp.int32, (ATTN_W, LANES), 1) == 0
            acc_ref[...] = jnp.where(lane0, vrep, 0.0)

    def decode_chunk():
        n = cnt_ref[0]
        slot = n % 2
        c = n % chunks_per_seq
        lf_all = jnp.concatenate([fbuf[slot, t] for t in range(pp)], axis=0)
        sx = jnp.dot(lf_all, suffix, precision=lax.Precision.HIGHEST, preferred_element_type=F32)
        r = r_ref[...]
        parts = []
        for t in range(pp):
            valid = c * pp + t < n_pages
            bias = r + sx[t * N_HEADS:(t + 1) * N_HEADS, :PAGE_SIZE]
            r = r + jnp.where(valid, sx[t * N_HEADS:(t + 1) * N_HEADS, PAGE_SIZE:PAGE_SIZE + 1], 0.0)
            rows = []
            for h in range(N_HEADS):
                prod = kbuf[slot, t, head_rows(h), :] * qrep_ref[head_rows(h), :]
                rows.append(jnp.sum(prod, axis=0, keepdims=True))
            parts.append(jnp.where(valid, jnp.concatenate(rows, axis=0) + bias * LOG2E, -jnp.inf))
        r_ref[...] = r
        s = jnp.concatenate(parts, axis=1)
        m = m_ref[...]
        m_new = jnp.maximum(m, jnp.max(s, axis=-1, keepdims=True))
        alpha = jnp.exp2(m - m_new)
        p = jnp.exp2(s - m_new)
        l_ref[...] = alpha * l_ref[...] + jnp.sum(p, axis=-1, keepdims=True)
        m_ref[...] = m_new
        for h in range(N_HEADS):
            pv = jnp.zeros((HEAD_DIM, PAGE_SIZE), F32)
            for t in range(pp):
                pv = pv + vbuf[slot, t, head_rows(h), :] * p[h:h + 1, t * PAGE_SIZE:(t + 1) * PAGE_SIZE]
            acc_ref[head_rows(h), :] = alpha[h:h + 1, :] * acc_ref[head_rows(h), :] + pv

    def decode_leave():
        n = cnt_ref[0]

        @pl.when(n % chunks_per_seq == chunks_per_seq - 1)
        def _():
            o = jnp.sum(acc_ref[...].T, axis=0, keepdims=True)
            inv_l = jnp.sum(jnp.where(diag, 1.0 / l_ref[...], 0.0), axis=0, keepdims=True)
            os_ref[n // chunks_per_seq] = (o * inv_l).astype(BF16)

        cnt_ref[0] = n + 1

    def scores(j, h):
        start = pl.multiple_of(j * tq, tq)
        ks = k_ref[0, h, pl.ds(start, tq), :]
        st = lax.dot_general(ks, qs[h], (((1,), (1,)), ((), ())), preferred_element_type=F32)
        return st, v_ref[0, h, :, pl.ds(start, tq)]

    def attend_two_pass(j, carry, masked):
        new = []
        for h in heads:
            m, acc = carry[h]
            st, vs = scores(j, h)
            if masked:
                key = lax.broadcasted_iota(jnp.int32, (tq, tq), 0)
                qry = lax.broadcasted_iota(jnp.int32, (tq, tq), 1)
                st = jnp.where(key <= qry, st, -jnp.inf)
            m_new = jnp.maximum(m, jnp.max(st, axis=0, keepdims=True))
            pt = jnp.exp2(st - m_new).astype(BF16)
            new.append((m_new, jnp.exp2(m - m_new) * acc + _dot(vs, pt)))
        return tuple(new)

    def attend_one_pass(j, carry):
        new, excess = [], []
        for h in heads:
            m, acc = carry[h]
            st, vs = scores(j, h)
            bmax = jnp.max(st, axis=0, keepdims=True)
            pt = jnp.exp2(st - m).astype(BF16)
            m_new = jnp.maximum(m, bmax)
            new.append((m_new, (acc + _dot(vs, pt)) * jnp.exp2(m - m_new)))
            excess.append(jnp.max(bmax - m))
        risky = jnp.maximum(excess[0], excess[1]) > EXP2_HEADROOM
        return lax.cond(risky, lambda: attend_two_pass(j, carry, False), lambda: tuple(new))

    def step(j, carry, diagonal):
        decode_enter()
        decode_chunk()
        carry = attend_two_pass(j, carry, True) if diagonal else attend_one_pass(j, carry)
        decode_leave()
        return carry

    init = tuple((jnp.full((1, tq), -jnp.inf, F32), jnp.zeros((VT_ROWS, tq), F32)) for _ in heads)
    carry = step(i, init, True)
    carry = lax.fori_loop(0, i, lambda t, c: step(i - 1 - t, c, False), carry)
    outs = []
    for h in heads:
        acc = carry[h][1]
        outs.append(acc[:HEAD_DIM] / acc[HEAD_DIM:HEAD_DIM + 1])
    o_ref[0] = jnp.concatenate(outs, axis=0).T.astype(BF16)


def _attn_fused(qa, ka, va, batch, seq, tq, page_table, qb, k_new, v_new, lf_new, cache_k, cache_v, cache_logf):
    hp = N_HEADS // 2
    nq = seq // tq
    bd, n_pages = page_table.shape
    n_chunks = batch * hp * nq * (nq + 1) // 2
    assert n_chunks % bd == 0
    chunks_per_seq = n_chunks // bd
    pp = -(-n_pages // chunks_per_seq)
    n_pool = cache_k.shape[0]
    kc = cache_k.transpose(0, 2, 3, 1).reshape(n_pool, ATTN_W, PAGE_SIZE)
    vc = cache_v.transpose(0, 2, 3, 1).reshape(n_pool, ATTN_W, PAGE_SIZE)
    fc = cache_logf.transpose(0, 2, 1)
    whole = lambda shape: pl.BlockSpec(shape, lambda b, h, i, pt: (0,) * len(shape))
    hbm = pl.BlockSpec(memory_space=pl.ANY)
    grid_spec = pltpu.PrefetchScalarGridSpec(
        num_scalar_prefetch=1, grid=(batch, hp, nq),
        in_specs=[
            pl.BlockSpec((1, 2, tq, LANES), lambda b, h, i, pt: (b, h, i, 0)),
            pl.BlockSpec((1, 2, seq, LANES), lambda b, h, i, pt: (b, h, 0, 0)),
            pl.BlockSpec((1, 2, VT_ROWS, seq), lambda b, h, i, pt: (b, h, 0, 0)),
            whole((bd, 1, ATTN_W)), whole((bd, 1, ATTN_W)), whole((bd, 1, ATTN_W)), whole((bd, N_HEADS, 1)),
            hbm, hbm, hbm,
        ],
        out_specs=(pl.BlockSpec((1, tq, LANES), lambda b, h, i, pt: (b, i, h)), whole((bd, 1, ATTN_W))),
        scratch_shapes=[
            pltpu.VMEM((2, pp, ATTN_W, PAGE_SIZE), F32), pltpu.VMEM((2, pp, ATTN_W, PAGE_SIZE), F32),
            pltpu.VMEM((2, pp, N_HEADS, PAGE_SIZE), F32), pltpu.SemaphoreType.DMA((2, 3)),
            pltpu.VMEM((N_HEADS, 1), F32), pltpu.VMEM((N_HEADS, 1), F32), pltpu.VMEM((N_HEADS, 1), F32),
            pltpu.VMEM((ATTN_W, LANES), F32), pltpu.VMEM((ATTN_W, LANES), F32), pltpu.SMEM((1,), jnp.int32),
        ],
    )
    attn, attn_s = pl.pallas_call(
        functools.partial(_attn_fused_kernel, tq=tq, pp=pp, chunks_per_seq=chunks_per_seq, n_pages=n_pages,
                          n_chunks=n_chunks),
        grid_spec=grid_spec,
        out_shape=(jax.ShapeDtypeStruct((batch, seq, ATTN_W), BF16), jax.ShapeDtypeStruct((bd, 1, ATTN_W), BF16)),
        compiler_params=pltpu.CompilerParams(dimension_semantics=("arbitrary", "arbitrary", "arbitrary"),
                                             vmem_limit_bytes=VMEM_LIMIT),
        name="attn_fused",
    )(page_table, qa, ka, va, qb.reshape(bd, 1, ATTN_W), k_new.reshape(bd, 1, ATTN_W),
      v_new.reshape(bd, 1, ATTN_W), lf_new.reshape(bd, N_HEADS, 1), kc, vc, fc)
    return attn, attn_s.reshape(bd, ATTN_W)


def _mix_kernel(x_ref, a_ref, y_ref, sg_ref, wao_ref, wco_ref, wout_ref, gffn_ref, wr_ref, br_ref,
                x1_ref, h2_ref, gates_ref):
    d = x_ref.shape[-1]
    ao = _dot(a_ref[...], wao_ref[...])
    co = _dot(y_ref[...], wco_ref[...])
    merged = sg_ref[:, :d] * ao + sg_ref[:, d:] * co
    x1 = x_ref[...] + _dot(merged.astype(BF16), wout_ref[...])
    x1_ref[...] = x1
    h2 = _rms(x1, gffn_ref[...]).astype(BF16)
    h2_ref[...] = h2

    logits = _dot(h2, wr_ref[...]) + br_ref[...]
    lane = lax.broadcasted_iota(jnp.int32, logits.shape, 1)
    neg = -jnp.inf
    big = jnp.int32(4 * LANES)
    gl = jnp.where(lane < N_GROUPS, logits, neg)
    gmax = jnp.max(gl, axis=-1, keepdims=True)
    gval = 1.0 / jnp.sum(jnp.exp(gl - gmax), axis=-1, keepdims=True)
    gidx = jnp.min(jnp.where(gl == gmax, lane, big), axis=-1, keepdims=True)
    ex = lane - EXPERT_LANE0
    in_group = (ex >= gidx * EXP_PER_GROUP) & (ex < (gidx + 1) * EXP_PER_GROUP)
    el = jnp.where(in_group, logits, neg)
    v1 = jnp.max(el, axis=-1, keepdims=True)
    i1 = jnp.min(jnp.where(el == v1, lane, big), axis=-1, keepdims=True)
    el2 = jnp.where(lane == i1, neg, el)
    v2 = jnp.max(el2, axis=-1, keepdims=True)
    i2 = jnp.min(jnp.where(el2 == v2, lane, big), axis=-1, keepdims=True)
    e2 = jnp.exp(v2 - v1)
    w1 = gval / (1.0 + e2)
    w2 = gval * e2 / (1.0 + e2)
    gates_ref[...] = (jnp.where(lane == i1, w1, 0.0) + jnp.where(lane == i2, w2, 0.0)
                      + jnp.where(lane == gidx, 1.0, 0.0))


def _mix(x2d, attn, y, sg, mw, tm):
    n, d = x2d.shape
    row = lambda w: pl.BlockSpec((tm, w), lambda i: (i, 0))
    return pl.pallas_call(
        _mix_kernel, grid=(n // tm,),
        in_specs=[row(d), row(ATTN_W), row(CONV_CH), row(2 * d)] + [_const_spec(a.shape) for a in mw],
        out_specs=(row(d), row(d), row(LANES)),
        out_shape=(jax.ShapeDtypeStruct((n, d), F32), jax.ShapeDtypeStruct((n, d), BF16),
                   jax.ShapeDtypeStruct((n, LANES), F32)),
        compiler_params=pltpu.CompilerParams(dimension_semantics=("arbitrary",), vmem_limit_bytes=VMEM_LIMIT),
        name="mix",
    )(x2d, attn, y, sg, *mw)


def _moe_kernel(x1_ref, h2_ref, gates_ref, p_ref, wg_ref, wu_ref, wd_ref, gple_ref, wpg_ref, wpp_ref,
                o_ref, acc_ref, *, eb):
    jb = pl.program_id(1)

    @pl.when(jb == 0)
    def _():
        acc_ref[...] = jnp.zeros_like(acc_ref)

    h2 = h2_ref[...]
    gates = gates_ref[...]
    lane = lax.broadcasted_iota(jnp.int32, gates.shape, 1)
    for e in range(eb):
        gcol = jnp.sum(jnp.where(lane == EXPERT_LANE0 + jb * eb + e, gates, 0.0), axis=-1, keepdims=True)
        hg = _dot(h2, wg_ref[e])
        hu = _dot(h2, wu_ref[e])
        he = (hg * _sigmoid(hg)) * hu * gcol
        acc_ref[...] += _dot(he.astype(BF16), wd_ref[e])

    @pl.when(jb == pl.num_programs(1) - 1)
    def _():
        x2 = x1_ref[...] + acc_ref[...]
        g = _sigmoid(_dot(_rms(x2, gple_ref[...]).astype(BF16), wpg_ref[...]))
        o_ref[...] = x2 + g * _dot(p_ref[...].astype(BF16), wpp_ref[...])


def _moe_ple(x1, h2, gates, p2d, ew, pw, tm, eb):
    n, d = x1.shape
    wg, wu, wd = ew
    row = lambda w: pl.BlockSpec((tm, w), lambda i, j: (i, 0))
    return pl.pallas_call(
        functools.partial(_moe_kernel, eb=eb), grid=(n // tm, N_EXPERTS // eb),
        in_specs=[row(d), row(d), row(LANES), row(p2d.shape[1]),
                  pl.BlockSpec((eb, d, EXPERT_FF), lambda i, j: (j, 0, 0)),
                  pl.BlockSpec((eb, d, EXPERT_FF), lambda i, j: (j, 0, 0)),
                  pl.BlockSpec((eb, EXPERT_FF, d), lambda i, j: (j, 0, 0))]
                 + [_const_spec(a.shape) for a in pw],
        out_specs=row(d), out_shape=jax.ShapeDtypeStruct((n, d), F32),
        scratch_shapes=[pltpu.VMEM((tm, d), F32)],
        compiler_params=pltpu.CompilerParams(dimension_semantics=("arbitrary", "arbitrary"),
                                             vmem_limit_bytes=VMEM_LIMIT),
        name="moe_ple",
    )(x1, h2, gates, p2d, wg, wu, wd, *pw)


def _moe_group_kernel(h2_ref, gates_ref, tri_ref, wg_ref, wu_ref, wd_ref, o_ref, route_ref, gsplit_ref,
                      *, tm, rows):
    g = pl.program_id(1)

    @pl.when(g == 0)
    def _():
        o_ref[...] = jnp.zeros_like(o_ref)
        gates = gates_ref[...]
        onehot_t = gates.T[0:8]
        route_ref[0:8, :] = onehot_t
        route_ref[8:16, :] = _dot(onehot_t.astype(BF16), tri_ref[...])
        a, b, c = _split3(gates)
        gsplit_ref[...] = jnp.concatenate([a, b, c], axis=1)

    member = route_ref[pl.ds(g, 1), :]
    rank = route_ref[pl.ds(8 + g, 1), :].astype(jnp.int32)
    count = jnp.sum(member).astype(jnp.int32)
    slot = lax.broadcasted_iota(jnp.int32, (rows, tm), 0)
    lane = lax.broadcasted_iota(jnp.int32, (1, LANES), 1)

    def block(b, carry):
        pick = jnp.where(slot == rank - b * rows, member, 0.0).astype(BF16)
        hc = _dot(pick, h2_ref[...]).astype(BF16)
        gc3 = _dot(pick, gsplit_ref[...])
        gc = gc3[:, :LANES] + gc3[:, LANES:2 * LANES] + gc3[:, 2 * LANES:]
        yb = jnp.zeros((rows, o_ref.shape[-1]), F32)
        for e in range(EXP_PER_GROUP):
            col = jnp.sum(jnp.where(lane == EXPERT_LANE0 + g * EXP_PER_GROUP + e, gc, 0.0), axis=-1, keepdims=True)
            hg = _dot(hc, wg_ref[e])
            hu = _dot(hc, wu_ref[e])
            he = (hg * _sigmoid(hg)) * hu * col
            yb = yb + _dot(he.astype(BF16), wd_ref[e])
        o_ref[...] += lax.dot_general(pick, yb.astype(BF16), (((0,), (0,)), ((), ())),
                                      preferred_element_type=F32)
        return carry

    lax.fori_loop(0, (count + rows - 1) // rows, block, 0)


def _moe_group(h2, gates, ew, tm, rows):
    n, d = h2.shape
    wg, wu, wd = ew
    tri = (jnp.arange(tm)[:, None] < jnp.arange(tm)[None, :]).astype(BF16)
    row = lambda w: pl.BlockSpec((tm, w), lambda i, g: (i, 0))
    grp = lambda a, b: pl.BlockSpec((EXP_PER_GROUP, a, b), lambda i, g: (g, 0, 0))
    return pl.pallas_call(
        functools.partial(_moe_group_kernel, tm=tm, rows=rows), grid=(n // tm, N_GROUPS),
        in_specs=[row(d), row(LANES), _const_spec(tri.shape), grp(d, EXPERT_FF), grp(d, EXPERT_FF),
                  grp(EXPERT_FF, d)],
        out_specs=row(d), out_shape=jax.ShapeDtypeStruct((n, d), F32),
        scratch_shapes=[pltpu.VMEM((16, tm), F32), pltpu.VMEM((tm, 3 * LANES), BF16)],
        compiler_params=pltpu.CompilerParams(dimension_semantics=("arbitrary", "arbitrary"),
                                             vmem_limit_bytes=VMEM_LIMIT),
        name="moe_group",
    )(h2, gates, tri, wg, wu, wd)


def _ple_kernel(x1_ref, y_ref, p_ref, gple_ref, wpg_ref, wpp_ref, o_ref):
    x2 = x1_ref[...] + y_ref[...]
    g = _sigmoid(_dot(_rms(x2, gple_ref[...]).astype(BF16), wpg_ref[...]))
    o_ref[...] = x2 + g * _dot(p_ref[...].astype(BF16), wpp_ref[...])


def _ple(x1, ymoe, p2d, pw, tm):
    n, d = x1.shape
    row = lambda w: pl.BlockSpec((tm, w), lambda i: (i, 0))
    return pl.pallas_call(
        _ple_kernel, grid=(n // tm,),
        in_specs=[row(d), row(d), row(p2d.shape[1])] + [_const_spec(a.shape) for a in pw],
        out_specs=row(d), out_shape=jax.ShapeDtypeStruct((n, d), F32),
        compiler_params=pltpu.CompilerParams(dimension_semantics=("arbitrary",), vmem_limit_bytes=VMEM_LIMIT),
        name="ple",
    )(x1, ymoe, p2d, *pw)


def _tile(n, pref):
    return pref if n % pref == 0 else n


def kernel(x_prompt, x_sample, cache_k, cache_v, cache_logf, state_conv, page_table, p_prompt, p_sample, norm_mix_g, w_in, b_forget, q_norm_g, k_norm_g, w_attn_o, conv_dw_w, conv_dw_b, conv_ln_g, conv_ln_b, w_conv_o, w_out, norm_ffn_g, w_router_group, b_router_group, w_router_expert, b_router_expert, w_exp_gate, w_exp_up, w_exp_down, norm_ple_g, w_ple_gate, w_ple_proj):
    depth = w_in.shape[0]
    assert depth == 1
    li = 0
    batch, seq, d = x_prompt.shape
    bd, dec_seq, _ = x_sample.shape
    assert dec_seq == 1
    n_pages = page_table.shape[1]

    pw = _prep_proj_weights(norm_mix_g[li], w_in[li], b_forget[li], q_norm_g[li], k_norm_g[li])
    conv_w = (conv_dw_w[li], conv_dw_b[li].reshape(1, CONV_CH), conv_ln_g[li].reshape(1, CONV_CH),
              conv_ln_b[li].reshape(1, CONV_CH))
    wr = jnp.zeros((d, LANES), F32)
    wr = wr.at[:, :N_GROUPS].set(w_router_group[li]).at[:, EXPERT_LANE0:EXPERT_LANE0 + N_EXPERTS].set(
        w_router_expert[li]).astype(BF16)
    br = jnp.zeros((1, LANES), F32)
    br = br.at[0, :N_GROUPS].set(b_router_group[li]).at[0, EXPERT_LANE0:EXPERT_LANE0 + N_EXPERTS].set(
        b_router_expert[li])
    mw = (w_attn_o[li].astype(BF16), w_conv_o[li].astype(BF16), w_out[li].astype(BF16),
          norm_ffn_g[li].reshape(1, d), wr, br)
    ew = (w_exp_gate[li].astype(BF16), w_exp_up[li].astype(BF16), w_exp_down[li].astype(BF16))
    plew = (norm_ple_g[li].reshape(1, d), w_ple_gate[li].astype(BF16), w_ple_proj[li].astype(BF16))

    xp = x_prompt.reshape(batch * seq, d)
    tm = _tile(seq, 512)
    qa, ka, va, kt, vt, lf, y, sg, utail = _proj_prompt(xp, pw, conv_w, batch, seq, tm)
    xs = x_sample.reshape(bd, d)
    state_t = state_conv[li].transpose(1, 0, 2)
    qs, ks, vs, lfs, us, ys, sgs = _proj_sample(xs, pw, state_t, conv_w)
    attn, attn_s = _attn_fused(qa, ka, va, batch, seq, _tile(seq, 1024), page_table, qs, ks, vs, lfs,
                               cache_k[li], cache_v[li], cache_logf[li])
    attn = attn.reshape(batch * seq, ATTN_W)

    x1, h2, gates = _mix(xp, attn, y, sg, mw, tm)
    ymoe = _moe_group(h2, gates, ew, _tile(batch * seq, MOE_TILE), MOE_BLOCK_ROWS)
    yp = _ple(x1, ymoe, p_prompt[li].reshape(batch * seq, -1), plew, tm)
    y_prompt = yp.reshape(batch, seq, d)
    to_rows = lambda t: t.reshape(batch, N_HEADS, HEAD_DIM, seq).transpose(0, 3, 1, 2)[None]
    new_k_prompt = to_rows(kt)
    new_v_prompt = to_rows(vt)
    new_logf_prompt = lf.reshape(1, batch, seq, N_HEADS)
    new_conv_prompt = utail[:, CONV_HALO - (CONV_K - 1):, :].reshape(1, batch, CONV_K - 1, CONV_CH)

    x1s, h2s, gates_s = _mix(xs, attn_s, ys, sgs, mw, bd)
    ysm = _moe_ple(x1s, h2s, gates_s, p_sample[li].reshape(bd, -1), ew, plew, bd, 4)
    y_sample = ysm.reshape(bd, 1, d)
    new_k_sample = ks.reshape(1, bd, 1, N_HEADS, HEAD_DIM)
    new_v_sample = vs.reshape(1, bd, 1, N_HEADS, HEAD_DIM)
    new_logf_sample = lfs.reshape(1, bd, 1, N_HEADS)
    new_conv_sample = jnp.concatenate([state_conv[li][:, 1:, :], us[:, None, :]], axis=1)[None]

    return (y_prompt, y_sample, new_k_prompt, new_v_prompt, new_logf_prompt, new_conv_prompt,
            new_k_sample, new_v_sample, new_logf_sample, new_conv_sample)
```

```python
import functools

import jax
import jax.numpy as jnp
from jax import lax
from jax.experimental import pallas as pl
from jax.experimental.pallas import tpu as pltpu

F32 = jnp.float32
BF16 = jnp.bfloat16

N_HEADS = 8
HEAD_DIM = 64
ATTN_W = N_HEADS * HEAD_DIM
CONV_CH = 512
CONV_K = 31
N_GROUPS = 4
EXP_PER_GROUP = 8
N_EXPERTS = N_GROUPS * EXP_PER_GROUP
EXPERT_FF = 256
PAGE_SIZE = 128
EPS = 1e-6

LANES = 128
SUBLANES = 8
CONV_HALO = 32
VMEM_LIMIT = 56 * 1024 * 1024
EXPERT_LANE0 = 32
LOG2E = 1.4426950408889634
AUG_K0 = 64
AUG_Q0 = 88
VT_ROWS = HEAD_DIM + 16
EXP2_HEADROOM = 100.0
MOE_TILE = 1024
MOE_BLOCK_ROWS = 320


def _const_spec(shape):
    nd = len(shape)
    return pl.BlockSpec(shape, lambda *_: (0,) * nd, pipeline_mode=pl.Buffered(1))


def _sigmoid(x):
    return 0.5 * jnp.tanh(0.5 * x) + 0.5


def _log_sigmoid(x):
    return -(jnp.maximum(-x, 0.0) + jnp.log1p(jnp.exp(-jnp.abs(x))))


def _rms(x, g):
    return x * lax.rsqrt(jnp.mean(x * x, axis=-1, keepdims=True) + EPS) * g


def _dot(a, b):
    return jnp.dot(a, b, preferred_element_type=F32)


def _project(x, gmix, wqkv, wf, wglu, wgate, bf, qg, kg, gsum):
    hb = _rms(x, gmix).astype(BF16)
    zqkv = _dot(hb, wqkv)
    zq = zqkv[:, :ATTN_W]
    zk = zqkv[:, ATTN_W:2 * ATTN_W]
    zv = zqkv[:, 2 * ATTN_W:]

    def head_norm(z, g):
        ss = _dot((z * z).astype(BF16), gsum)
        return z * lax.rsqrt(ss * (1.0 / HEAD_DIM) + EPS) * g

    q = head_norm(zq, qg) * (HEAD_DIM ** -0.5 * LOG2E)
    k = head_norm(zk, kg)
    zf = _dot(hb, wf) + bf
    lane = lax.broadcasted_iota(jnp.int32, zf.shape, 1)
    logf = jnp.where(lane < N_HEADS, _log_sigmoid(zf), 0.0)
    zglu = _dot(hb, wglu)
    u = zglu[:, :CONV_CH] * _sigmoid(zglu[:, CONV_CH:])
    sg = _sigmoid(_dot(hb, wgate))
    return q, k, zv, logf, u, sg


def _ln_silu(y, g, b):
    mu = jnp.mean(y, axis=-1, keepdims=True)
    d = y - mu
    var = jnp.mean(d * d, axis=-1, keepdims=True)
    z = d * lax.rsqrt(var + EPS) * g + b
    return z * _sigmoid(z)


def _split3(x):
    a = x.astype(BF16)
    r = x - a.astype(F32)
    b = r.astype(BF16)
    c = (r - b.astype(F32)).astype(BF16)
    return a, b, c


def _proj_prompt_kernel(x_ref, gmix_ref, wqkv_ref, wf_ref, wglu_ref, wgate_ref, bf_ref, qg_ref, kg_ref,
                        gsum_ref, tri_ref, wdw_ref, bdw_ref, lng_ref, lnb_ref,
                        qa_ref, ka_ref, va_ref, kt_ref, vt_ref, lf_ref, y_ref, sg_ref, utail_ref,
                        ext_ref, carry_ref, shift_ref, *, tiles_per_seq, tm):
    i = pl.program_id(0)

    @pl.when(i % tiles_per_seq == 0)
    def _():
        ext_ref[0:CONV_HALO, :] = jnp.zeros((CONV_HALO, CONV_CH), F32)
        carry_ref[...] = jnp.zeros_like(carry_ref)

    q, k, v, logf, u, sg = _project(x_ref[...], gmix_ref[...], wqkv_ref[...], wf_ref[...], wglu_ref[...],
                                    wgate_ref[...], bf_ref[...], qg_ref[...], kg_ref[...], gsum_ref[...])
    sg_ref[...] = sg
    lf_ref[...] = logf[:, :N_HEADS]

    tri = tri_ref[...]
    c3 = _dot(tri, jnp.concatenate(_split3(logf), axis=1))
    c = c3[:, :LANES] + c3[:, LANES:2 * LANES] + c3[:, 2 * LANES:] + carry_ref[0:1, :]
    carry_ref[0:1, :] = c[tm - 1:tm, :]

    ca, cb, cc = (t.astype(F32) for t in _split3(c * LOG2E))
    tail_k = -(pltpu.roll(ca, AUG_K0, 1) + pltpu.roll(cb, AUG_K0 + 8, 1) + pltpu.roll(cc, AUG_K0 + 16, 1))
    tail_q = pltpu.roll(ca, AUG_Q0, 1) + pltpu.roll(cb, AUG_Q0 + 8, 1) + pltpu.roll(cc, AUG_Q0 + 16, 1)
    lane = lax.broadcasted_iota(jnp.int32, (1, LANES), 1)
    low = lane < HEAD_DIM
    for h in range(N_HEADS):
        blk = slice((h // 2) * LANES, (h // 2 + 1) * LANES)
        qh, kh = q[:, blk], k[:, blk]
        if h % 2:
            qh, kh = pltpu.roll(qh, HEAD_DIM, 1), pltpu.roll(kh, HEAD_DIM, 1)
        pick_k = ((lane == AUG_K0 + h) | (lane == AUG_K0 + 8 + h) | (lane == AUG_K0 + 16 + h)).astype(F32)
        pick_q = ((lane == AUG_Q0 + h) | (lane == AUG_Q0 + 8 + h) | (lane == AUG_Q0 + 16 + h)).astype(F32)
        qa_ref[0, h] = jnp.where(low, qh, tail_q + pick_k).astype(BF16)
        ka_ref[0, h] = jnp.where(low, kh, tail_k + pick_q).astype(BF16)

    kt_ref[0] = k.T
    vt = v.T
    vt_ref[0] = vt
    row = lax.broadcasted_iota(jnp.int32, (VT_ROWS - HEAD_DIM, tm), 0)
    ones_row = jnp.where(row == 0, 1.0, 0.0)
    for h in range(N_HEADS):
        va_ref[0, h] = jnp.concatenate([vt[h * HEAD_DIM:(h + 1) * HEAD_DIM], ones_row], axis=0).astype(BF16)

    ext_ref[CONV_HALO:CONV_HALO + tm, :] = u
    off = CONV_HALO - (CONV_K - 1)
    span = tm + CONV_HALO - SUBLANES
    for s in range(1, SUBLANES):
        shift_ref[s - 1] = ext_ref[s:s + span, :]
    rc = min(tm, 128)
    for r0 in range(0, tm, rc):
        for l0 in range(0, CONV_CH, LANES):
            acc = jnp.zeros((rc, LANES), F32)
            for j in range(CONV_K):
                s = (off + j) % SUBLANES
                a0 = off + j - s + r0
                src = ext_ref if s == 0 else shift_ref.at[s - 1]
                acc = acc + src[a0:a0 + rc, l0:l0 + LANES] * wdw_ref[j:j + 1, l0:l0 + LANES]
            ext_ref[CONV_HALO + tm + r0:CONV_HALO + tm + r0 + rc, l0:l0 + LANES] = acc
    yc = ext_ref[CONV_HALO + tm:CONV_HALO + 2 * tm, :] + bdw_ref[...]
    y_ref[...] = _ln_silu(yc, lng_ref[...], lnb_ref[...]).astype(BF16)
    tail = u[tm - CONV_HALO:, :]
    ext_ref[0:CONV_HALO, :] = tail
    utail_ref[0] = tail


def _proj_sample_kernel(x_ref, gmix_ref, wqkv_ref, wf_ref, wglu_ref, wgate_ref, bf_ref, qg_ref, kg_ref,
                        gsum_ref, st_ref, wdw_ref, bdw_ref, lng_ref, lnb_ref,
                        qb_ref, k32_ref, v32_ref, lf_ref, u_ref, y_ref, sg_ref):
    q, k, v, logf, u, sg = _project(x_ref[...], gmix_ref[...], wqkv_ref[...], wf_ref[...], wglu_ref[...],
                                    wgate_ref[...], bf_ref[...], qg_ref[...], kg_ref[...], gsum_ref[...])
    qb_ref[...] = q.astype(BF16)
    k32_ref[...] = k
    v32_ref[...] = v
    sg_ref[...] = sg
    lf_ref[...] = logf[:, :N_HEADS]
    u_ref[...] = u
    acc = u * wdw_ref[CONV_K - 1:CONV_K, :]
    for j in range(CONV_K - 1):
        acc = acc + st_ref[j] * wdw_ref[j:j + 1, :]
    y_ref[...] = _ln_silu(acc + bdw_ref[...], lng_ref[...], lnb_ref[...]).astype(BF16)


def _prep_proj_weights(norm_mix_g, w_in, b_forget, q_norm_g, k_norm_g):
    d = w_in.shape[0]
    o_f = 3 * ATTN_W
    o_glu = o_f + N_HEADS
    o_gate = o_glu + 2 * CONV_CH
    wqkv = w_in[:, :o_f].astype(BF16)
    wf = jnp.pad(w_in[:, o_f:o_glu], ((0, 0), (0, LANES - N_HEADS))).astype(BF16)
    wglu = w_in[:, o_glu:o_gate].astype(BF16)
    wgate = w_in[:, o_gate:].astype(BF16)
    bf = jnp.pad(b_forget, (0, LANES - N_HEADS)).reshape(1, LANES)
    qg = jnp.tile(q_norm_g, N_HEADS).reshape(1, ATTN_W)
    kg = jnp.tile(k_norm_g, N_HEADS).reshape(1, ATTN_W)
    hid = jnp.arange(ATTN_W) // HEAD_DIM
    gsum = (hid[:, None] == hid[None, :]).astype(BF16)
    return (norm_mix_g.reshape(1, d), wqkv, wf, wglu, wgate, bf, qg, kg, gsum)


def _proj_prompt(x2d, pw, conv_w, batch, seq, tm):
    n, d = x2d.shape
    tps = seq // tm
    tri = (jnp.arange(tm)[:, None] >= jnp.arange(tm)[None, :]).astype(BF16)
    wdw, bdw, lng, lnb = conv_w
    row = lambda w: pl.BlockSpec((tm, w), lambda i: (i, 0))
    in_specs = ([row(d)] + [_const_spec(a.shape) for a in pw] + [_const_spec(tri.shape)]
                + [_const_spec(a.shape) for a in conv_w])
    head_rows = jax.ShapeDtypeStruct((batch, N_HEADS, seq, LANES), BF16)
    head_rows_spec = pl.BlockSpec((1, N_HEADS, tm, LANES), lambda i: (i // tps, 0, i % tps, 0))
    seq_minor = jax.ShapeDtypeStruct((batch, ATTN_W, seq), F32)
    seq_minor_spec = pl.BlockSpec((1, ATTN_W, tm), lambda i: (i // tps, 0, i % tps))
    out_shape = (
        head_rows, head_rows, jax.ShapeDtypeStruct((batch, N_HEADS, VT_ROWS, seq), BF16), seq_minor, seq_minor,
        jax.ShapeDtypeStruct((n, N_HEADS), F32), jax.ShapeDtypeStruct((n, CONV_CH), BF16),
        jax.ShapeDtypeStruct((n, 2 * d), F32), jax.ShapeDtypeStruct((batch, CONV_HALO, CONV_CH), F32),
    )
    out_specs = (head_rows_spec, head_rows_spec,
                 pl.BlockSpec((1, N_HEADS, VT_ROWS, tm), lambda i: (i // tps, 0, 0, i % tps)),
                 seq_minor_spec, seq_minor_spec, row(N_HEADS),
                 row(CONV_CH), row(2 * d), pl.BlockSpec((1, CONV_HALO, CONV_CH), lambda i: (i // tps, 0, 0)))
    return pl.pallas_call(
        functools.partial(_proj_prompt_kernel, tiles_per_seq=tps, tm=tm),
        grid=(n // tm,), in_specs=in_specs, out_specs=out_specs, out_shape=out_shape,
        scratch_shapes=[pltpu.VMEM((CONV_HALO + 2 * tm, CONV_CH), F32), pltpu.VMEM((SUBLANES, LANES), F32),
                        pltpu.VMEM((SUBLANES - 1, tm + CONV_HALO - SUBLANES, CONV_CH), F32)],
        compiler_params=pltpu.CompilerParams(dimension_semantics=("arbitrary",), vmem_limit_bytes=VMEM_LIMIT),
        name="proj_prompt",
    )(x2d, *pw, tri, wdw, bdw, lng, lnb)


def _proj_sample(x2d, pw, state_t, conv_w):
    n, d = x2d.shape
    args = (x2d,) + tuple(pw) + (state_t,) + tuple(conv_w)
    full = lambda a: pl.BlockSpec(a.shape, lambda i, nd=a.ndim: (0,) * nd)
    out_shape = (
        jax.ShapeDtypeStruct((n, ATTN_W), BF16), jax.ShapeDtypeStruct((n, ATTN_W), F32),
        jax.ShapeDtypeStruct((n, ATTN_W), F32), jax.ShapeDtypeStruct((n, N_HEADS), F32),
        jax.ShapeDtypeStruct((n, CONV_CH), F32), jax.ShapeDtypeStruct((n, CONV_CH), BF16),
        jax.ShapeDtypeStruct((n, 2 * d), F32),
    )
    return pl.pallas_call(
        _proj_sample_kernel, grid=(1,), in_specs=[full(a) for a in args],
        out_specs=tuple(full(s) for s in out_shape), out_shape=out_shape,
        compiler_params=pltpu.CompilerParams(dimension_semantics=("arbitrary",), vmem_limit_bytes=VMEM_LIMIT),
        name="proj_sample",
    )(*args)


def _attn_fused_kernel(pt_ref, q_ref, k_ref, v_ref, qs_ref, kn_ref, vn_ref, lfn_ref, kc_hbm, vc_hbm, fc_hbm,
                       o_ref, os_ref, kbuf, vbuf, fbuf, sems, m_ref, l_ref, r_ref, acc_ref, qrep_ref, cnt_ref,
                       *, tq, pp, chunks_per_seq, n_pages, n_chunks):
    b_id, h_id, i = pl.program_id(0), pl.program_id(1), pl.program_id(2)
    heads = range(2)
    qs = [q_ref[0, h] for h in heads]

    def page_copies(n, slot):
        seq = n // chunks_per_seq
        c = n % chunks_per_seq
        copies = []
        for t in range(pp):
            idx = n_pages - 1 - jnp.minimum(c * pp + t, n_pages - 1)
            page = pt_ref[seq, idx]
            copies.append(pltpu.make_async_copy(kc_hbm.at[page], kbuf.at[slot, t], sems.at[slot, 0]))
            copies.append(pltpu.make_async_copy(vc_hbm.at[page], vbuf.at[slot, t], sems.at[slot, 1]))
            copies.append(pltpu.make_async_copy(fc_hbm.at[page], fbuf.at[slot, t], sems.at[slot, 2]))
        return copies

    @pl.when((b_id == 0) & (h_id == 0) & (i == 0))
    def _():
        cnt_ref[0] = 0
        for cp in page_copies(0, 0):
            cp.start()

    hrow = lax.broadcasted_iota(jnp.int32, (N_HEADS, ATTN_W), 0)
    hlane = lax.broadcasted_iota(jnp.int32, (N_HEADS, ATTN_W), 1) // HEAD_DIM
    diag = hrow == hlane
    srow = lax.broadcasted_iota(jnp.int32, (PAGE_SIZE, 2 * PAGE_SIZE), 0)
    scol = lax.broadcasted_iota(jnp.int32, (PAGE_SIZE, 2 * PAGE_SIZE), 1)
    suffix = jnp.where((srow > scol) | (scol >= PAGE_SIZE), 1.0, 0.0).astype(F32)

    def head_rows(h):
        return slice(h * HEAD_DIM, (h + 1) * HEAD_DIM)

    def decode_enter():
        n = cnt_ref[0]
        slot = n % 2
        seq = n // chunks_per_seq
        for cp in page_copies(n, slot):
            cp.wait()

        @pl.when(n + 1 < n_chunks)
        def _():
            for cp in page_copies(n + 1, 1 - slot):
                cp.start()

        @pl.when(n % chunks_per_seq == 0)
        def _():
            qrow = qs_ref[seq].astype(F32)
            m_ref[...] = jnp.sum(jnp.where(diag, qrow, 0.0) * kn_ref[seq], axis=-1, keepdims=True)
            l_ref[...] = jnp.ones_like(l_ref)
            r_ref[...] = lfn_ref[seq]
            qrep_ref[...] = jnp.broadcast_to(qrow, (LANES, ATTN_W)).T
            vrep = jnp.broadcast_to(vn_ref[seq], (LANES, ATTN_W)).T
            lane0 = lax.broadcasted_iota(jnp.int32, (ATTN_W, LANES), 1) == 0
            acc_ref[...] = jnp.where(lane0, vrep, 0.0)

    def decode_chunk():
        n = cnt_ref[0]
        slot = n % 2
        c = n % chunks_per_seq
        lf_all = jnp.concatenate([fbuf[slot, t] for t in range(pp)], axis=0)
        sx = jnp.dot(lf_all, suffix, precision=lax.Precision.HIGHEST, preferred_element_type=F32)
        r = r_ref[...]
        parts = []
        for t in range(pp):
            valid = c * pp + t < n_pages
            bias = r + sx[t * N_HEADS:(t + 1) * N_HEADS, :PAGE_SIZE]
            r = r + jnp.where(valid, sx[t * N_HEADS:(t + 1) * N_HEADS, PAGE_SIZE:PAGE_SIZE + 1], 0.0)
            rows = []
            for h in range(N_HEADS):
                prod = kbuf[slot, t, head_rows(h), :] * qrep_ref[head_rows(h), :]
                rows.append(jnp.sum(prod, axis=0, keepdims=True))
            parts.append(jnp.where(valid, jnp.concatenate(rows, axis=0) + bias * LOG2E, -jnp.inf))
        r_ref[...] = r
        s = jnp.concatenate(parts, axis=1)
        m = m_ref[...]
        m_new = jnp.maximum(m, jnp.max(s, axis=-1, keepdims=True))
        alpha = jnp.exp2(m - m_new)
        p = jnp.exp2(s - m_new)
        l_ref[...] = alpha * l_ref[...] + jnp.sum(p, axis=-1, keepdims=True)
        m_ref[...] = m_new
        for h in range(N_HEADS):
            pv = jnp.zeros((HEAD_DIM, PAGE_SIZE), F32)
            for t in range(pp):
                pv = pv + vbuf[slot, t, head_rows(h), :] * p[h:h + 1, t * PAGE_SIZE:(t + 1) * PAGE_SIZE]
            acc_ref[head_rows(h), :] = alpha[h:h + 1, :] * acc_ref[head_rows(h), :] + pv

    def decode_leave():
        n = cnt_ref[0]

        @pl.when(n % chunks_per_seq == chunks_per_seq - 1)
        def _():
            o = jnp.sum(acc_ref[...].T, axis=0, keepdims=True)
            inv_l = jnp.sum(jnp.where(diag, 1.0 / l_ref[...], 0.0), axis=0, keepdims=True)
            os_ref[n // chunks_per_seq] = (o * inv_l).astype(BF16)

        cnt_ref[0] = n + 1

    def scores(j, h):
        start = pl.multiple_of(j * tq, tq)
        ks = k_ref[0, h, pl.ds(start, tq), :]
        st = lax.dot_general(ks, qs[h], (((1,), (1,)), ((), ())), preferred_element_type=F32)
        return st, v_ref[0, h, :, pl.ds(start, tq)]

    def attend_two_pass(j, carry, masked):
        new = []
        for h in heads:
            m, acc = carry[h]
            st, vs = scores(j, h)
            if masked:
                key = lax.broadcasted_iota(jnp.int32, (tq, tq), 0)
                qry = lax.broadcasted_iota(jnp.int32, (tq, tq), 1)
                st = jnp.where(key <= qry, st, -jnp.inf)
            m_new = jnp.maximum(m, jnp.max(st, axis=0, keepdims=True))
            pt = jnp.exp2(st - m_new).astype(BF16)
            new.append((m_new, jnp.exp2(m - m_new) * acc + _dot(vs, pt)))
        return tuple(new)

    def attend_one_pass(j, carry):
        new, excess = [], []
        for h in heads:
            m, acc = carry[h]
            st, vs = scores(j, h)
            bmax = jnp.max(st, axis=0, keepdims=True)
            pt = jnp.exp2(st - m).astype(BF16)
            m_new = jnp.maximum(m, bmax)
            new.append((m_new, (acc + _dot(vs, pt)) * jnp.exp2(m - m_new)))
            excess.append(jnp.max(bmax - m))
        risky = jnp.maximum(excess[0], excess[1]) > EXP2_HEADROOM
        return lax.cond(risky, lambda: attend_two_pass(j, carry, False), lambda: tuple(new))

    def step(j, carry, diagonal):
        decode_enter()
        decode_chunk()
        carry = attend_two_pass(j, carry, True) if diagonal else attend_one_pass(j, carry)
        decode_leave()
        return carry

    init = tuple((jnp.full((1, tq), -jnp.inf, F32), jnp.zeros((VT_ROWS, tq), F32)) for _ in heads)
    carry = step(i, init, True)
    carry = lax.fori_loop(0, i, lambda t, c: step(i - 1 - t, c, False), carry)
    outs = []
    for h in heads:
        acc = carry[h][1]
        outs.append(acc[:HEAD_DIM] / acc[HEAD_DIM:HEAD_DIM + 1])
    o_ref[0] = jnp.concatenate(outs, axis=0).T.astype(BF16)


def _attn_fused(qa, ka, va, batch, seq, tq, page_table, qb, k_new, v_new, lf_new, cache_k, cache_v, cache_logf):
    hp = N_HEADS // 2
    nq = seq // tq
    bd, n_pages = page_table.shape
    n_chunks = batch * hp * nq * (nq + 1) // 2
    assert n_chunks % bd == 0
    chunks_per_seq = n_chunks // bd
    pp = -(-n_pages // chunks_per_seq)
    n_pool = cache_k.shape[0]
    kc = cache_k.transpose(0, 2, 3, 1).reshape(n_pool, ATTN_W, PAGE_SIZE)
    vc = cache_v.transpose(0, 2, 3, 1).reshape(n_pool, ATTN_W, PAGE_SIZE)
    fc = cache_logf.transpose(0, 2, 1)
    whole = lambda shape: pl.BlockSpec(shape, lambda b, h, i, pt: (0,) * len(shape))
    hbm = pl.BlockSpec(memory_space=pl.ANY)
    grid_spec = pltpu.PrefetchScalarGridSpec(
        num_scalar_prefetch=1, grid=(batch, hp, nq),
        in_specs=[
            pl.BlockSpec((1, 2, tq, LANES), lambda b, h, i, pt: (b, h, i, 0)),
            pl.BlockSpec((1, 2, seq, LANES), lambda b, h, i, pt: (b, h, 0, 0)),
            pl.BlockSpec((1, 2, VT_ROWS, seq), lambda b, h, i, pt: (b, h, 0, 0)),
            whole((bd, 1, ATTN_W)), whole((bd, 1, ATTN_W)), whole((bd, 1, ATTN_W)), whole((bd, N_HEADS, 1)),
            hbm, hbm, hbm,
        ],
        out_specs=(pl.BlockSpec((1, tq, LANES), lambda b, h, i, pt: (b, i, h)), whole((bd, 1, ATTN_W))),
        scratch_shapes=[
            pltpu.VMEM((2, pp, ATTN_W, PAGE_SIZE), F32), pltpu.VMEM((2, pp, ATTN_W, PAGE_SIZE), F32),
            pltpu.VMEM((2, pp, N_HEADS, PAGE_SIZE), F32), pltpu.SemaphoreType.DMA((2, 3)),
            pltpu.VMEM((N_HEADS, 1), F32), pltpu.VMEM((N_HEADS, 1), F32), pltpu.VMEM((N_HEADS, 1), F32),
            pltpu.VMEM((ATTN_W, LANES), F32), pltpu.VMEM((ATTN_W, LANES), F32), pltpu.SMEM((1,), jnp.int32),
        ],
    )
    attn, attn_s = pl.pallas_call(
        functools.partial(_attn_fused_kernel, tq=tq, pp=pp, chunks_per_seq=chunks_per_seq, n_pages=n_pages,
                          n_chunks=n_chunks),
        grid_spec=grid_spec,
        out_shape=(jax.ShapeDtypeStruct((batch, seq, ATTN_W), BF16), jax.ShapeDtypeStruct((bd, 1, ATTN_W), BF16)),
        compiler_params=pltpu.CompilerParams(dimension_semantics=("arbitrary", "arbitrary", "arbitrary"),
                                             vmem_limit_bytes=VMEM_LIMIT),
        name="attn_fused",
    )(page_table, qa, ka, va, qb.reshape(bd, 1, ATTN_W), k_new.reshape(bd, 1, ATTN_W),
      v_new.reshape(bd, 1, ATTN_W), lf_new.reshape(bd, N_HEADS, 1), kc, vc, fc)
    return attn, attn_s.reshape(bd, ATTN_W)


def _mix_kernel(x_ref, a_ref, y_ref, sg_ref, wao_ref, wco_ref, wout_ref, gffn_ref, wr_ref, br_ref,
                x1_ref, h2_ref, gates_ref):
    d = x_ref.shape[-1]
    ao = _dot(a_ref[...], wao_ref[...])
    co = _dot(y_ref[...], wco_ref[...])
    merged = sg_ref[:, :d] * ao + sg_ref[:, d:] * co
    x1 = x_ref[...] + _dot(merged.astype(BF16), wout_ref[...])
    x1_ref[...] = x1
    h2 = _rms(x1, gffn_ref[...]).astype(BF16)
    h2_ref[...] = h2

    logits = _dot(h2, wr_ref[...]) + br_ref[...]
    lane = lax.broadcasted_iota(jnp.int32, logits.shape, 1)
    neg = -jnp.inf
    big = jnp.int32(4 * LANES)
    gl = jnp.where(lane < N_GROUPS, logits, neg)
    gmax = jnp.max(gl, axis=-1, keepdims=True)
    gval = 1.0 / jnp.sum(jnp.exp(gl - gmax), axis=-1, keepdims=True)
    gidx = jnp.min(jnp.where(gl == gmax, lane, big), axis=-1, keepdims=True)
    ex = lane - EXPERT_LANE0
    in_group = (ex >= gidx * EXP_PER_GROUP) & (ex < (gidx + 1) * EXP_PER_GROUP)
    el = jnp.where(in_group, logits, neg)
    v1 = jnp.max(el, axis=-1, keepdims=True)
    i1 = jnp.min(jnp.where(el == v1, lane, big), axis=-1, keepdims=True)
    el2 = jnp.where(lane == i1, neg, el)
    v2 = jnp.max(el2, axis=-1, keepdims=True)
    i2 = jnp.min(jnp.where(el2 == v2, lane, big), axis=-1, keepdims=True)
    e2 = jnp.exp(v2 - v1)
    w1 = gval / (1.0 + e2)
    w2 = gval * e2 / (1.0 + e2)
    gates_ref[...] = (jnp.where(lane == i1, w1, 0.0) + jnp.where(lane == i2, w2, 0.0)
                      + jnp.where(lane == gidx, 1.0, 0.0))


def _mix(x2d, attn, y, sg, mw, tm):
    n, d = x2d.shape
    row = lambda w: pl.BlockSpec((tm, w), lambda i: (i, 0))
    return pl.pallas_call(
        _mix_kernel, grid=(n // tm,),
        in_specs=[row(d), row(ATTN_W), row(CONV_CH), row(2 * d)] + [_const_spec(a.shape) for a in mw],
        out_specs=(row(d), row(d), row(LANES)),
        out_shape=(jax.ShapeDtypeStruct((n, d), F32), jax.ShapeDtypeStruct((n, d), BF16),
                   jax.ShapeDtypeStruct((n, LANES), F32)),
        compiler_params=pltpu.CompilerParams(dimension_semantics=("arbitrary",), vmem_limit_bytes=VMEM_LIMIT),
        name="mix",
    )(x2d, attn, y, sg, *mw)


def _moe_kernel(x1_ref, h2_ref, gates_ref, p_ref, wg_ref, wu_ref, wd_ref, gple_ref, wpg_ref, wpp_ref,
                o_ref, acc_ref, *, eb):
    jb = pl.program_id(1)

    @pl.when(jb == 0)
    def _():
        acc_ref[...] = jnp.zeros_like(acc_ref)

    h2 = h2_ref[...]
    gates = gates_ref[...]
    lane = lax.broadcasted_iota(jnp.int32, gates.shape, 1)
    for e in range(eb):
        gcol = jnp.sum(jnp.where(lane == EXPERT_LANE0 + jb * eb + e, gates, 0.0), axis=-1, keepdims=True)
        hg = _dot(h2, wg_ref[e])
        hu = _dot(h2, wu_ref[e])
        he = (hg * _sigmoid(hg)) * hu * gcol
        acc_ref[...] += _dot(he.astype(BF16), wd_ref[e])

    @pl.when(jb == pl.num_programs(1) - 1)
    def _():
        x2 = x1_ref[...] + acc_ref[...]
        g = _sigmoid(_dot(_rms(x2, gple_ref[...]).astype(BF16), wpg_ref[...]))
        o_ref[...] = x2 + g * _dot(p_ref[...].astype(BF16), wpp_ref[...])


def _moe_ple(x1, h2, gates, p2d, ew, pw, tm, eb):
    n, d = x1.shape
    wg, wu, wd = ew
    row = lambda w: pl.BlockSpec((tm, w), lambda i, j: (i, 0))
    return pl.pallas_call(
        functools.partial(_moe_kernel, eb=eb), grid=(n // tm, N_EXPERTS // eb),
        in_specs=[row(d), row(d), row(LANES), row(p2d.shape[1]),
                  pl.BlockSpec((eb, d, EXPERT_FF), lambda i, j: (j, 0, 0)),
                  pl.BlockSpec((eb, d, EXPERT_FF), lambda i, j: (j, 0, 0)),
                  pl.BlockSpec((eb, EXPERT_FF, d), lambda i, j: (j, 0, 0))]
                 + [_const_spec(a.shape) for a in pw],
        out_specs=row(d), out_shape=jax.ShapeDtypeStruct((n, d), F32),
        scratch_shapes=[pltpu.VMEM((tm, d), F32)],
        compiler_params=pltpu.CompilerParams(dimension_semantics=("arbitrary", "arbitrary"),
                                             vmem_limit_bytes=VMEM_LIMIT),
        name="moe_ple",
    )(x1, h2, gates, p2d, wg, wu, wd, *pw)


def _moe_group_kernel(x1_ref, h2_ref, gates_ref, tri_ref, wg_ref, wu_ref, wd_ref, o_ref, route_ref, gsplit_ref,
                      *, tm, rows):
    g = pl.program_id(1)

    @pl.when(g == 0)
    def _():
        o_ref[...] = x1_ref[...]
        gates = gates_ref[...]
        onehot_t = gates.T[0:8]
        route_ref[0:8, :] = onehot_t
        route_ref[8:16, :] = _dot(onehot_t.astype(BF16), tri_ref[...])
        a, b, _ = _split3(gates)
        gsplit_ref[...] = jnp.concatenate([a, b], axis=1)

    member = route_ref[pl.ds(g, 1), :]
    rank = route_ref[pl.ds(8 + g, 1), :].astype(jnp.int32)
    count = jnp.sum(member).astype(jnp.int32)
    slot = lax.broadcasted_iota(jnp.int32, (rows, tm), 0)
    lane = lax.broadcasted_iota(jnp.int32, (1, LANES), 1)

    def block(b, carry):
        pick = jnp.where(slot == rank - b * rows, member, 0.0).astype(BF16)
        hc = _dot(pick, h2_ref[...]).astype(BF16)
        gc2 = _dot(pick, gsplit_ref[...])
        gc = gc2[:, :LANES] + gc2[:, LANES:]
        hidden = []
        for e in range(EXP_PER_GROUP):
            col = jnp.sum(jnp.where(lane == EXPERT_LANE0 + g * EXP_PER_GROUP + e, gc, 0.0), axis=-1, keepdims=True)
            hg = _dot(hc, wg_ref[e])
            hu = _dot(hc, wu_ref[e])
            hidden.append(((hg * _sigmoid(hg)) * hu * col).astype(BF16))
        yb = _dot(jnp.concatenate(hidden, axis=1), wd_ref[...].reshape(EXP_PER_GROUP * EXPERT_FF, -1))
        o_ref[...] += lax.dot_general(pick, yb.astype(BF16), (((0,), (0,)), ((), ())),
                                      preferred_element_type=F32)
        return carry

    lax.fori_loop(0, (count + rows - 1) // rows, block, 0)


def _moe_group(x1, h2, gates, ew, tm, rows):
    n, d = h2.shape
    wg, wu, wd = ew
    tri = (jnp.arange(tm)[:, None] < jnp.arange(tm)[None, :]).astype(BF16)
    row = lambda w: pl.BlockSpec((tm, w), lambda i, g: (i, 0))
    grp = lambda a, b: pl.BlockSpec((EXP_PER_GROUP, a, b), lambda i, g: (g, 0, 0))
    return pl.pallas_call(
        functools.partial(_moe_group_kernel, tm=tm, rows=rows), grid=(n // tm, N_GROUPS),
        in_specs=[row(d), row(d), row(LANES), _const_spec(tri.shape), grp(d, EXPERT_FF), grp(d, EXPERT_FF),
                  grp(EXPERT_FF, d)],
        out_specs=row(d), out_shape=jax.ShapeDtypeStruct((n, d), F32),
        scratch_shapes=[pltpu.VMEM((16, tm), F32), pltpu.VMEM((tm, 2 * LANES), BF16)],
        compiler_params=pltpu.CompilerParams(dimension_semantics=("arbitrary", "arbitrary"),
                                             vmem_limit_bytes=VMEM_LIMIT),
        name="moe_group",
    )(x1, h2, gates, tri, wg, wu, wd)


def _ple_kernel(x2_ref, p_ref, gple_ref, wpg_ref, wpp_ref, o_ref):
    x2 = x2_ref[...]
    g = _sigmoid(_dot(_rms(x2, gple_ref[...]).astype(BF16), wpg_ref[...]))
    o_ref[...] = x2 + g * _dot(p_ref[...].astype(BF16), wpp_ref[...])


def _ple(x2, p2d, pw, tm):
    n, d = x2.shape
    row = lambda w: pl.BlockSpec((tm, w), lambda i: (i, 0))
    return pl.pallas_call(
        _ple_kernel, grid=(n // tm,),
        in_specs=[row(d), row(p2d.shape[1])] + [_const_spec(a.shape) for a in pw],
        out_specs=row(d), out_shape=jax.ShapeDtypeStruct((n, d), F32),
        compiler_params=pltpu.CompilerParams(dimension_semantics=("arbitrary",), vmem_limit_bytes=VMEM_LIMIT),
        name="ple",
    )(x2, p2d, *pw)


def _tile(n, pref):
    return pref if n % pref == 0 else n


def kernel(x_prompt, x_sample, cache_k, cache_v, cache_logf, state_conv, page_table, p_prompt, p_sample, norm_mix_g, w_in, b_forget, q_norm_g, k_norm_g, w_attn_o, conv_dw_w, conv_dw_b, conv_ln_g, conv_ln_b, w_conv_o, w_out, norm_ffn_g, w_router_group, b_router_group, w_router_expert, b_router_expert, w_exp_gate, w_exp_up, w_exp_down, norm_ple_g, w_ple_gate, w_ple_proj):
    depth = w_in.shape[0]
    assert depth == 1
    li = 0
    batch, seq, d = x_prompt.shape
    bd, dec_seq, _ = x_sample.shape
    assert dec_seq == 1
    n_pages = page_table.shape[1]

    pw = _prep_proj_weights(norm_mix_g[li], w_in[li], b_forget[li], q_norm_g[li], k_norm_g[li])
    conv_w = (conv_dw_w[li], conv_dw_b[li].reshape(1, CONV_CH), conv_ln_g[li].reshape(1, CONV_CH),
              conv_ln_b[li].reshape(1, CONV_CH))
    wr = jnp.zeros((d, LANES), F32)
    wr = wr.at[:, :N_GROUPS].set(w_router_group[li]).at[:, EXPERT_LANE0:EXPERT_LANE0 + N_EXPERTS].set(
        w_router_expert[li]).astype(BF16)
    br = jnp.zeros((1, LANES), F32)
    br = br.at[0, :N_GROUPS].set(b_router_group[li]).at[0, EXPERT_LANE0:EXPERT_LANE0 + N_EXPERTS].set(
        b_router_expert[li])
    mw = (w_attn_o[li].astype(BF16), w_conv_o[li].astype(BF16), w_out[li].astype(BF16),
          norm_ffn_g[li].reshape(1, d), wr, br)
    ew = (w_exp_gate[li].astype(BF16), w_exp_up[li].astype(BF16), w_exp_down[li].astype(BF16))
    plew = (norm_ple_g[li].reshape(1, d), w_ple_gate[li].astype(BF16), w_ple_proj[li].astype(BF16))

    xp = x_prompt.reshape(batch * seq, d)
    tm = _tile(seq, 512)
    qa, ka, va, kt, vt, lf, y, sg, utail = _proj_prompt(xp, pw, conv_w, batch, seq, tm)
    xs = x_sample.reshape(bd, d)
    state_t = state_conv[li].transpose(1, 0, 2)
    qs, ks, vs, lfs, us, ys, sgs = _proj_sample(xs, pw, state_t, conv_w)
    attn, attn_s = _attn_fused(qa, ka, va, batch, seq, _tile(seq, 1024), page_table, qs, ks, vs, lfs,
                               cache_k[li], cache_v[li], cache_logf[li])
    attn = attn.reshape(batch * seq, ATTN_W)

    x1, h2, gates = _mix(xp, attn, y, sg, mw, tm)
    x2 = _moe_group(x1, h2, gates, ew, _tile(batch * seq, MOE_TILE), MOE_BLOCK_ROWS)
    yp = _ple(x2, p_prompt[li].reshape(batch * seq, -1), plew, tm)
    y_prompt = yp.reshape(batch, seq, d)
    to_rows = lambda t: t.reshape(batch, N_HEADS, HEAD_DIM, seq).transpose(0, 3, 1, 2)[None]
    new_k_prompt = to_rows(kt)
    new_v_prompt = to_rows(vt)
    new_logf_prompt = lf.reshape(1, batch, seq, N_HEADS)
    new_conv_prompt = utail[:, CONV_HALO - (CONV_K - 1):, :].reshape(1, batch, CONV_K - 1, CONV_CH)

    x1s, h2s, gates_s = _mix(xs, attn_s, ys, sgs, mw, bd)
    ysm = _moe_ple(x1s, h2s, gates_s, p_sample[li].reshape(bd, -1), ew, plew, bd, 4)
    y_sample = ysm.reshape(bd, 1, d)
    new_k_sample = ks.reshape(1, bd, 1, N_HEADS, HEAD_DIM)
    new_v_sample = vs.reshape(1, bd, 1, N_HEADS, HEAD_DIM)
    new_logf_sample = lfs.reshape(1, bd, 1, N_HEADS)
    new_conv_sample = jnp.concatenate([state_conv[li][:, 1:, :], us[:, None, :]], axis=1)[None]

    return (y_prompt, y_sample, new_k_prompt, new_v_prompt, new_logf_prompt, new_conv_prompt,
            new_k_sample, new_v_sample, new_logf_sample, new_conv_sample)
```

```python
import functools

import jax
import jax.numpy as jnp
from jax import lax
from jax.experimental import pallas as pl
from jax.experimental.pallas import tpu as pltpu

F32 = jnp.float32
BF16 = jnp.bfloat16

N_HEADS = 8
HEAD_DIM = 64
ATTN_W = N_HEADS * HEAD_DIM
CONV_CH = 512
CONV_K = 31
N_GROUPS = 4
EXP_PER_GROUP = 8
N_EXPERTS = N_GROUPS * EXP_PER_GROUP
EXPERT_FF = 256
PAGE_SIZE = 128
EPS = 1e-6

LANES = 128
SUBLANES = 8
CONV_HALO = 32
VMEM_LIMIT = 56 * 1024 * 1024
EXPERT_LANE0 = 32
LOG2E = 1.4426950408889634
AUG_K0 = 64
AUG_Q0 = 88
VT_ROWS = HEAD_DIM + 16
EXP2_HEADROOM = 100.0
MOE_TILE = 1024
MOE_BLOCK_ROWS = 320


def _const_spec(shape):
    nd = len(shape)
    return pl.BlockSpec(shape, lambda *_: (0,) * nd, pipeline_mode=pl.Buffered(1))


def _sigmoid(x):
    return 0.5 * jnp.tanh(0.5 * x) + 0.5


def _log_sigmoid(x):
    return -(jnp.maximum(-x, 0.0) + jnp.log1p(jnp.exp(-jnp.abs(x))))


def _rms(x, g):
    return x * lax.rsqrt(jnp.mean(x * x, axis=-1, keepdims=True) + EPS) * g


def _dot(a, b):
    return jnp.dot(a, b, preferred_element_type=F32)


def _project(x, gmix, wqkv, wf, wglu, wgate, bf, qg, kg, gsum):
    hb = _rms(x, gmix).astype(BF16)
    zqkv = _dot(hb, wqkv)
    zq = zqkv[:, :ATTN_W]
    zk = zqkv[:, ATTN_W:2 * ATTN_W]
    zv = zqkv[:, 2 * ATTN_W:]

    def head_norm(z, g):
        ss = _dot((z * z).astype(BF16), gsum)
        return z * lax.rsqrt(ss * (1.0 / HEAD_DIM) + EPS) * g

    q = head_norm(zq, qg) * (HEAD_DIM ** -0.5 * LOG2E)
    k = head_norm(zk, kg)
    zf = _dot(hb, wf) + bf
    lane = lax.broadcasted_iota(jnp.int32, zf.shape, 1)
    logf = jnp.where(lane < N_HEADS, _log_sigmoid(zf), 0.0)
    zglu = _dot(hb, wglu)
    u = zglu[:, :CONV_CH] * _sigmoid(zglu[:, CONV_CH:])
    sg = _sigmoid(_dot(hb, wgate))
    return q, k, zv, logf, u, sg


def _ln_silu(y, g, b):
    mu = jnp.mean(y, axis=-1, keepdims=True)
    d = y - mu
    var = jnp.mean(d * d, axis=-1, keepdims=True)
    z = d * lax.rsqrt(var + EPS) * g + b
    return z * _sigmoid(z)


def _split3(x):
    a = x.astype(BF16)
    r = x - a.astype(F32)
    b = r.astype(BF16)
    c = (r - b.astype(F32)).astype(BF16)
    return a, b, c


def _proj_prompt_kernel(x_ref, gmix_ref, wqkv_ref, wf_ref, wglu_ref, wgate_ref, bf_ref, qg_ref, kg_ref,
                        gsum_ref, tri_ref, wdw_ref, bdw_ref, lng_ref, lnb_ref,
                        qa_ref, ka_ref, va_ref, kt_ref, vt_ref, lf_ref, y_ref, sg_ref, utail_ref,
                        ext_ref, carry_ref, shift_ref, *, tiles_per_seq, tm):
    i = pl.program_id(0)

    @pl.when(i % tiles_per_seq == 0)
    def _():
        ext_ref[0:CONV_HALO, :] = jnp.zeros((CONV_HALO, CONV_CH), F32)
        carry_ref[...] = jnp.zeros_like(carry_ref)

    q, k, v, logf, u, sg = _project(x_ref[...], gmix_ref[...], wqkv_ref[...], wf_ref[...], wglu_ref[...],
                                    wgate_ref[...], bf_ref[...], qg_ref[...], kg_ref[...], gsum_ref[...])
    sg_ref[...] = sg
    lf_ref[...] = logf[:, :N_HEADS]

    tri = tri_ref[...]
    c3 = _dot(tri, jnp.concatenate(_split3(logf), axis=1))
    c = c3[:, :LANES] + c3[:, LANES:2 * LANES] + c3[:, 2 * LANES:] + carry_ref[0:1, :]
    carry_ref[0:1, :] = c[tm - 1:tm, :]

    ca, cb, cc = (t.astype(F32) for t in _split3(c * LOG2E))
    tail_k = -(pltpu.roll(ca, AUG_K0, 1) + pltpu.roll(cb, AUG_K0 + 8, 1) + pltpu.roll(cc, AUG_K0 + 16, 1))
    tail_q = pltpu.roll(ca, AUG_Q0, 1) + pltpu.roll(cb, AUG_Q0 + 8, 1) + pltpu.roll(cc, AUG_Q0 + 16, 1)
    lane = lax.broadcasted_iota(jnp.int32, (1, LANES), 1)
    low = lane < HEAD_DIM
    for h in range(N_HEADS):
        blk = slice((h // 2) * LANES, (h // 2 + 1) * LANES)
        qh, kh = q[:, blk], k[:, blk]
        if h % 2:
            qh, kh = pltpu.roll(qh, HEAD_DIM, 1), pltpu.roll(kh, HEAD_DIM, 1)
        pick_k = ((lane == AUG_K0 + h) | (lane == AUG_K0 + 8 + h) | (lane == AUG_K0 + 16 + h)).astype(F32)
        pick_q = ((lane == AUG_Q0 + h) | (lane == AUG_Q0 + 8 + h) | (lane == AUG_Q0 + 16 + h)).astype(F32)
        qa_ref[0, h] = jnp.where(low, qh, tail_q + pick_k).astype(BF16)
        ka_ref[0, h] = jnp.where(low, kh, tail_k + pick_q).astype(BF16)

    kt_ref[0] = k.T
    vt = v.T
    vt_ref[0] = vt
    row = lax.broadcasted_iota(jnp.int32, (VT_ROWS - HEAD_DIM, tm), 0)
    ones_row = jnp.where(row == 0, 1.0, 0.0)
    for h in range(N_HEADS):
        va_ref[0, h] = jnp.concatenate([vt[h * HEAD_DIM:(h + 1) * HEAD_DIM], ones_row], axis=0).astype(BF16)

    ext_ref[CONV_HALO:CONV_HALO + tm, :] = u
    off = CONV_HALO - (CONV_K - 1)
    span = tm + CONV_HALO - SUBLANES
    for s in range(1, SUBLANES):
        shift_ref[s - 1] = ext_ref[s:s + span, :]
    rc = min(tm, 128)
    for r0 in range(0, tm, rc):
        for l0 in range(0, CONV_CH, LANES):
            acc = jnp.zeros((rc, LANES), F32)
            for j in range(CONV_K):
                s = (off + j) % SUBLANES
                a0 = off + j - s + r0
                src = ext_ref if s == 0 else shift_ref.at[s - 1]
                acc = acc + src[a0:a0 + rc, l0:l0 + LANES] * wdw_ref[j:j + 1, l0:l0 + LANES]
            ext_ref[CONV_HALO + tm + r0:CONV_HALO + tm + r0 + rc, l0:l0 + LANES] = acc
    yc = ext_ref[CONV_HALO + tm:CONV_HALO + 2 * tm, :] + bdw_ref[...]
    y_ref[...] = _ln_silu(yc, lng_ref[...], lnb_ref[...]).astype(BF16)
    tail = u[tm - CONV_HALO:, :]
    ext_ref[0:CONV_HALO, :] = tail
    utail_ref[0] = tail


def _proj_sample_kernel(x_ref, gmix_ref, wqkv_ref, wf_ref, wglu_ref, wgate_ref, bf_ref, qg_ref, kg_ref,
                        gsum_ref, st_ref, wdw_ref, bdw_ref, lng_ref, lnb_ref,
                        qb_ref, k32_ref, v32_ref, lf_ref, u_ref, y_ref, sg_ref):
    q, k, v, logf, u, sg = _project(x_ref[...], gmix_ref[...], wqkv_ref[...], wf_ref[...], wglu_ref[...],
                                    wgate_ref[...], bf_ref[...], qg_ref[...], kg_ref[...], gsum_ref[...])
    qb_ref[...] = q.astype(BF16)
    k32_ref[...] = k
    v32_ref[...] = v
    sg_ref[...] = sg
    lf_ref[...] = logf[:, :N_HEADS]
    u_ref[...] = u
    acc = u * wdw_ref[CONV_K - 1:CONV_K, :]
    for j in range(CONV_K - 1):
        acc = acc + st_ref[j] * wdw_ref[j:j + 1, :]
    y_ref[...] = _ln_silu(acc + bdw_ref[...], lng_ref[...], lnb_ref[...]).astype(BF16)


def _prep_proj_weights(norm_mix_g, w_in, b_forget, q_norm_g, k_norm_g):
    d = w_in.shape[0]
    o_f = 3 * ATTN_W
    o_glu = o_f + N_HEADS
    o_gate = o_glu + 2 * CONV_CH
    wqkv = w_in[:, :o_f].astype(BF16)
    wf = jnp.pad(w_in[:, o_f:o_glu], ((0, 0), (0, LANES - N_HEADS))).astype(BF16)
    wglu = w_in[:, o_glu:o_gate].astype(BF16)
    wgate = w_in[:, o_gate:].astype(BF16)
    bf = jnp.pad(b_forget, (0, LANES - N_HEADS)).reshape(1, LANES)
    qg = jnp.tile(q_norm_g, N_HEADS).reshape(1, ATTN_W)
    kg = jnp.tile(k_norm_g, N_HEADS).reshape(1, ATTN_W)
    hid = jnp.arange(ATTN_W) // HEAD_DIM
    gsum = (hid[:, None] == hid[None, :]).astype(BF16)
    return (norm_mix_g.reshape(1, d), wqkv, wf, wglu, wgate, bf, qg, kg, gsum)


def _proj_prompt(x2d, pw, conv_w, batch, seq, tm):
    n, d = x2d.shape
    tps = seq // tm
    tri = (jnp.arange(tm)[:, None] >= jnp.arange(tm)[None, :]).astype(BF16)
    wdw, bdw, lng, lnb = conv_w
    row = lambda w: pl.BlockSpec((tm, w), lambda i: (i, 0))
    in_specs = ([row(d)] + [_const_spec(a.shape) for a in pw] + [_const_spec(tri.shape)]
                + [_const_spec(a.shape) for a in conv_w])
    head_rows = jax.ShapeDtypeStruct((batch, N_HEADS, seq, LANES), BF16)
    head_rows_spec = pl.BlockSpec((1, N_HEADS, tm, LANES), lambda i: (i // tps, 0, i % tps, 0))
    seq_minor = jax.ShapeDtypeStruct((batch, ATTN_W, seq), F32)
    seq_minor_spec = pl.BlockSpec((1, ATTN_W, tm), lambda i: (i // tps, 0, i % tps))
    out_shape = (
        head_rows, head_rows, jax.ShapeDtypeStruct((batch, N_HEADS, VT_ROWS, seq), BF16), seq_minor, seq_minor,
        jax.ShapeDtypeStruct((n, N_HEADS), F32), jax.ShapeDtypeStruct((n, CONV_CH), BF16),
        jax.ShapeDtypeStruct((n, 2 * d), F32), jax.ShapeDtypeStruct((batch, CONV_HALO, CONV_CH), F32),
    )
    out_specs = (head_rows_spec, head_rows_spec,
                 pl.BlockSpec((1, N_HEADS, VT_ROWS, tm), lambda i: (i // tps, 0, 0, i % tps)),
                 seq_minor_spec, seq_minor_spec, row(N_HEADS),
                 row(CONV_CH), row(2 * d), pl.BlockSpec((1, CONV_HALO, CONV_CH), lambda i: (i // tps, 0, 0)))
    return pl.pallas_call(
        functools.partial(_proj_prompt_kernel, tiles_per_seq=tps, tm=tm),
        grid=(n // tm,), in_specs=in_specs, out_specs=out_specs, out_shape=out_shape,
        scratch_shapes=[pltpu.VMEM((CONV_HALO + 2 * tm, CONV_CH), F32), pltpu.VMEM((SUBLANES, LANES), F32),
                        pltpu.VMEM((SUBLANES - 1, tm + CONV_HALO - SUBLANES, CONV_CH), F32)],
        compiler_params=pltpu.CompilerParams(dimension_semantics=("arbitrary",), vmem_limit_bytes=VMEM_LIMIT),
        name="proj_prompt",
    )(x2d, *pw, tri, wdw, bdw, lng, lnb)


def _proj_sample(x2d, pw, state_t, conv_w):
    n, d = x2d.shape
    args = (x2d,) + tuple(pw) + (state_t,) + tuple(conv_w)
    full = lambda a: pl.BlockSpec(a.shape, lambda i, nd=a.ndim: (0,) * nd)
    out_shape = (
        jax.ShapeDtypeStruct((n, ATTN_W), BF16), jax.ShapeDtypeStruct((n, ATTN_W), F32),
        jax.ShapeDtypeStruct((n, ATTN_W), F32), jax.ShapeDtypeStruct((n, N_HEADS), F32),
        jax.ShapeDtypeStruct((n, CONV_CH), F32), jax.ShapeDtypeStruct((n, CONV_CH), BF16),
        jax.ShapeDtypeStruct((n, 2 * d), F32),
    )
    return pl.pallas_call(
        _proj_sample_kernel, grid=(1,), in_specs=[full(a) for a in args],
        out_specs=tuple(full(s) for s in out_shape), out_shape=out_shape,
        compiler_params=pltpu.CompilerParams(dimension_semantics=("arbitrary",), vmem_limit_bytes=VMEM_LIMIT),
        name="proj_sample",
    )(*args)


def _attn_fused_kernel(pt_ref, q_ref, k_ref, v_ref, qs_ref, kn_ref, vn_ref, lfn_ref, kc_hbm, vc_hbm, fc_hbm,
                       o_ref, os_ref, kbuf, vbuf, fbuf, sems, m_ref, l_ref, r_ref, acc_ref, qrep_ref, cnt_ref,
                       *, tq, pp, chunks_per_seq, n_pages, n_chunks):
    b_id, h_id, i = pl.program_id(0), pl.program_id(1), pl.program_id(2)
    heads = range(2)
    qs = [q_ref[0, h] for h in heads]

    def page_copies(n, slot):
        seq = n // chunks_per_seq
        c = n % chunks_per_seq
        copies = []
        for t in range(pp):
            idx = n_pages - 1 - jnp.minimum(c * pp + t, n_pages - 1)
            page = pt_ref[seq, idx]
            copies.append(pltpu.make_async_copy(kc_hbm.at[page], kbuf.at[slot, t], sems.at[slot, 0]))
            copies.append(pltpu.make_async_copy(vc_hbm.at[page], vbuf.at[slot, t], sems.at[slot, 1]))
            copies.append(pltpu.make_async_copy(fc_hbm.at[page], fbuf.at[slot, t], sems.at[slot, 2]))
        return copies

    @pl.when((b_id == 0) & (h_id == 0) & (i == 0))
    def _():
        cnt_ref[0] = 0
        for cp in page_copies(0, 0):
            cp.start()

    hrow = lax.broadcasted_iota(jnp.int32, (N_HEADS, ATTN_W), 0)
    hlane = lax.broadcasted_iota(jnp.int32, (N_HEADS, ATTN_W), 1) // HEAD_DIM
    diag = hrow == hlane
    srow = lax.broadcasted_iota(jnp.int32, (PAGE_SIZE, 2 * PAGE_SIZE), 0)
    scol = lax.broadcasted_iota(jnp.int32, (PAGE_SIZE, 2 * PAGE_SIZE), 1)
    suffix = jnp.where((srow > scol) | (scol >= PAGE_SIZE), 1.0, 0.0).astype(F32)

    def head_rows(h):
        return slice(h * HEAD_DIM, (h + 1) * HEAD_DIM)

    def decode_enter():
        n = cnt_ref[0]
        slot = n % 2
        seq = n // chunks_per_seq
        for cp in page_copies(n, slot):
            cp.wait()

        @pl.when(n + 1 < n_chunks)
        def _():
            for cp in page_copies(n + 1, 1 - slot):
                cp.start()

        @pl.when(n % chunks_per_seq == 0)
        def _():
            qrow = qs_ref[seq].astype(F32)
            m_ref[...] = jnp.sum(jnp.where(diag, qrow, 0.0) * kn_ref[seq], axis=-1, keepdims=True)
            l_ref[...] = jnp.ones_like(l_ref)
            r_ref[...] = lfn_ref[seq]
            qrep_ref[...] = jnp.broadcast_to(qrow, (LANES, ATTN_W)).T
            vrep = jnp.broadcast_to(vn_ref[seq], (LANES, ATTN_W)).T
            lane0 = lax.broadcasted_iota(jnp.int32, (ATTN_W, LANES), 1) == 0
            acc_ref[...] = jnp.where(lane0, vrep, 0.0)

    def decode_chunk():
        n = cnt_ref[0]
        slot = n % 2
        c = n % chunks_per_seq
        lf_all = jnp.concatenate([fbuf[slot, t] for t in range(pp)], axis=0)
        sx = jnp.dot(lf_all, suffix, precision=lax.Precision.HIGHEST, preferred_element_type=F32)
        r = r_ref[...]
        parts = []
        for t in range(pp):
            valid = c * pp + t < n_pages
            bias = r + sx[t * N_HEADS:(t + 1) * N_HEADS, :PAGE_SIZE]
            r = r + jnp.where(valid, sx[t * N_HEADS:(t + 1) * N_HEADS, PAGE_SIZE:PAGE_SIZE + 1], 0.0)
            rows = []
            for h in range(N_HEADS):
                prod = kbuf[slot, t, head_rows(h), :] * qrep_ref[head_rows(h), :]
                rows.append(jnp.sum(prod, axis=0, keepdims=True))
            parts.append(jnp.where(valid, jnp.concatenate(rows, axis=0) + bias * LOG2E, -jnp.inf))
        r_ref[...] = r
        s = jnp.concatenate(parts, axis=1)
        m = m_ref[...]
        m_new = jnp.maximum(m, jnp.max(s, axis=-1, keepdims=True))
        alpha = jnp.exp2(m - m_new)
        p = jnp.exp2(s - m_new)
        l_ref[...] = alpha * l_ref[...] + jnp.sum(p, axis=-1, keepdims=True)
        m_ref[...] = m_new
        for h in range(N_HEADS):
            pv = jnp.zeros((HEAD_DIM, PAGE_SIZE), F32)
            for t in range(pp):
                pv = pv + vbuf[slot, t, head_rows(h), :] * p[h:h + 1, t * PAGE_SIZE:(t + 1) * PAGE_SIZE]
            acc_ref[head_rows(h), :] = alpha[h:h + 1, :] * acc_ref[head_rows(h), :] + pv

    def decode_leave():
        n = cnt_ref[0]

        @pl.when(n % chunks_per_seq == chunks_per_seq - 1)
        def _():
            o = jnp.sum(acc_ref[...].T, axis=0, keepdims=True)
            inv_l = jnp.sum(jnp.where(diag, 1.0 / l_ref[...], 0.0), axis=0, keepdims=True)
            os_ref[n // chunks_per_seq] = (o * inv_l).astype(BF16)

        cnt_ref[0] = n + 1

    def scores(j, h):
        start = pl.multiple_of(j * tq, tq)
        ks = k_ref[0, h, pl.ds(start, tq), :]
        st = lax.dot_general(ks, qs[h], (((1,), (1,)), ((), ())), preferred_element_type=F32)
        return st, v_ref[0, h, :, pl.ds(start, tq)]

    def attend_two_pass(j, carry, masked):
        new = []
        for h in heads:
            m, acc = carry[h]
            st, vs = scores(j, h)
            if masked:
                key = lax.broadcasted_iota(jnp.int32, (tq, tq), 0)
                qry = lax.broadcasted_iota(jnp.int32, (tq, tq), 1)
                st = jnp.where(key <= qry, st, -jnp.inf)
            m_new = jnp.maximum(m, jnp.max(st, axis=0, keepdims=True))
            pt = jnp.exp2(st - m_new).astype(BF16)
            new.append((m_new, jnp.exp2(m - m_new) * acc + _dot(vs, pt)))
        return tuple(new)

    def attend_one_pass(j, carry):
        new, excess = [], []
        both = [scores(j, h) for h in heads]
        for h in heads:
            m, acc = carry[h]
            st, vs = both[h]
            bmax = jnp.max(st, axis=0, keepdims=True)
            pt = jnp.exp2(st - m).astype(BF16)
            m_new = jnp.maximum(m, bmax)
            new.append((m_new, (acc + _dot(vs, pt)) * jnp.exp2(m - m_new)))
            excess.append(jnp.max(bmax - m))
        risky = jnp.maximum(excess[0], excess[1]) > EXP2_HEADROOM
        return lax.cond(risky, lambda: attend_two_pass(j, carry, False), lambda: tuple(new))

    def step(j, carry, diagonal):
        decode_enter()
        decode_chunk()
        carry = attend_two_pass(j, carry, True) if diagonal else attend_one_pass(j, carry)
        decode_leave()
        return carry

    init = tuple((jnp.full((1, tq), -jnp.inf, F32), jnp.zeros((VT_ROWS, tq), F32)) for _ in heads)
    carry = step(i, init, True)
    carry = lax.fori_loop(0, i, lambda t, c: step(i - 1 - t, c, False), carry)
    outs = []
    for h in heads:
        acc = carry[h][1]
        outs.append(acc[:HEAD_DIM] / acc[HEAD_DIM:HEAD_DIM + 1])
    o_ref[0] = jnp.concatenate(outs, axis=0).T.astype(BF16)


def _attn_fused(qa, ka, va, batch, seq, tq, page_table, qb, k_new, v_new, lf_new, cache_k, cache_v, cache_logf):
    hp = N_HEADS // 2
    nq = seq // tq
    bd, n_pages = page_table.shape
    n_chunks = batch * hp * nq * (nq + 1) // 2
    assert n_chunks % bd == 0
    chunks_per_seq = n_chunks // bd
    pp = -(-n_pages // chunks_per_seq)
    n_pool = cache_k.shape[0]
    kc = cache_k.transpose(0, 2, 3, 1).reshape(n_pool, ATTN_W, PAGE_SIZE)
    vc = cache_v.transpose(0, 2, 3, 1).reshape(n_pool, ATTN_W, PAGE_SIZE)
    fc = cache_logf.transpose(0, 2, 1)
    whole = lambda shape: pl.BlockSpec(shape, lambda b, h, i, pt: (0,) * len(shape))
    hbm = pl.BlockSpec(memory_space=pl.ANY)
    grid_spec = pltpu.PrefetchScalarGridSpec(
        num_scalar_prefetch=1, grid=(batch, hp, nq),
        in_specs=[
            pl.BlockSpec((1, 2, tq, LANES), lambda b, h, i, pt: (b, h, i, 0)),
            pl.BlockSpec((1, 2, seq, LANES), lambda b, h, i, pt: (b, h, 0, 0)),
            pl.BlockSpec((1, 2, VT_ROWS, seq), lambda b, h, i, pt: (b, h, 0, 0)),
            whole((bd, 1, ATTN_W)), whole((bd, 1, ATTN_W)), whole((bd, 1, ATTN_W)), whole((bd, N_HEADS, 1)),
            hbm, hbm, hbm,
        ],
        out_specs=(pl.BlockSpec((1, tq, LANES), lambda b, h, i, pt: (b, i, h)), whole((bd, 1, ATTN_W))),
        scratch_shapes=[
            pltpu.VMEM((2, pp, ATTN_W, PAGE_SIZE), F32), pltpu.VMEM((2, pp, ATTN_W, PAGE_SIZE), F32),
            pltpu.VMEM((2, pp, N_HEADS, PAGE_SIZE), F32), pltpu.SemaphoreType.DMA((2, 3)),
            pltpu.VMEM((N_HEADS, 1), F32), pltpu.VMEM((N_HEADS, 1), F32), pltpu.VMEM((N_HEADS, 1), F32),
            pltpu.VMEM((ATTN_W, LANES), F32), pltpu.VMEM((ATTN_W, LANES), F32), pltpu.SMEM((1,), jnp.int32),
        ],
    )
    attn, attn_s = pl.pallas_call(
        functools.partial(_attn_fused_kernel, tq=tq, pp=pp, chunks_per_seq=chunks_per_seq, n_pages=n_pages,
                          n_chunks=n_chunks),
        grid_spec=grid_spec,
        out_shape=(jax.ShapeDtypeStruct((batch, seq, ATTN_W), BF16), jax.ShapeDtypeStruct((bd, 1, ATTN_W), BF16)),
        compiler_params=pltpu.CompilerParams(dimension_semantics=("arbitrary", "arbitrary", "arbitrary"),
                                             vmem_limit_bytes=VMEM_LIMIT),
        name="attn_fused",
    )(page_table, qa, ka, va, qb.reshape(bd, 1, ATTN_W), k_new.reshape(bd, 1, ATTN_W),
      v_new.reshape(bd, 1, ATTN_W), lf_new.reshape(bd, N_HEADS, 1), kc, vc, fc)
    return attn, attn_s.reshape(bd, ATTN_W)


def _mix_kernel(x_ref, a_ref, y_ref, sg_ref, wao_ref, wco_ref, wout_ref, gffn_ref, wr_ref, br_ref,
                x1_ref, h2_ref, gates_ref):
    d = x_ref.shape[-1]
    ao = _dot(a_ref[...], wao_ref[...])
    co = _dot(y_ref[...], wco_ref[...])
    merged = sg_ref[:, :d] * ao + sg_ref[:, d:] * co
    x1 = x_ref[...] + _dot(merged.astype(BF16), wout_ref[...])
    x1_ref[...] = x1
    h2 = _rms(x1, gffn_ref[...]).astype(BF16)
    h2_ref[...] = h2

    logits = _dot(h2, wr_ref[...]) + br_ref[...]
    lane = lax.broadcasted_iota(jnp.int32, logits.shape, 1)
    neg = -jnp.inf
    big = jnp.int32(4 * LANES)
    gl = jnp.where(lane < N_GROUPS, logits, neg)
    gmax = jnp.max(gl, axis=-1, keepdims=True)
    gval = 1.0 / jnp.sum(jnp.exp(gl - gmax), axis=-1, keepdims=True)
    gidx = jnp.min(jnp.where(gl == gmax, lane, big), axis=-1, keepdims=True)
    ex = lane - EXPERT_LANE0
    in_group = (ex >= gidx * EXP_PER_GROUP) & (ex < (gidx + 1) * EXP_PER_GROUP)
    el = jnp.where(in_group, logits, neg)
    v1 = jnp.max(el, axis=-1, keepdims=True)
    i1 = jnp.min(jnp.where(el == v1, lane, big), axis=-1, keepdims=True)
    el2 = jnp.where(lane == i1, neg, el)
    v2 = jnp.max(el2, axis=-1, keepdims=True)
    i2 = jnp.min(jnp.where(el2 == v2, lane, big), axis=-1, keepdims=True)
    e2 = jnp.exp(v2 - v1)
    w1 = gval / (1.0 + e2)
    w2 = gval * e2 / (1.0 + e2)
    gates_ref[...] = (jnp.where(lane == i1, w1, 0.0) + jnp.where(lane == i2, w2, 0.0)
                      + jnp.where(lane == gidx, 1.0, 0.0))


def _mix(x2d, attn, y, sg, mw, tm):
    n, d = x2d.shape
    row = lambda w: pl.BlockSpec((tm, w), lambda i: (i, 0))
    return pl.pallas_call(
        _mix_kernel, grid=(n // tm,),
        in_specs=[row(d), row(ATTN_W), row(CONV_CH), row(2 * d)] + [_const_spec(a.shape) for a in mw],
        out_specs=(row(d), row(d), row(LANES)),
        out_shape=(jax.ShapeDtypeStruct((n, d), F32), jax.ShapeDtypeStruct((n, d), BF16),
                   jax.ShapeDtypeStruct((n, LANES), F32)),
        compiler_params=pltpu.CompilerParams(dimension_semantics=("arbitrary",), vmem_limit_bytes=VMEM_LIMIT),
        name="mix",
    )(x2d, attn, y, sg, *mw)


def _moe_kernel(x1_ref, h2_ref, gates_ref, p_ref, wg_ref, wu_ref, wd_ref, gple_ref, wpg_ref, wpp_ref,
                o_ref, acc_ref, *, eb):
    jb = pl.program_id(1)

    @pl.when(jb == 0)
    def _():
        acc_ref[...] = jnp.zeros_like(acc_ref)

    h2 = h2_ref[...]
    gates = gates_ref[...]
    lane = lax.broadcasted_iota(jnp.int32, gates.shape, 1)
    for e in range(eb):
        gcol = jnp.sum(jnp.where(lane == EXPERT_LANE0 + jb * eb + e, gates, 0.0), axis=-1, keepdims=True)
        hg = _dot(h2, wg_ref[e])
        hu = _dot(h2, wu_ref[e])
        he = (hg * _sigmoid(hg)) * hu * gcol
        acc_ref[...] += _dot(he.astype(BF16), wd_ref[e])

    @pl.when(jb == pl.num_programs(1) - 1)
    def _():
        x2 = x1_ref[...] + acc_ref[...]
        g = _sigmoid(_dot(_rms(x2, gple_ref[...]).astype(BF16), wpg_ref[...]))
        o_ref[...] = x2 + g * _dot(p_ref[...].astype(BF16), wpp_ref[...])


def _moe_ple(x1, h2, gates, p2d, ew, pw, tm, eb):
    n, d = x1.shape
    wg, wu, wd = ew
    row = lambda w: pl.BlockSpec((tm, w), lambda i, j: (i, 0))
    return pl.pallas_call(
        functools.partial(_moe_kernel, eb=eb), grid=(n // tm, N_EXPERTS // eb),
        in_specs=[row(d), row(d), row(LANES), row(p2d.shape[1]),
                  pl.BlockSpec((eb, d, EXPERT_FF), lambda i, j: (j, 0, 0)),
                  pl.BlockSpec((eb, d, EXPERT_FF), lambda i, j: (j, 0, 0)),
                  pl.BlockSpec((eb, EXPERT_FF, d), lambda i, j: (j, 0, 0))]
                 + [_const_spec(a.shape) for a in pw],
        out_specs=row(d), out_shape=jax.ShapeDtypeStruct((n, d), F32),
        scratch_shapes=[pltpu.VMEM((tm, d), F32)],
        compiler_params=pltpu.CompilerParams(dimension_semantics=("arbitrary", "arbitrary"),
                                             vmem_limit_bytes=VMEM_LIMIT),
        name="moe_ple",
    )(x1, h2, gates, p2d, wg, wu, wd, *pw)


def _moe_group_kernel(x1_ref, h2_ref, gates_ref, tri_ref, wg_ref, wu_ref, wd_ref, o_ref, route_ref, gsplit_ref,
                      *, tm, rows):
    g = pl.program_id(1)

    @pl.when(g == 0)
    def _():
        o_ref[...] = x1_ref[...]
        gates = gates_ref[...]
        onehot_t = gates.T[0:8]
        route_ref[0:8, :] = onehot_t
        route_ref[8:16, :] = _dot(onehot_t.astype(BF16), tri_ref[...])
        a, b, _ = _split3(gates)
        gsplit_ref[...] = jnp.concatenate([a, b], axis=1)

    member = route_ref[pl.ds(g, 1), :]
    rank = route_ref[pl.ds(8 + g, 1), :].astype(jnp.int32)
    count = jnp.sum(member).astype(jnp.int32)
    slot = lax.broadcasted_iota(jnp.int32, (rows, tm), 0)
    lane = lax.broadcasted_iota(jnp.int32, (1, LANES), 1)

    def block(b, carry):
        pick = jnp.where(slot == rank - b * rows, member, 0.0).astype(BF16)
        hc = _dot(pick, h2_ref[...]).astype(BF16)
        gc2 = _dot(pick, gsplit_ref[...])
        gc = gc2[:, :LANES] + gc2[:, LANES:]
        hidden = []
        for e in range(EXP_PER_GROUP):
            col = jnp.sum(jnp.where(lane == EXPERT_LANE0 + g * EXP_PER_GROUP + e, gc, 0.0), axis=-1, keepdims=True)
            hg = _dot(hc, wg_ref[e])
            hu = _dot(hc, wu_ref[e])
            hidden.append(((hg * _sigmoid(hg)) * hu * col).astype(BF16))
        yb = _dot(jnp.concatenate(hidden, axis=1), wd_ref[...].reshape(EXP_PER_GROUP * EXPERT_FF, -1))
        o_ref[...] += lax.dot_general(pick, yb.astype(BF16), (((0,), (0,)), ((), ())),
                                      preferred_element_type=F32)
        return carry

    lax.fori_loop(0, (count + rows - 1) // rows, block, 0)


def _moe_group(x1, h2, gates, ew, tm, rows):
    n, d = h2.shape
    wg, wu, wd = ew
    tri = (jnp.arange(tm)[:, None] < jnp.arange(tm)[None, :]).astype(BF16)
    row = lambda w: pl.BlockSpec((tm, w), lambda i, g: (i, 0))
    grp = lambda a, b: pl.BlockSpec((EXP_PER_GROUP, a, b), lambda i, g: (g, 0, 0))
    return pl.pallas_call(
        functools.partial(_moe_group_kernel, tm=tm, rows=rows), grid=(n // tm, N_GROUPS),
        in_specs=[row(d), row(d), row(LANES), _const_spec(tri.shape), grp(d, EXPERT_FF), grp(d, EXPERT_FF),
                  grp(EXPERT_FF, d)],
        out_specs=row(d), out_shape=jax.ShapeDtypeStruct((n, d), F32),
        scratch_shapes=[pltpu.VMEM((16, tm), F32), pltpu.VMEM((tm, 2 * LANES), BF16)],
        compiler_params=pltpu.CompilerParams(dimension_semantics=("arbitrary", "arbitrary"),
                                             vmem_limit_bytes=VMEM_LIMIT),
        name="moe_group",
    )(x1, h2, gates, tri, wg, wu, wd)


def _ple_kernel(x2_ref, p_ref, gple_ref, wpg_ref, wpp_ref, o_ref):
    x2 = x2_ref[...]
    g = _sigmoid(_dot(_rms(x2, gple_ref[...]).astype(BF16), wpg_ref[...]))
    o_ref[...] = x2 + g * _dot(p_ref[...].astype(BF16), wpp_ref[...])


def _ple(x2, p2d, pw, tm):
    n, d = x2.shape
    row = lambda w: pl.BlockSpec((tm, w), lambda i: (i, 0))
    return pl.pallas_call(
        _ple_kernel, grid=(n // tm,),
        in_specs=[row(d), row(p2d.shape[1])] + [_const_spec(a.shape) for a in pw],
        out_specs=row(d), out_shape=jax.ShapeDtypeStruct((n, d), F32),
        compiler_params=pltpu.CompilerParams(dimension_semantics=("arbitrary",), vmem_limit_bytes=VMEM_LIMIT),
        name="ple",
    )(x2, p2d, *pw)


def _tile(n, pref):
    return pref if n % pref == 0 else n


def kernel(x_prompt, x_sample, cache_k, cache_v, cache_logf, state_conv, page_table, p_prompt, p_sample, norm_mix_g, w_in, b_forget, q_norm_g, k_norm_g, w_attn_o, conv_dw_w, conv_dw_b, conv_ln_g, conv_ln_b, w_conv_o, w_out, norm_ffn_g, w_router_group, b_router_group, w_router_expert, b_router_expert, w_exp_gate, w_exp_up, w_exp_down, norm_ple_g, w_ple_gate, w_ple_proj):
    depth = w_in.shape[0]
    assert depth == 1
    li = 0
    batch, seq, d = x_prompt.shape
    bd, dec_seq, _ = x_sample.shape
    assert dec_seq == 1
    n_pages = page_table.shape[1]

    pw = _prep_proj_weights(norm_mix_g[li], w_in[li], b_forget[li], q_norm_g[li], k_norm_g[li])
    conv_w = (conv_dw_w[li], conv_dw_b[li].reshape(1, CONV_CH), conv_ln_g[li].reshape(1, CONV_CH),
              conv_ln_b[li].reshape(1, CONV_CH))
    wr = jnp.zeros((d, LANES), F32)
    wr = wr.at[:, :N_GROUPS].set(w_router_group[li]).at[:, EXPERT_LANE0:EXPERT_LANE0 + N_EXPERTS].set(
        w_router_expert[li]).astype(BF16)
    br = jnp.zeros((1, LANES), F32)
    br = br.at[0, :N_GROUPS].set(b_router_group[li]).at[0, EXPERT_LANE0:EXPERT_LANE0 + N_EXPERTS].set(
        b_router_expert[li])
    mw = (w_attn_o[li].astype(BF16), w_conv_o[li].astype(BF16), w_out[li].astype(BF16),
          norm_ffn_g[li].reshape(1, d), wr, br)
    ew = (w_exp_gate[li].astype(BF16), w_exp_up[li].astype(BF16), w_exp_down[li].astype(BF16))
    plew = (norm_ple_g[li].reshape(1, d), w_ple_gate[li].astype(BF16), w_ple_proj[li].astype(BF16))

    xp = x_prompt.reshape(batch * seq, d)
    tm = _tile(seq, 512)
    qa, ka, va, kt, vt, lf, y, sg, utail = _proj_prompt(xp, pw, conv_w, batch, seq, tm)
    xs = x_sample.reshape(bd, d)
    state_t = state_conv[li].transpose(1, 0, 2)
    qs, ks, vs, lfs, us, ys, sgs = _proj_sample(xs, pw, state_t, conv_w)
    attn, attn_s = _attn_fused(qa, ka, va, batch, seq, _tile(seq, 1024), page_table, qs, ks, vs, lfs,
                               cache_k[li], cache_v[li], cache_logf[li])
    attn = attn.reshape(batch * seq, ATTN_W)

    x1, h2, gates = _mix(xp, attn, y, sg, mw, tm)
    x2 = _moe_group(x1, h2, gates, ew, _tile(batch * seq, MOE_TILE), MOE_BLOCK_ROWS)
    yp = _ple(x2, p_prompt[li].reshape(batch * seq, -1), plew, tm)
    y_prompt = yp.reshape(batch, seq, d)
    to_rows = lambda t: t.reshape(batch, N_HEADS, HEAD_DIM, seq).transpose(0, 3, 1, 2)[None]
    new_k_prompt = to_rows(kt)
    new_v_prompt = to_rows(vt)
    new_logf_prompt = lf.reshape(1, batch, seq, N_HEADS)
    new_conv_prompt = utail[:, CONV_HALO - (CONV_K - 1):, :].reshape(1, batch, CONV_K - 1, CONV_CH)

    x1s, h2s, gates_s = _mix(xs, attn_s, ys, sgs, mw, bd)
    ysm = _moe_ple(x1s, h2s, gates_s, p_sample[li].reshape(bd, -1), ew, plew, bd, 4)
    y_sample = ysm.reshape(bd, 1, d)
    new_k_sample = ks.reshape(1, bd, 1, N_HEADS, HEAD_DIM)
    new_v_sample = vs.reshape(1, bd, 1, N_HEADS, HEAD_DIM)
    new_logf_sample = lfs.reshape(1, bd, 1, N_HEADS)
    new_conv_sample = jnp.concatenate([state_conv[li][:, 1:, :], us[:, None, :]], axis=1)[None]

    return (y_prompt, y_sample, new_k_prompt, new_v_prompt, new_logf_prompt, new_conv_prompt,
            new_k_sample, new_v_sample, new_logf_sample, new_conv_sample)
```

```python
import functools

import jax
import jax.numpy as jnp
from jax import lax
from jax.experimental import pallas as pl
from jax.experimental.pallas import tpu as pltpu

F32 = jnp.float32
BF16 = jnp.bfloat16

N_HEADS = 8
HEAD_DIM = 64
ATTN_W = N_HEADS * HEAD_DIM
CONV_CH = 512
CONV_K = 31
N_GROUPS = 4
EXP_PER_GROUP = 8
N_EXPERTS = N_GROUPS * EXP_PER_GROUP
EXPERT_FF = 256
PAGE_SIZE = 128
EPS = 1e-6

LANES = 128
SUBLANES = 8
CONV_HALO = 32
VMEM_LIMIT = 56 * 1024 * 1024
EXPERT_LANE0 = 32
LOG2E = 1.4426950408889634
AUG_K0 = 64
AUG_Q0 = 88
VT_ROWS = HEAD_DIM + 16
EXP2_HEADROOM = 100.0
MOE_TILE = 1024
MOE_BLOCK_ROWS = 320


def _const_spec(shape):
    nd = len(shape)
    return pl.BlockSpec(shape, lambda *_: (0,) * nd, pipeline_mode=pl.Buffered(1))


def _sigmoid(x):
    return 0.5 * jnp.tanh(0.5 * x) + 0.5


def _log_sigmoid(x):
    return -(jnp.maximum(-x, 0.0) + jnp.log1p(jnp.exp(-jnp.abs(x))))


def _rms(x, g):
    return x * lax.rsqrt(jnp.mean(x * x, axis=-1, keepdims=True) + EPS) * g


def _dot(a, b):
    return jnp.dot(a, b, preferred_element_type=F32)


def _project(x, gmix, wqkv, wf, wglu, wgate, bf, qg, kg, gsum):
    hb = _rms(x, gmix).astype(BF16)
    zqkv = _dot(hb, wqkv)
    zq = zqkv[:, :ATTN_W]
    zk = zqkv[:, ATTN_W:2 * ATTN_W]
    zv = zqkv[:, 2 * ATTN_W:]

    def head_norm(z, g):
        ss = _dot((z * z).astype(BF16), gsum)
        return z * lax.rsqrt(ss * (1.0 / HEAD_DIM) + EPS) * g

    q = head_norm(zq, qg) * (HEAD_DIM ** -0.5 * LOG2E)
    k = head_norm(zk, kg)
    zf = _dot(hb, wf) + bf
    lane = lax.broadcasted_iota(jnp.int32, zf.shape, 1)
    logf = jnp.where(lane < N_HEADS, _log_sigmoid(zf), 0.0)
    zglu = _dot(hb, wglu)
    u = zglu[:, :CONV_CH] * _sigmoid(zglu[:, CONV_CH:])
    sg = _sigmoid(_dot(hb, wgate))
    return q, k, zv, logf, u, sg


def _ln_silu(y, g, b):
    mu = jnp.mean(y, axis=-1, keepdims=True)
    d = y - mu
    var = jnp.mean(d * d, axis=-1, keepdims=True)
    z = d * lax.rsqrt(var + EPS) * g + b
    return z * _sigmoid(z)


def _split3(x):
    a = x.astype(BF16)
    r = x - a.astype(F32)
    b = r.astype(BF16)
    c = (r - b.astype(F32)).astype(BF16)
    return a, b, c


def _proj_prompt_kernel(x_ref, gmix_ref, wqkv_ref, wf_ref, wglu_ref, wgate_ref, bf_ref, qg_ref, kg_ref,
                        gsum_ref, tri_ref, wdw_ref, bdw_ref, lng_ref, lnb_ref,
                        qa_ref, ka_ref, va_ref, kt_ref, vt_ref, lf_ref, y_ref, sg_ref, utail_ref,
                        ext_ref, carry_ref, shift_ref, *, tiles_per_seq, tm):
    i = pl.program_id(0)

    @pl.when(i % tiles_per_seq == 0)
    def _():
        ext_ref[0:CONV_HALO, :] = jnp.zeros((CONV_HALO, CONV_CH), F32)
        carry_ref[...] = jnp.zeros_like(carry_ref)

    q, k, v, logf, u, sg = _project(x_ref[...], gmix_ref[...], wqkv_ref[...], wf_ref[...], wglu_ref[...],
                                    wgate_ref[...], bf_ref[...], qg_ref[...], kg_ref[...], gsum_ref[...])
    sg_ref[...] = sg
    lf_ref[...] = logf[:, :N_HEADS]

    tri = tri_ref[...]
    c3 = _dot(tri, jnp.concatenate(_split3(logf), axis=1))
    c = c3[:, :LANES] + c3[:, LANES:2 * LANES] + c3[:, 2 * LANES:] + carry_ref[0:1, :]
    carry_ref[0:1, :] = c[tm - 1:tm, :]

    ca, cb, cc = (t.astype(F32) for t in _split3(c * LOG2E))
    tail_k = -(pltpu.roll(ca, AUG_K0, 1) + pltpu.roll(cb, AUG_K0 + 8, 1) + pltpu.roll(cc, AUG_K0 + 16, 1))
    tail_q = pltpu.roll(ca, AUG_Q0, 1) + pltpu.roll(cb, AUG_Q0 + 8, 1) + pltpu.roll(cc, AUG_Q0 + 16, 1)
    lane = lax.broadcasted_iota(jnp.int32, (1, LANES), 1)
    low = lane < HEAD_DIM
    for h in range(N_HEADS):
        blk = slice((h // 2) * LANES, (h // 2 + 1) * LANES)
        qh, kh = q[:, blk], k[:, blk]
        if h % 2:
            qh, kh = pltpu.roll(qh, HEAD_DIM, 1), pltpu.roll(kh, HEAD_DIM, 1)
        pick_k = ((lane == AUG_K0 + h) | (lane == AUG_K0 + 8 + h) | (lane == AUG_K0 + 16 + h)).astype(F32)
        pick_q = ((lane == AUG_Q0 + h) | (lane == AUG_Q0 + 8 + h) | (lane == AUG_Q0 + 16 + h)).astype(F32)
        qa_ref[0, h] = jnp.where(low, qh, tail_q + pick_k).astype(BF16)
        ka_ref[0, h] = jnp.where(low, kh, tail_k + pick_q).astype(BF16)

    kt_ref[0] = k.T
    vt = v.T
    vt_ref[0] = vt
    row = lax.broadcasted_iota(jnp.int32, (VT_ROWS - HEAD_DIM, tm), 0)
    ones_row = jnp.where(row == 0, 1.0, 0.0)
    for h in range(N_HEADS):
        va_ref[0, h] = jnp.concatenate([vt[h * HEAD_DIM:(h + 1) * HEAD_DIM], ones_row], axis=0).astype(BF16)

    ext_ref[CONV_HALO:CONV_HALO + tm, :] = u
    off = CONV_HALO - (CONV_K - 1)
    span = tm + CONV_HALO - SUBLANES
    for s in range(1, SUBLANES):
        shift_ref[s - 1] = ext_ref[s:s + span, :]
    rc = min(tm, 128)
    for r0 in range(0, tm, rc):
        for l0 in range(0, CONV_CH, LANES):
            acc = jnp.zeros((rc, LANES), F32)
            for j in range(CONV_K):
                s = (off + j) % SUBLANES
                a0 = off + j - s + r0
                src = ext_ref if s == 0 else shift_ref.at[s - 1]
                acc = acc + src[a0:a0 + rc, l0:l0 + LANES] * wdw_ref[j:j + 1, l0:l0 + LANES]
            ext_ref[CONV_HALO + tm + r0:CONV_HALO + tm + r0 + rc, l0:l0 + LANES] = acc
    yc = ext_ref[CONV_HALO + tm:CONV_HALO + 2 * tm, :] + bdw_ref[...]
    y_ref[...] = _ln_silu(yc, lng_ref[...], lnb_ref[...]).astype(BF16)
    tail = u[tm - CONV_HALO:, :]
    ext_ref[0:CONV_HALO, :] = tail
    utail_ref[0] = tail


def _proj_sample_kernel(x_ref, gmix_ref, wqkv_ref, wf_ref, wglu_ref, wgate_ref, bf_ref, qg_ref, kg_ref,
                        gsum_ref, st_ref, wdw_ref, bdw_ref, lng_ref, lnb_ref,
                        qb_ref, k32_ref, v32_ref, lf_ref, u_ref, y_ref, sg_ref):
    q, k, v, logf, u, sg = _project(x_ref[...], gmix_ref[...], wqkv_ref[...], wf_ref[...], wglu_ref[...],
                                    wgate_ref[...], bf_ref[...], qg_ref[...], kg_ref[...], gsum_ref[...])
    qb_ref[...] = q.astype(BF16)
    k32_ref[...] = k
    v32_ref[...] = v
    sg_ref[...] = sg
    lf_ref[...] = logf[:, :N_HEADS]
    u_ref[...] = u
    acc = u * wdw_ref[CONV_K - 1:CONV_K, :]
    for j in range(CONV_K - 1):
        acc = acc + st_ref[j] * wdw_ref[j:j + 1, :]
    y_ref[...] = _ln_silu(acc + bdw_ref[...], lng_ref[...], lnb_ref[...]).astype(BF16)


def _prep_proj_weights(norm_mix_g, w_in, b_forget, q_norm_g, k_norm_g):
    d = w_in.shape[0]
    o_f = 3 * ATTN_W
    o_glu = o_f + N_HEADS
    o_gate = o_glu + 2 * CONV_CH
    wqkv = w_in[:, :o_f].astype(BF16)
    wf = jnp.pad(w_in[:, o_f:o_glu], ((0, 0), (0, LANES - N_HEADS))).astype(BF16)
    wglu = w_in[:, o_glu:o_gate].astype(BF16)
    wgate = w_in[:, o_gate:].astype(BF16)
    bf = jnp.pad(b_forget, (0, LANES - N_HEADS)).reshape(1, LANES)
    qg = jnp.tile(q_norm_g, N_HEADS).reshape(1, ATTN_W)
    kg = jnp.tile(k_norm_g, N_HEADS).reshape(1, ATTN_W)
    hid = jnp.arange(ATTN_W) // HEAD_DIM
    gsum = (hid[:, None] == hid[None, :]).astype(BF16)
    return (norm_mix_g.reshape(1, d), wqkv, wf, wglu, wgate, bf, qg, kg, gsum)


def _proj_prompt(x2d, pw, conv_w, batch, seq, tm):
    n, d = x2d.shape
    tps = seq // tm
    tri = (jnp.arange(tm)[:, None] >= jnp.arange(tm)[None, :]).astype(BF16)
    wdw, bdw, lng, lnb = conv_w
    row = lambda w: pl.BlockSpec((tm, w), lambda i: (i, 0))
    in_specs = ([row(d)] + [_const_spec(a.shape) for a in pw] + [_const_spec(tri.shape)]
                + [_const_spec(a.shape) for a in conv_w])
    head_rows = jax.ShapeDtypeStruct((batch, N_HEADS, seq, LANES), BF16)
    head_rows_spec = pl.BlockSpec((1, N_HEADS, tm, LANES), lambda i: (i // tps, 0, i % tps, 0))
    seq_minor = jax.ShapeDtypeStruct((batch, ATTN_W, seq), F32)
    seq_minor_spec = pl.BlockSpec((1, ATTN_W, tm), lambda i: (i // tps, 0, i % tps))
    out_shape = (
        head_rows, head_rows, jax.ShapeDtypeStruct((batch, N_HEADS, VT_ROWS, seq), BF16), seq_minor, seq_minor,
        jax.ShapeDtypeStruct((n, N_HEADS), F32), jax.ShapeDtypeStruct((n, CONV_CH), BF16),
        jax.ShapeDtypeStruct((n, 2 * d), F32), jax.ShapeDtypeStruct((batch, CONV_HALO, CONV_CH), F32),
    )
    out_specs = (head_rows_spec, head_rows_spec,
                 pl.BlockSpec((1, N_HEADS, VT_ROWS, tm), lambda i: (i // tps, 0, 0, i % tps)),
                 seq_minor_spec, seq_minor_spec, row(N_HEADS),
                 row(CONV_CH), row(2 * d), pl.BlockSpec((1, CONV_HALO, CONV_CH), lambda i: (i // tps, 0, 0)))
    return pl.pallas_call(
        functools.partial(_proj_prompt_kernel, tiles_per_seq=tps, tm=tm),
        grid=(n // tm,), in_specs=in_specs, out_specs=out_specs, out_shape=out_shape,
        scratch_shapes=[pltpu.VMEM((CONV_HALO + 2 * tm, CONV_CH), F32), pltpu.VMEM((SUBLANES, LANES), F32),
                        pltpu.VMEM((SUBLANES - 1, tm + CONV_HALO - SUBLANES, CONV_CH), F32)],
        compiler_params=pltpu.CompilerParams(dimension_semantics=("arbitrary",), vmem_limit_bytes=VMEM_LIMIT),
        name="proj_prompt",
    )(x2d, *pw, tri, wdw, bdw, lng, lnb)


def _proj_sample(x2d, pw, state_t, conv_w):
    n, d = x2d.shape
    args = (x2d,) + tuple(pw) + (state_t,) + tuple(conv_w)
    full = lambda a: pl.BlockSpec(a.shape, lambda i, nd=a.ndim: (0,) * nd)
    out_shape = (
        jax.ShapeDtypeStruct((n, ATTN_W), BF16), jax.ShapeDtypeStruct((n, ATTN_W), F32),
        jax.ShapeDtypeStruct((n, ATTN_W), F32), jax.ShapeDtypeStruct((n, N_HEADS), F32),
        jax.ShapeDtypeStruct((n, CONV_CH), F32), jax.ShapeDtypeStruct((n, CONV_CH), BF16),
        jax.ShapeDtypeStruct((n, 2 * d), F32),
    )
    return pl.pallas_call(
        _proj_sample_kernel, grid=(1,), in_specs=[full(a) for a in args],
        out_specs=tuple(full(s) for s in out_shape), out_shape=out_shape,
        compiler_params=pltpu.CompilerParams(dimension_semantics=("arbitrary",), vmem_limit_bytes=VMEM_LIMIT),
        name="proj_sample",
    )(*args)


def _attn_fused_kernel(pt_ref, q_ref, k_ref, v_ref, qs_ref, kn_ref, vn_ref, lfn_ref, kc_hbm, vc_hbm, fc_hbm,
                       o_ref, os_ref, kbuf, vbuf, fbuf, sems, m_ref, l_ref, r_ref, acc_ref, qrep_ref, cnt_ref,
                       *, tq, pp, chunks_per_seq, n_pages, n_chunks):
    b_id, h_id, i = pl.program_id(0), pl.program_id(1), pl.program_id(2)
    heads = range(2)
    qs = [q_ref[0, h] for h in heads]

    def page_copies(n, slot):
        seq = n // chunks_per_seq
        c = n % chunks_per_seq
        copies = []
        for t in range(pp):
            idx = n_pages - 1 - jnp.minimum(c * pp + t, n_pages - 1)
            page = pt_ref[seq, idx]
            copies.append(pltpu.make_async_copy(kc_hbm.at[page], kbuf.at[slot, t], sems.at[slot, 0]))
            copies.append(pltpu.make_async_copy(vc_hbm.at[page], vbuf.at[slot, t], sems.at[slot, 1]))
            copies.append(pltpu.make_async_copy(fc_hbm.at[page], fbuf.at[slot, t], sems.at[slot, 2]))
        return copies

    @pl.when((b_id == 0) & (h_id == 0) & (i == 0))
    def _():
        cnt_ref[0] = 0
        for cp in page_copies(0, 0):
            cp.start()

    hrow = lax.broadcasted_iota(jnp.int32, (N_HEADS, ATTN_W), 0)
    hlane = lax.broadcasted_iota(jnp.int32, (N_HEADS, ATTN_W), 1) // HEAD_DIM
    diag = hrow == hlane
    srow = lax.broadcasted_iota(jnp.int32, (PAGE_SIZE, 2 * PAGE_SIZE), 0)
    scol = lax.broadcasted_iota(jnp.int32, (PAGE_SIZE, 2 * PAGE_SIZE), 1)
    suffix = jnp.where((srow > scol) | (scol >= PAGE_SIZE), 1.0, 0.0).astype(F32)

    def head_rows(h):
        return slice(h * HEAD_DIM, (h + 1) * HEAD_DIM)

    def decode_enter():
        n = cnt_ref[0]
        slot = n % 2
        seq = n // chunks_per_seq
        for cp in page_copies(n, slot):
            cp.wait()

        @pl.when(n + 1 < n_chunks)
        def _():
            for cp in page_copies(n + 1, 1 - slot):
                cp.start()

        @pl.when(n % chunks_per_seq == 0)
        def _():
            qrow = qs_ref[seq].astype(F32)
            m_ref[...] = jnp.sum(jnp.where(diag, qrow, 0.0) * kn_ref[seq], axis=-1, keepdims=True)
            l_ref[...] = jnp.ones_like(l_ref)
            r_ref[...] = lfn_ref[seq]
            qrep_ref[...] = jnp.broadcast_to(qrow, (LANES, ATTN_W)).T
            vrep = jnp.broadcast_to(vn_ref[seq], (LANES, ATTN_W)).T
            lane0 = lax.broadcasted_iota(jnp.int32, (ATTN_W, LANES), 1) == 0
            acc_ref[...] = jnp.where(lane0, vrep, 0.0)

    def decode_chunk():
        n = cnt_ref[0]
        slot = n % 2
        c = n % chunks_per_seq
        lf_all = jnp.concatenate([fbuf[slot, t] for t in range(pp)], axis=0)
        sx = jnp.dot(lf_all, suffix, precision=lax.Precision.HIGHEST, preferred_element_type=F32)
        r = r_ref[...]
        parts = []
        for t in range(pp):
            valid = c * pp + t < n_pages
            bias = r + sx[t * N_HEADS:(t + 1) * N_HEADS, :PAGE_SIZE]
            r = r + jnp.where(valid, sx[t * N_HEADS:(t + 1) * N_HEADS, PAGE_SIZE:PAGE_SIZE + 1], 0.0)
            rows = []
            for h in range(N_HEADS):
                prod = kbuf[slot, t, head_rows(h), :] * qrep_ref[head_rows(h), :]
                rows.append(jnp.sum(prod, axis=0, keepdims=True))
            parts.append(jnp.where(valid, jnp.concatenate(rows, axis=0) + bias * LOG2E, -jnp.inf))
        r_ref[...] = r
        s = jnp.concatenate(parts, axis=1)
        m = m_ref[...]
        m_new = jnp.maximum(m, jnp.max(s, axis=-1, keepdims=True))
        alpha = jnp.exp2(m - m_new)
        p = jnp.exp2(s - m_new)
        l_ref[...] = alpha * l_ref[...] + jnp.sum(p, axis=-1, keepdims=True)
        m_ref[...] = m_new
        for h in range(N_HEADS):
            pv = jnp.zeros((HEAD_DIM, PAGE_SIZE), F32)
            for t in range(pp):
                pv = pv + vbuf[slot, t, head_rows(h), :] * p[h:h + 1, t * PAGE_SIZE:(t + 1) * PAGE_SIZE]
            acc_ref[head_rows(h), :] = alpha[h:h + 1, :] * acc_ref[head_rows(h), :] + pv

    def decode_leave():
        n = cnt_ref[0]

        @pl.when(n % chunks_per_seq == chunks_per_seq - 1)
        def _():
            o = jnp.sum(acc_ref[...].T, axis=0, keepdims=True)
            inv_l = jnp.sum(jnp.where(diag, 1.0 / l_ref[...], 0.0), axis=0, keepdims=True)
            os_ref[n // chunks_per_seq] = (o * inv_l).astype(BF16)

        cnt_ref[0] = n + 1

    def scores(j, h):
        start = pl.multiple_of(j * tq, tq)
        ks = k_ref[0, h, pl.ds(start, tq), :]
        st = lax.dot_general(ks, qs[h], (((1,), (1,)), ((), ())), preferred_element_type=F32)
        return st, v_ref[0, h, :, pl.ds(start, tq)]

    def attend_two_pass(j, carry, masked):
        new = []
        for h in heads:
            m, acc = carry[h]
            st, vs = scores(j, h)
            if masked:
                key = lax.broadcasted_iota(jnp.int32, (tq, tq), 0)
                qry = lax.broadcasted_iota(jnp.int32, (tq, tq), 1)
                st = jnp.where(key <= qry, st, -jnp.inf)
            m_new = jnp.maximum(m, jnp.max(st, axis=0, keepdims=True))
            pt = jnp.exp2(st - m_new).astype(BF16)
            new.append((m_new, jnp.exp2(m - m_new) * acc + _dot(vs, pt)))
        return tuple(new)

    def attend_one_pass(j, carry):
        new, excess = [], []
        both = [scores(j, h) for h in heads]
        for h in heads:
            m, acc = carry[h]
            st, vs = both[h]
            bmax = jnp.max(st, axis=0, keepdims=True)
            pt = jnp.exp2(st - m).astype(BF16)
            m_new = jnp.maximum(m, bmax)
            new.append((m_new, (acc + _dot(vs, pt)) * jnp.exp2(m - m_new)))
            excess.append(jnp.max(bmax - m))
        risky = jnp.maximum(excess[0], excess[1]) > EXP2_HEADROOM
        return lax.cond(risky, lambda: attend_two_pass(j, carry, False), lambda: tuple(new))

    def step(j, carry, diagonal):
        decode_enter()
        decode_chunk()
        carry = attend_two_pass(j, carry, True) if diagonal else attend_one_pass(j, carry)
        decode_leave()
        return carry

    init = tuple((jnp.full((1, tq), -jnp.inf, F32), jnp.zeros((VT_ROWS, tq), F32)) for _ in heads)
    carry = step(i, init, True)
    carry = lax.fori_loop(0, i, lambda t, c: step(i - 1 - t, c, False), carry)
    outs = []
    for h in heads:
        acc = carry[h][1]
        outs.append(acc[:HEAD_DIM] / acc[HEAD_DIM:HEAD_DIM + 1])
    o_ref[0] = jnp.concatenate(outs, axis=0).T.astype(BF16)


def _attn_fused(qa, ka, va, batch, seq, tq, page_table, qb, k_new, v_new, lf_new, cache_k, cache_v, cache_logf):
    hp = N_HEADS // 2
    nq = seq // tq
    bd, n_pages = page_table.shape
    n_chunks = batch * hp * nq * (nq + 1) // 2
    assert n_chunks % bd == 0
    chunks_per_seq = n_chunks // bd
    pp = -(-n_pages // chunks_per_seq)
    n_pool = cache_k.shape[0]
    kc = cache_k.transpose(0, 2, 3, 1).reshape(n_pool, ATTN_W, PAGE_SIZE)
    vc = cache_v.transpose(0, 2, 3, 1).reshape(n_pool, ATTN_W, PAGE_SIZE)
    fc = cache_logf.transpose(0, 2, 1)
    whole = lambda shape: pl.BlockSpec(shape, lambda b, h, i, pt: (0,) * len(shape))
    hbm = pl.BlockSpec(memory_space=pl.ANY)
    grid_spec = pltpu.PrefetchScalarGridSpec(
        num_scalar_prefetch=1, grid=(batch, hp, nq),
        in_specs=[
            pl.BlockSpec((1, 2, tq, LANES), lambda b, h, i, pt: (b, h, i, 0)),
            pl.BlockSpec((1, 2, seq, LANES), lambda b, h, i, pt: (b, h, 0, 0)),
            pl.BlockSpec((1, 2, VT_ROWS, seq), lambda b, h, i, pt: (b, h, 0, 0)),
            whole((bd, 1, ATTN_W)), whole((bd, 1, ATTN_W)), whole((bd, 1, ATTN_W)), whole((bd, N_HEADS, 1)),
            hbm, hbm, hbm,
        ],
        out_specs=(pl.BlockSpec((1, tq, LANES), lambda b, h, i, pt: (b, i, h)), whole((bd, 1, ATTN_W))),
        scratch_shapes=[
            pltpu.VMEM((2, pp, ATTN_W, PAGE_SIZE), F32), pltpu.VMEM((2, pp, ATTN_W, PAGE_SIZE), F32),
            pltpu.VMEM((2, pp, N_HEADS, PAGE_SIZE), F32), pltpu.SemaphoreType.DMA((2, 3)),
            pltpu.VMEM((N_HEADS, 1), F32), pltpu.VMEM((N_HEADS, 1), F32), pltpu.VMEM((N_HEADS, 1), F32),
            pltpu.VMEM((ATTN_W, LANES), F32), pltpu.VMEM((ATTN_W, LANES), F32), pltpu.SMEM((1,), jnp.int32),
        ],
    )
    attn, attn_s = pl.pallas_call(
        functools.partial(_attn_fused_kernel, tq=tq, pp=pp, chunks_per_seq=chunks_per_seq, n_pages=n_pages,
                          n_chunks=n_chunks),
        grid_spec=grid_spec,
        out_shape=(jax.ShapeDtypeStruct((batch, seq, ATTN_W), BF16), jax.ShapeDtypeStruct((bd, 1, ATTN_W), BF16)),
        compiler_params=pltpu.CompilerParams(dimension_semantics=("arbitrary", "arbitrary", "arbitrary"),
                                             vmem_limit_bytes=VMEM_LIMIT),
        name="attn_fused",
    )(page_table, qa, ka, va, qb.reshape(bd, 1, ATTN_W), k_new.reshape(bd, 1, ATTN_W),
      v_new.reshape(bd, 1, ATTN_W), lf_new.reshape(bd, N_HEADS, 1), kc, vc, fc)
    return attn, attn_s.reshape(bd, ATTN_W)


def _mix_kernel(x_ref, a_ref, y_ref, sg_ref, wao_ref, wco_ref, wout_ref, gffn_ref, wr_ref, br_ref,
                x1_ref, h2_ref, gates_ref):
    d = x_ref.shape[-1]
    ao = _dot(a_ref[...], wao_ref[...])
    co = _dot(y_ref[...], wco_ref[...])
    merged = sg_ref[:, :d] * ao + sg_ref[:, d:] * co
    x1 = x_ref[...] + _dot(merged.astype(BF16), wout_ref[...])
    x1_ref[...] = x1
    h2 = _rms(x1, gffn_ref[...]).astype(BF16)
    h2_ref[...] = h2

    logits = _dot(h2, wr_ref[...]) + br_ref[...]
    lane = lax.broadcasted_iota(jnp.int32, logits.shape, 1)
    neg = -jnp.inf
    big = jnp.int32(4 * LANES)
    gl = jnp.where(lane < N_GROUPS, logits, neg)
    gmax = jnp.max(gl, axis=-1, keepdims=True)
    gval = 1.0 / jnp.sum(jnp.exp(gl - gmax), axis=-1, keepdims=True)
    gidx = jnp.min(jnp.where(gl == gmax, lane, big), axis=-1, keepdims=True)
    ex = lane - EXPERT_LANE0
    in_group = (ex >= gidx * EXP_PER_GROUP) & (ex < (gidx + 1) * EXP_PER_GROUP)
    el = jnp.where(in_group, logits, neg)
    v1 = jnp.max(el, axis=-1, keepdims=True)
    i1 = jnp.min(jnp.where(el == v1, lane, big), axis=-1, keepdims=True)
    el2 = jnp.where(lane == i1, neg, el)
    v2 = jnp.max(el2, axis=-1, keepdims=True)
    i2 = jnp.min(jnp.where(el2 == v2, lane, big), axis=-1, keepdims=True)
    e2 = jnp.exp(v2 - v1)
    w1 = gval / (1.0 + e2)
    w2 = gval * e2 / (1.0 + e2)
    gates_ref[...] = (jnp.where(lane == i1, w1, 0.0) + jnp.where(lane == i2, w2, 0.0)
                      + jnp.where(lane == gidx, 1.0, 0.0))


def _mix(x2d, attn, y, sg, mw, tm):
    n, d = x2d.shape
    row = lambda w: pl.BlockSpec((tm, w), lambda i: (i, 0))
    return pl.pallas_call(
        _mix_kernel, grid=(n // tm,),
        in_specs=[row(d), row(ATTN_W), row(CONV_CH), row(2 * d)] + [_const_spec(a.shape) for a in mw],
        out_specs=(row(d), row(d), row(LANES)),
        out_shape=(jax.ShapeDtypeStruct((n, d), F32), jax.ShapeDtypeStruct((n, d), BF16),
                   jax.ShapeDtypeStruct((n, LANES), F32)),
        compiler_params=pltpu.CompilerParams(dimension_semantics=("arbitrary",), vmem_limit_bytes=VMEM_LIMIT),
        name="mix",
    )(x2d, attn, y, sg, *mw)


def _moe_kernel(x1_ref, h2_ref, gates_ref, p_ref, wg_ref, wu_ref, wd_ref, gple_ref, wpg_ref, wpp_ref,
                o_ref, acc_ref, *, eb):
    jb = pl.program_id(1)

    @pl.when(jb == 0)
    def _():
        acc_ref[...] = jnp.zeros_like(acc_ref)

    h2 = h2_ref[...]
    gates = gates_ref[...]
    lane = lax.broadcasted_iota(jnp.int32, gates.shape, 1)
    for e in range(eb):
        gcol = jnp.sum(jnp.where(lane == EXPERT_LANE0 + jb * eb + e, gates, 0.0), axis=-1, keepdims=True)
        hg = _dot(h2, wg_ref[e])
        hu = _dot(h2, wu_ref[e])
        he = (hg * _sigmoid(hg)) * hu * gcol
        acc_ref[...] += _dot(he.astype(BF16), wd_ref[e])

    @pl.when(jb == pl.num_programs(1) - 1)
    def _():
        x2 = x1_ref[...] + acc_ref[...]
        g = _sigmoid(_dot(_rms(x2, gple_ref[...]).astype(BF16), wpg_ref[...]))
        o_ref[...] = x2 + g * _dot(p_ref[...].astype(BF16), wpp_ref[...])


def _moe_ple(x1, h2, gates, p2d, ew, pw, tm, eb):
    n, d = x1.shape
    wg, wu, wd = ew
    row = lambda w: pl.BlockSpec((tm, w), lambda i, j: (i, 0))
    return pl.pallas_call(
        functools.partial(_moe_kernel, eb=eb), grid=(n // tm, N_EXPERTS // eb),
        in_specs=[row(d), row(d), row(LANES), row(p2d.shape[1]),
                  pl.BlockSpec((eb, d, EXPERT_FF), lambda i, j: (j, 0, 0)),
                  pl.BlockSpec((eb, d, EXPERT_FF), lambda i, j: (j, 0, 0)),
                  pl.BlockSpec((eb, EXPERT_FF, d), lambda i, j: (j, 0, 0))]
                 + [_const_spec(a.shape) for a in pw],
        out_specs=row(d), out_shape=jax.ShapeDtypeStruct((n, d), F32),
        scratch_shapes=[pltpu.VMEM((tm, d), F32)],
        compiler_params=pltpu.CompilerParams(dimension_semantics=("arbitrary", "arbitrary"),
                                             vmem_limit_bytes=VMEM_LIMIT),
        name="moe_ple",
    )(x1, h2, gates, p2d, wg, wu, wd, *pw)


def _moe_group_kernel(x1_ref, h2_ref, gates_ref, tri_ref, wg_ref, wu_ref, wd_ref, o_ref, route_ref, gsplit_ref,
                      *, tm, rows):
    g = pl.program_id(1)

    @pl.when(g == 0)
    def _():
        o_ref[...] = x1_ref[...]
        gates = gates_ref[...]
        onehot_t = gates.T[0:8]
        route_ref[0:8, :] = onehot_t
        route_ref[8:16, :] = _dot(onehot_t.astype(BF16), tri_ref[...])
        a, b, _ = _split3(gates)
        gsplit_ref[...] = jnp.concatenate([a, b], axis=1)

    member = route_ref[pl.ds(g, 1), :]
    rank = route_ref[pl.ds(8 + g, 1), :].astype(jnp.int32)
    count = jnp.sum(member).astype(jnp.int32)
    slot = lax.broadcasted_iota(jnp.int32, (rows, tm), 0)
    lane = lax.broadcasted_iota(jnp.int32, (1, LANES), 1)

    def block(b, carry):
        pick = jnp.where(slot == rank - b * rows, member, 0.0).astype(BF16)
        hc = _dot(pick, h2_ref[...]).astype(BF16)
        gc2 = _dot(pick, gsplit_ref[...])
        gc = gc2[:, :LANES] + gc2[:, LANES:]
        hidden = []
        for e in range(EXP_PER_GROUP):
            col = jnp.sum(jnp.where(lane == EXPERT_LANE0 + g * EXP_PER_GROUP + e, gc, 0.0), axis=-1, keepdims=True)
            hg = _dot(hc, wg_ref[e])
            hu = _dot(hc, wu_ref[e])
            hidden.append(((hg * _sigmoid(hg)) * hu * col).astype(BF16))
        yb = _dot(jnp.concatenate(hidden, axis=1), wd_ref[...].reshape(EXP_PER_GROUP * EXPERT_FF, -1))
        o_ref[...] += lax.dot_general(pick, yb.astype(BF16), (((0,), (0,)), ((), ())),
                                      preferred_element_type=F32)
        return carry

    lax.fori_loop(0, (count + rows - 1) // rows, block, 0)


def _moe_group(x1, h2, gates, ew, tm, rows):
    n, d = h2.shape
    wg, wu, wd = ew
    tri = (jnp.arange(tm)[:, None] < jnp.arange(tm)[None, :]).astype(BF16)
    row = lambda w: pl.BlockSpec((tm, w), lambda i, g: (i, 0))
    grp = lambda a, b: pl.BlockSpec((EXP_PER_GROUP, a, b), lambda i, g: (g, 0, 0))
    return pl.pallas_call(
        functools.partial(_moe_group_kernel, tm=tm, rows=rows), grid=(n // tm, N_GROUPS),
        in_specs=[row(d), row(d), row(LANES), _const_spec(tri.shape), grp(d, EXPERT_FF), grp(d, EXPERT_FF),
                  grp(EXPERT_FF, d)],
        out_specs=row(d), out_shape=jax.ShapeDtypeStruct((n, d), F32),
        scratch_shapes=[pltpu.VMEM((16, tm), F32), pltpu.VMEM((tm, 2 * LANES), BF16)],
        compiler_params=pltpu.CompilerParams(dimension_semantics=("arbitrary", "arbitrary"),
                                             vmem_limit_bytes=VMEM_LIMIT),
        name="moe_group",
    )(x1, h2, gates, tri, wg, wu, wd)


def _ple_kernel(x2_ref, p_ref, gple_ref, wpg_ref, wpp_ref, o_ref):
    x2 = x2_ref[...]
    g = _sigmoid(_dot(_rms(x2, gple_ref[...]).astype(BF16), wpg_ref[...]))
    o_ref[...] = x2 + g * _dot(p_ref[...].astype(BF16), wpp_ref[...])


def _ple(x2, p2d, pw, tm):
    n, d = x2.shape
    row = lambda w: pl.BlockSpec((tm, w), lambda i: (i, 0))
    return pl.pallas_call(
        _ple_kernel, grid=(n // tm,),
        in_specs=[row(d), row(p2d.shape[1])] + [_const_spec(a.shape) for a in pw],
        out_specs=row(d), out_shape=jax.ShapeDtypeStruct((n, d), F32),
        compiler_params=pltpu.CompilerParams(dimension_semantics=("arbitrary",), vmem_limit_bytes=VMEM_LIMIT),
        name="ple",
    )(x2, p2d, *pw)


def _tile(n, pref):
    return pref if n % pref == 0 else n


def kernel(x_prompt, x_sample, cache_k, cache_v, cache_logf, state_conv, page_table, p_prompt, p_sample, norm_mix_g, w_in, b_forget, q_norm_g, k_norm_g, w_attn_o, conv_dw_w, conv_dw_b, conv_ln_g, conv_ln_b, w_conv_o, w_out, norm_ffn_g, w_router_group, b_router_group, w_router_expert, b_router_expert, w_exp_gate, w_exp_up, w_exp_down, norm_ple_g, w_ple_gate, w_ple_proj):
    depth = w_in.shape[0]
    assert depth == 1
    li = 0
    batch, seq, d = x_prompt.shape
    bd, dec_seq, _ = x_sample.shape
    assert dec_seq == 1
    n_pages = page_table.shape[1]

    pw = _prep_proj_weights(norm_mix_g[li], w_in[li], b_forget[li], q_norm_g[li], k_norm_g[li])
    conv_w = (conv_dw_w[li], conv_dw_b[li].reshape(1, CONV_CH), conv_ln_g[li].reshape(1, CONV_CH),
              conv_ln_b[li].reshape(1, CONV_CH))
    wr = jnp.zeros((d, LANES), F32)
    wr = wr.at[:, :N_GROUPS].set(w_router_group[li]).at[:, EXPERT_LANE0:EXPERT_LANE0 + N_EXPERTS].set(
        w_router_expert[li]).astype(BF16)
    br = jnp.zeros((1, LANES), F32)
    br = br.at[0, :N_GROUPS].set(b_router_group[li]).at[0, EXPERT_LANE0:EXPERT_LANE0 + N_EXPERTS].set(
        b_router_expert[li])
    mw = (w_attn_o[li].astype(BF16), w_conv_o[li].astype(BF16), w_out[li].astype(BF16),
          norm_ffn_g[li].reshape(1, d), wr, br)
    ew = (w_exp_gate[li].astype(BF16), w_exp_up[li].astype(BF16), w_exp_down[li].astype(BF16))
    plew = (norm_ple_g[li].reshape(1, d), w_ple_gate[li].astype(BF16), w_ple_proj[li].astype(BF16))

    xp = x_prompt.reshape(batch * seq, d)
    tm = _tile(seq, 512)
    qa, ka, va, kt, vt, lf, y, sg, utail = _proj_prompt(xp, pw, conv_w, batch, seq, tm)
    xs = x_sample.reshape(bd, d)
    state_t = state_conv[li].transpose(1, 0, 2)
    qs, ks, vs, lfs, us, ys, sgs = _proj_sample(xs, pw, state_t, conv_w)
    attn, attn_s = _attn_fused(qa, ka, va, batch, seq, _tile(seq, 1024), page_table, qs, ks, vs, lfs,
                               cache_k[li], cache_v[li], cache_logf[li])
    attn = attn.reshape(batch * seq, ATTN_W)

    wide = _tile(batch * seq, MOE_TILE)
    x1, h2, gates = _mix(xp, attn, y, sg, mw, wide)
    x2 = _moe_group(x1, h2, gates, ew, _tile(batch * seq, MOE_TILE), MOE_BLOCK_ROWS)
    yp = _ple(x2, p_prompt[li].reshape(batch * seq, -1), plew, wide)
    y_prompt = yp.reshape(batch, seq, d)
    to_rows = lambda t: t.reshape(batch, N_HEADS, HEAD_DIM, seq).transpose(0, 3, 1, 2)[None]
    new_k_prompt = to_rows(kt)
    new_v_prompt = to_rows(vt)
    new_logf_prompt = lf.reshape(1, batch, seq, N_HEADS)
    new_conv_prompt = utail[:, CONV_HALO - (CONV_K - 1):, :].reshape(1, batch, CONV_K - 1, CONV_CH)

    x1s, h2s, gates_s = _mix(xs, attn_s, ys, sgs, mw, bd)
    ysm = _moe_ple(x1s, h2s, gates_s, p_sample[li].reshape(bd, -1), ew, plew, bd, 4)
    y_sample = ysm.reshape(bd, 1, d)
    new_k_sample = ks.reshape(1, bd, 1, N_HEADS, HEAD_DIM)
    new_v_sample = vs.reshape(1, bd, 1, N_HEADS, HEAD_DIM)
    new_logf_sample = lfs.reshape(1, bd, 1, N_HEADS)
    new_conv_sample = jnp.concatenate([state_conv[li][:, 1:, :], us[:, None, :]], axis=1)[None]

    return (y_prompt, y_sample, new_k_prompt, new_v_prompt, new_logf_prompt, new_conv_prompt,
            new_k_sample, new_v_sample, new_logf_sample, new_conv_sample)
```
